```python
import jax
import jax.numpy as jnp
from jax import lax
import numpy as np

D_MODEL = 2048
BATCH = 4
SEQ = 2048
DEPTH = 4

PLE_DIM = 256
D_FF = 5632
RMS_EPS = 1e-6
N_EVEN = (DEPTH + 1) // 2
N_ODD = DEPTH // 2
N_VRES = max(N_ODD - 1, 0)

A_WIDTH = D_MODEL // 2
A_GROUP = 128
A_GROUPS = A_WIDTH // A_GROUP
A_CHUNK = 128
B_WIDTH = D_MODEL // 2
B_HEAD_DIM = 128
B_HEADS = B_WIDTH // B_HEAD_DIM
B_CHUNK = 64
B_MIN_F = 1e-30
EVEN_IN = 2 * A_WIDTH + 4 * B_WIDTH
C_HEAD = 64
C_HEADS = D_MODEL // C_HEAD
C_DECAY_LORA = 96
C_AAA_LORA = 96
C_MV_LORA = 64
C_GATE_LORA = 256
C_GN_EPS = 64e-5

kernel_name = 'hybrid_gmlp_hgrn2_rwkv7_macaron'


def rmsnorm(x, g, eps=RMS_EPS):
    xf = x.astype(jnp.float32)
    y = xf * lax.rsqrt(jnp.mean(xf * xf, axis=-1, keepdims=True) + eps)
    return (y * g.astype(jnp.float32)).astype(x.dtype)


def swiglu(x, w_gate, w_up, w_down):
    return (jax.nn.silu(x @ w_gate) * (x @ w_up)) @ w_down


def chunked_gmlp(u, v, v_gain, w_s, b_s):
    bsz, seq, _ = u.shape
    n_chunks = seq // A_CHUNK
    vg = rmsnorm(v.reshape(bsz, seq, A_GROUPS, A_GROUP), v_gain.reshape(A_GROUPS, A_GROUP))
    vg = vg.reshape(bsz, n_chunks, A_CHUNK, A_GROUPS, A_GROUP)
    causal = jnp.tril(jnp.ones((A_CHUNK, A_CHUNK), dtype=bool))
    w = jnp.where(causal[None], w_s, jnp.zeros_like(w_s))
    s = jnp.einsum('gts,bnsgc->bntgc', w, vg) + b_s.T[None, None, :, :, None]
    return u * s.reshape(bsz, seq, A_WIDTH)


def hgrn2(q, f_logit, i_in, lb):
    bsz, seq, _ = q.shape
    n_chunks = seq // B_CHUNK
    z = f_logit.astype(jnp.float32)
    lb = lb.astype(jnp.float32)
    sig = jax.nn.sigmoid(z)
    f = lb + (1.0 - lb) * sig
    log_f = jnp.log(jnp.maximum(f, B_MIN_F))
    k = 1.0 - f
    qf = jax.nn.silu(q.astype(jnp.float32))

    def to_chunks(t):
        return t.reshape(bsz, n_chunks, B_CHUNK, B_HEADS, B_HEAD_DIM).transpose(1, 0, 3, 2, 4)

    xs = (to_chunks(qf), to_chunks(k), to_chunks(i_in.astype(jnp.float32)), to_chunks(log_f))
    causal = jnp.tril(jnp.ones((B_CHUNK, B_CHUNK), dtype=bool))[:, :, None]

    def step(state, chunk):
        qb, kb, vb, gb = chunk
        cum = jnp.cumsum(gb, axis=2)
        last = cum[:, :, -1:, :]
        o_inter = jnp.einsum('bhtk,bhkv->bhtv', qb * jnp.exp(cum), state)
        rel = cum[:, :, :, None, :] - cum[:, :, None, :, :]
        dec = jnp.where(causal, jnp.exp(jnp.minimum(rel, 0.0)), 0.0)
        att = jnp.einsum('bhtk,bhsk,bhtsk->bhts', qb, kb, dec)
        o = o_inter + jnp.einsum('bhts,bhsv->bhtv', att, vb)
        new_state = jnp.exp(last[:, :, 0, :])[..., None] * state + jnp.einsum('bhsk,bhsv->bhkv', kb * jnp.exp(last - cum), vb)
        return new_state, o

    s0 = jnp.zeros((bsz, B_HEADS, B_HEAD_DIM, B_HEAD_DIM), jnp.float32)
    _, o = lax.scan(step, s0, xs)
    return o.transpose(1, 0, 3, 2, 4).reshape(bsz, seq, B_WIDTH)


def rwkv7(x, mix, w_r, w_k, w_v, w_o, w0, w1, w2, a0, a1, a2, g1, g2, k_k, k_a, r_k, gn_g, gn_b, v_first, v_res):
    bsz, seq, d = x.shape
    xx = jnp.pad(x, ((0, 0), (1, 0), (0, 0)))[:, :-1] - x
    xr, xw, xk, xv, xa, xg = [x + xx * mix[j] for j in range(6)]
    r = xr @ w_r
    k = xk @ w_k
    v = xv @ w_v
    w = -jax.nn.softplus(-(w0 + jnp.tanh(xw @ w1) @ w2)) - 0.5
    if v_res is None:
        v_first = v
    else:
        v0, v1, v2 = v_res
        v = v + (v_first - v) * jax.nn.sigmoid(v0 + (xv @ v1) @ v2)
    a = jax.nn.sigmoid(a0 + (xa @ a1) @ a2)
    g = jax.nn.sigmoid(xg @ g1) @ g2

    def heads(t):
        return t.reshape(bsz, seq, C_HEADS, C_HEAD).astype(jnp.float32)

    kk = heads(k * k_k)
    kk = kk / jnp.maximum(jnp.sqrt(jnp.sum(kk * kk, axis=-1, keepdims=True)), 1e-12)
    k = k * (1 + (a - 1) * k_a)
    decay = jnp.exp(-jnp.exp(w.astype(jnp.float32)))
    rh, kh, vh, ah = heads(r), heads(k), heads(v), heads(a)

    def step(state, inp):
        r_t, w_t, k_t, v_t, a_t, b_t = inp
        sa = jnp.einsum('bhvk,bhk->bhv', state, a_t)
        state = state * w_t[:, :, None, :] + v_t[..., None] * k_t[:, :, None, :] + sa[..., None] * b_t[:, :, None, :]
        return state, jnp.einsum('bhvk,bhk->bhv', state, r_t)

    def time_major(t):
        return jnp.moveaxis(t, 1, 0)

    xs = tuple(time_major(t) for t in (rh, heads(decay), kh, vh, -kk, kk * ah))
    s0 = jnp.zeros((bsz, C_HEADS, C_HEAD, C_HEAD), jnp.float32)
    _, y = lax.scan(step, s0, xs)
    y = jnp.moveaxis(y, 0, 1)
    mu = jnp.mean(y, axis=-1, keepdims=True)
    var = jnp.mean(jnp.square(y - mu), axis=-1, keepdims=True)
    y = ((y - mu) * lax.rsqrt(var + C_GN_EPS)).reshape(bsz, seq, d) * gn_g + gn_b
    bonus = jnp.sum(rh * kh * r_k, axis=-1, keepdims=True) * vh
    y = (y + bonus.reshape(bsz, seq, d)).astype(x.dtype)
    return ((y * g) @ w_o).astype(x.dtype), v_first


def setup_inputs(seed: int = 0) -> dict:
    key = jax.random.key(seed)
    ks = iter(jax.random.split(key, 64))
    f32 = jnp.float32

    def nrm(shape, scale):
        return jax.random.normal(next(ks), shape, f32) * scale

    def gain(shape):
        return 1.0 + nrm(shape, 0.05)

    D = D_MODEL
    return {
        'x': nrm((BATCH, SEQ, D), 1.0),
        'p': nrm((DEPTH, BATCH, SEQ, PLE_DIM), 1.0),
        'norms': gain((DEPTH, 4, D)),
        'final_norm': gain((D,)),
        'ffn_wg': nrm((DEPTH, 2, D, D_FF), D ** -0.5),
        'ffn_wu': nrm((DEPTH, 2, D, D_FF), D ** -0.5),
        'ffn_wd': nrm((DEPTH, 2, D_FF, D), 0.5 * D_FF ** -0.5),
        'ple_wp': nrm((DEPTH, PLE_DIM, D), 0.5 * PLE_DIM ** -0.5),
        'ple_wg': nrm((DEPTH, D, D), D ** -0.5),
        'e_w_in': nrm((N_EVEN, D, EVEN_IN), D ** -0.5),
        'e_w_out': nrm((N_EVEN, D, D), 0.5 * D ** -0.5),
        'a_vnorm': gain((N_EVEN, A_WIDTH)),
        'a_ws': nrm((N_EVEN, A_GROUPS, A_CHUNK, A_CHUNK), 0.5 * A_CHUNK ** -0.5),
        'a_bs': 1.0 + nrm((N_EVEN, A_GROUPS, A_CHUNK), 0.1),
        'b_onorm': gain((N_EVEN, B_WIDTH)),
        'b_lb_logits': nrm((DEPTH, B_WIDTH), 0.5),
        'c_mix': jax.random.uniform(next(ks), (N_ODD, 6, D), f32),
        'c_wr': nrm((N_ODD, D, D), D ** -0.5),
        'c_wk': nrm((N_ODD, D, D), D ** -0.5),
        'c_wv': nrm((N_ODD, D, D), D ** -0.5),
        'c_wo': nrm((N_ODD, D, D), 0.5 * D ** -0.5),
        'c_w0': jax.random.uniform(next(ks), (N_ODD, D), f32, -6.0, -1.0),
        'c_w1': nrm((N_ODD, D, C_DECAY_LORA), D ** -0.5),
        'c_w2': nrm((N_ODD, C_DECAY_LORA, D), 0.5 * C_DECAY_LORA ** -0.5),
        'c_a0': nrm((N_ODD, D), 0.5),
        'c_a1': nrm((N_ODD, D, C_AAA_LORA), D ** -0.5),
        'c_a2': nrm((N_ODD, C_AAA_LORA, D), 0.5 * C_AAA_LORA ** -0.5),
        'c_g1': nrm((N_ODD, D, C_GATE_LORA), D ** -0.5),
        'c_g2': nrm((N_ODD, C_GATE_LORA, D), C_GATE_LORA ** -0.5),
        'c_kk': 0.85 + nrm((N_ODD, D), 0.05),
        'c_ka': 1.0 + nrm((N_ODD, D), 0.05),
        'c_rk': nrm((N_ODD, C_HEADS, C_HEAD), 0.1),
        'c_gn_g': gain((N_ODD, D)),
        'c_gn_b': nrm((N_ODD, D), 0.01),
        'c_v0': 1.0 + nrm((N_VRES, D), 0.1),
        'c_v1': nrm((N_VRES, D, C_MV_LORA), 0.5 * D ** -0.5),
        'c_v2': nrm((N_VRES, C_MV_LORA, D), 0.5 * C_MV_LORA ** -0.5),
    }


def reference(x, p, norms, final_norm, ffn_wg, ffn_wu, ffn_wd, ple_wp, ple_wg, e_w_in, e_w_out,
              a_vnorm, a_ws, a_bs, b_onorm, b_lb_logits, c_mix, c_wr, c_wk, c_wv, c_wo, c_w0, c_w1, c_w2,
              c_a0, c_a1, c_a2, c_g1, c_g2, c_kk, c_ka, c_rk, c_gn_g, c_gn_b, c_v0, c_v1, c_v2):
    probs = jax.nn.softmax(b_lb_logits.astype(jnp.float32), axis=0)
    lower_bounds = jnp.cumsum(probs, axis=0) - probs[0]
    split_at = [A_WIDTH, 2 * A_WIDTH, 2 * A_WIDTH + B_WIDTH, 2 * A_WIDTH + 2 * B_WIDTH, 2 * A_WIDTH + 3 * B_WIDTH]
    h = x
    v_first = None
    for i in range(DEPTH):
        j = i // 2
        h = h + 0.5 * swiglu(rmsnorm(h, norms[i, 0]), ffn_wg[i, 0], ffn_wu[i, 0], ffn_wd[i, 0])
        hn = rmsnorm(h, norms[i, 1])
        if i % 2 == 0:
            proj = hn @ e_w_in[j]
            au, av, bq, bf, bi, bg = jnp.split(proj, split_at, axis=-1)
            a_out = chunked_gmlp(jax.nn.gelu(au), jax.nn.gelu(av), a_vnorm[j], a_ws[j], a_bs[j])
            b_o = hgrn2(bq, bf, bi, lower_bounds[i]).astype(hn.dtype)
            bsz, seq, _ = b_o.shape
            b_o = rmsnorm(b_o.reshape(bsz, seq, B_HEADS, B_HEAD_DIM), b_onorm[j].reshape(B_HEADS, B_HEAD_DIM))
            b_out = b_o.reshape(bsz, seq, B_WIDTH) * jax.nn.silu(bg)
            mixed = jnp.concatenate([a_out.astype(hn.dtype), b_out.astype(hn.dtype)], axis=-1) @ e_w_out[j]
        else:
            v_res = None if j == 0 else (c_v0[j - 1], c_v1[j - 1], c_v2[j - 1])
            mixed, v_first = rwkv7(hn, c_mix[j], c_wr[j], c_wk[j], c_wv[j], c_wo[j], c_w0[j], c_w1[j], c_w2[j],
                                   c_a0[j], c_a1[j], c_a2[j], c_g1[j], c_g2[j], c_kk[j], c_ka[j], c_rk[j],
                                   c_gn_g[j], c_gn_b[j], v_first, v_res)
        h = h + mixed.astype(h.dtype)
        h = h + 0.5 * swiglu(rmsnorm(h, norms[i, 2]), ffn_wg[i, 1], ffn_wu[i, 1], ffn_wd[i, 1])
        gate = jax.nn.sigmoid(rmsnorm(h, norms[i, 3]) @ ple_wg[i])
        h = h + gate * (p[i] @ ple_wp[i])
    return rmsnorm(h, final_norm)
```

```python
import functools

import jax
import jax.numpy as jnp
from jax import lax
from jax.experimental import pallas as pl
from jax.experimental.pallas import tpu as pltpu

F32 = jnp.float32
BF16 = jnp.bfloat16

LANES = 128
RMS_EPS = 1e-6
A_CHUNK = 128
B_HEAD = 128
B_MIN_F = 1e-30
C_HEAD = 64
C_GN_EPS = 64e-5
LORA_PAD = 128

VMEM_LIMIT = 56 * 1024 * 1024

_NN = ((1,), (0,))
_NT = ((1,), (1,))
_TN = ((0,), (0,))


def _cparams(sem):
    return pltpu.CompilerParams(dimension_semantics=sem, vmem_limit_bytes=VMEM_LIMIT)


def _rms(x, g, eps=RMS_EPS):
    return x * lax.rsqrt(jnp.mean(x * x, axis=-1, keepdims=True) + eps) * g


def _sigmoid(x):
    return 1.0 / (1.0 + jnp.exp(-x))


def _silu(x):
    return x * _sigmoid(x)


def _gelu_tanh(x):
    return 0.5 * x * (1.0 + jnp.tanh(0.7978845608028654 * (x + 0.044715 * (x * x * x))))


def _dot(a, b, dims=_NN):
    return lax.dot_general(a, b, (dims, ((), ())), preferred_element_type=F32)


def _split2(x):
    hi = x.astype(BF16)
    lo = (x - hi.astype(F32)).astype(BF16)
    return hi, lo


def _dot3s(a, b, dims=_NN):
    ah, al = a
    bh, bl = b
    return _dot(ah, bh, dims) + (_dot(ah, bl, dims) + _dot(al, bh, dims))


def _dot3(a, b, dims=_NN):
    return _dot3s(_split2(a), _split2(b), dims)


def _cumsum_rows(tri_bf16, x):
    hi = x.astype(BF16)
    r1 = x - hi.astype(F32)
    mid = r1.astype(BF16)
    lo = (r1 - mid.astype(F32)).astype(BF16)
    return _dot(tri_bf16, hi) + (_dot(tri_bf16, mid) + _dot(tri_bf16, lo))


def _tri_masks(n):
    row = lax.broadcasted_iota(jnp.int32, (n, n), 0)
    col = lax.broadcasted_iota(jnp.int32, (n, n), 1)
    return col <= row, col < row


def _ffn_body(x_ref, g_ref, wg_ref, wu_ref, wd_ref, o_ref, xn_ref, acc_ref):
    j = pl.program_id(1)

    @pl.when(j == 0)
    def _():
        xn_ref[...] = _rms(x_ref[...], g_ref[...]).astype(BF16)
        acc_ref[...] = jnp.zeros_like(acc_ref)

    xn = xn_ref[...]
    gate = _dot(xn, wg_ref[...])
    up = _dot(xn, wu_ref[...])
    hid = (_silu(gate) * up).astype(BF16)
    acc_ref[...] += _dot(hid, wd_ref[...])

    @pl.when(j == pl.num_programs(1) - 1)
    def _():
        o_ref[...] = x_ref[...] + 0.5 * acc_ref[...]


def _ffn(h, g, wg, wu, wd, layer, half, tm=512, tf=512):
    t, d = h.shape
    f = wg.shape[-1]
    return pl.pallas_call(
        _ffn_body,
        grid=(t // tm, f // tf),
        in_specs=[
            pl.BlockSpec((tm, d), lambda i, j: (i, 0)),
            pl.BlockSpec((1, d), lambda i, j: (0, 0)),
            pl.BlockSpec((None, None, d, tf), lambda i, j: (layer, half, 0, j)),
            pl.BlockSpec((None, None, d, tf), lambda i, j: (layer, half, 0, j)),
            pl.BlockSpec((None, None, tf, d), lambda i, j: (layer, half, j, 0)),
        ],
        out_specs=pl.BlockSpec((tm, d), lambda i, j: (i, 0)),
        out_shape=jax.ShapeDtypeStruct((t, d), F32),
        scratch_shapes=[pltpu.VMEM((tm, d), BF16), pltpu.VMEM((tm, d), F32)],
        compiler_params=_cparams(("parallel", "arbitrary")),
        name="ffn",
    )(h, g, wg, wu, wd)


def _mm_body(x_ref, w_ref, o_ref, *, act):
    y = _dot(x_ref[...], w_ref[...])
    if act == "tanh":
        y = jnp.tanh(y)
    elif act == "sigmoid":
        y = _sigmoid(y)
    o_ref[...] = y.astype(o_ref.dtype)


def _mm(x, w, layer, act=None, tm=512, tn=512):
    t, k = x.shape
    n = w.shape[-1]
    tn = min(tn, n)
    return pl.pallas_call(
        functools.partial(_mm_body, act=act),
        grid=(t // tm, n // tn),
        in_specs=[
            pl.BlockSpec((tm, k), lambda i, j: (i, 0)),
            pl.BlockSpec((None, k, tn), lambda i, j: (layer, 0, j)),
        ],
        out_specs=pl.BlockSpec((tm, tn), lambda i, j: (i, j)),
        out_shape=jax.ShapeDtypeStruct((t, n), F32),
        compiler_params=_cparams(("parallel", "parallel")),
        name="mm",
    )(x, w)


def _mm_res_body(h_ref, x_ref, w_ref, o_ref):
    o_ref[...] = h_ref[...] + _dot(x_ref[...], w_ref[...])


def _mm_res(h, x, w, layer, tm=512, tn=512):
    t, k = x.shape
    n = w.shape[-1]
    return pl.pallas_call(
        _mm_res_body,
        grid=(t // tm, n // tn),
        in_specs=[
            pl.BlockSpec((tm, tn), lambda i, j: (i, j)),
            pl.BlockSpec((tm, k), lambda i, j: (i, 0)),
            pl.BlockSpec((None, k, tn), lambda i, j: (layer, 0, j)),
        ],
        out_specs=pl.BlockSpec((tm, tn), lambda i, j: (i, j)),
        out_shape=jax.ShapeDtypeStruct((t, n), F32),
        compiler_params=_cparams(("parallel", "parallel")),
        name="mm_res",
    )(h, x, w)


def _mm2_res_body(h_ref, xa_ref, xb_ref, wa_ref, wb_ref, o_ref):
    o_ref[...] = h_ref[...] + (_dot(xa_ref[...], wa_ref[...]) + _dot(xb_ref[...], wb_ref[...]))


def _mm2_res(h, xa, xb, w, layer, tm=512, tn=512):
    t, ka = xa.shape
    kb = xb.shape[1]
    n = w.shape[-1]
    assert ka == kb
    return pl.pallas_call(
        _mm2_res_body,
        grid=(t // tm, n // tn),
        in_specs=[
            pl.BlockSpec((tm, tn), lambda i, j: (i, j)),
            pl.BlockSpec((tm, ka), lambda i, j: (i, 0)),
            pl.BlockSpec((tm, kb), lambda i, j: (i, 0)),
            pl.BlockSpec((None, ka, tn), lambda i, j: (layer, 0, j)),
            pl.BlockSpec((None, kb, tn), lambda i, j: (layer, 1, j)),
        ],
        out_specs=pl.BlockSpec((tm, tn), lambda i, j: (i, j)),
        out_shape=jax.ShapeDtypeStruct((t, n), F32),
        compiler_params=_cparams(("parallel", "parallel")),
        name="mm2_res",
    )(h, xa, xb, w, w)


def _nmm_body(x_ref, g_ref, w_ref, o_ref, xn_ref):
    @pl.when(pl.program_id(1) == 0)
    def _():
        xn_ref[...] = _rms(x_ref[...], g_ref[...]).astype(BF16)

    o_ref[...] = _dot(xn_ref[...], w_ref[...])


def _nmm(h, g, w, layer, tm=512, tn=512):
    t, d = h.shape
    n = w.shape[-1]
    return pl.pallas_call(
        _nmm_body,
        grid=(t // tm, n // tn),
        in_specs=[
            pl.BlockSpec((tm, d), lambda i, j: (i, 0)),
            pl.BlockSpec((1, d), lambda i, j: (0, 0)),
            pl.BlockSpec((None, d, tn), lambda i, j: (layer, 0, j)),
        ],
        out_specs=pl.BlockSpec((tm, tn), lambda i, j: (i, j)),
        out_shape=jax.ShapeDtypeStruct((t, n), F32),
        scratch_shapes=[pltpu.VMEM((tm, d), BF16)],
        compiler_params=_cparams(("parallel", "arbitrary")),
        name="norm_mm",
    )(h, g, w)


def _ple_body(h_ref, g_ref, p_ref, wg_ref, wp_ref, fg_ref, o_ref, *, final):
    h = h_ref[...]
    xn = _rms(h, g_ref[...]).astype(BF16)
    gate = _sigmoid(_dot(xn, wg_ref[...]))
    pe = _dot(p_ref[...].astype(BF16), wp_ref[...])
    out = h + gate * pe
    if final:
        out = _rms(out, fg_ref[...])
    o_ref[...] = out


def _ple(h, g, p, wg, wp, fg, layer, final, tm=512):
    t, d = h.shape
    pd = p.shape[-1]
    return pl.pallas_call(
        functools.partial(_ple_body, final=final),
        grid=(t // tm,),
        in_specs=[
            pl.BlockSpec((tm, d), lambda i: (i, 0)),
            pl.BlockSpec((1, d), lambda i: (0, 0)),
            pl.BlockSpec((None, tm, pd), lambda i: (layer, i, 0)),
            pl.BlockSpec((None, d, d), lambda i: (layer, 0, 0)),
            pl.BlockSpec((None, pd, d), lambda i: (layer, 0, 0)),
            pl.BlockSpec((1, d), lambda i: (0, 0)),
        ],
        out_specs=pl.BlockSpec((tm, d), lambda i: (i, 0)),
        out_shape=jax.ShapeDtypeStruct((t, d), F32),
        compiler_params=_cparams(("parallel",)),
        name="ple",
    )(h, g, p, wg, wp, fg)


def _gmlp_body(u_ref, v_ref, gain_ref, ws_ref, bs_ref, o_ref):
    u = _gelu_tanh(u_ref[...])
    v = _gelu_tanh(v_ref[...])
    vg = _rms(v, gain_ref[...])
    tril, _ = _tri_masks(A_CHUNK)
    w = jnp.where(tril, ws_ref[...], 0.0).astype(BF16)
    s = _dot(w, vg.astype(BF16)) + bs_ref[...]
    o_ref[...] = (u * s).astype(o_ref.dtype)


def _gmlp(proj, gain, ws, bs, layer, a_width):
    t = proj.shape[0]
    groups = a_width // A_CHUNK
    return pl.pallas_call(
        _gmlp_body,
        grid=(t // A_CHUNK, groups),
        in_specs=[
            pl.BlockSpec((A_CHUNK, A_CHUNK), lambda c, g: (c, g)),
            pl.BlockSpec((A_CHUNK, A_CHUNK), lambda c, g: (c, groups + g)),
            pl.BlockSpec((None, 1, A_CHUNK), lambda c, g: (layer, 0, g)),
            pl.BlockSpec((None, None, A_CHUNK, A_CHUNK), lambda c, g: (layer, g, 0, 0)),
            pl.BlockSpec((None, None, A_CHUNK, 1), lambda c, g: (layer, g, 0, 0)),
        ],
        out_specs=pl.BlockSpec((A_CHUNK, A_CHUNK), lambda c, g: (c, g)),
        out_shape=jax.ShapeDtypeStruct((t, a_width), BF16),
        compiler_params=_cparams(("parallel", "parallel")),
        name="gmlp",
    )(proj, proj, gain, ws, bs)


HG_CHUNK = 64
HG_SUB = 16


def _hgrn2_body(q_ref, f_ref, i_ref, g_ref, lbl_ref, on_ref, o_ref, st_ref, *, layer):
    c = pl.program_id(2)

    @pl.when(c == 0)
    def _():
        st_ref[...] = jnp.zeros_like(st_ref)

    n = HG_CHUNK
    logits = lbl_ref[...]
    e = jnp.exp(logits - jnp.max(logits, axis=0, keepdims=True))
    probs = e / jnp.sum(e, axis=0, keepdims=True)
    lb = jnp.zeros((1, B_HEAD), F32)
    for r in range(1, layer + 1):
        lb = lb + probs[r:r + 1, :]

    sig = _sigmoid(f_ref[...])
    f = lb + (1.0 - lb) * sig
    lf = jnp.log(jnp.maximum(f, B_MIN_F))
    kf = 1.0 - f
    qf = _silu(q_ref[...])
    v = i_ref[...]

    tril, _ = _tri_masks(n)
    tri = jnp.where(tril, 1.0, 0.0).astype(BF16)
    cum = _cumsum_rows(tri, lf)
    last = cum[n - 1:n, :]

    st = st_ref[...]
    st_s = _split2(st)
    v_s = _split2(v)
    o = _dot3s(_split2(qf * jnp.exp(cum)), st_s, _NT)

    nsub = n // HG_SUB
    rows = []
    tsub, _ = _tri_masks(HG_SUB)
    for bi in range(nsub):
        lo, hi = bi * HG_SUB, (bi + 1) * HG_SUB
        q_b, c_b, k_b, v_b = qf[lo:hi], cum[lo:hi], kf[lo:hi], v[lo:hi]
        acc = jnp.zeros((HG_SUB, B_HEAD), F32)
        if bi > 0:
            ref = cum[lo - 1:lo, :]
            qh = q_b * jnp.exp(c_b - ref)
            kh = kf[:lo] * jnp.exp(ref - cum[:lo])
            att = _dot3(qh, kh, _NT)
            acc = acc + _dot3s(_split2(att), (v_s[0][:lo], v_s[1][:lo]))
        trow = lax.broadcasted_iota(jnp.int32, (HG_SUB, 1), 0)
        for s in range(HG_SUB):
            dec = jnp.exp(jnp.minimum(c_b - c_b[s:s + 1, :], 0.0))
            col = jnp.sum(q_b * dec * k_b[s:s + 1, :], axis=-1, keepdims=True)
            col = jnp.where(trow >= s, col, 0.0)
            acc = acc + col * v_b[s:s + 1, :]
        rows.append(acc)
    o = o + jnp.concatenate(rows, axis=0)

    kend = kf * jnp.exp(last - cum)
    st_ref[...] = st * jnp.exp(last) + _dot3s(v_s, _split2(kend), _TN)

    o_ref[...] = (_rms(o, on_ref[...]) * _silu(g_ref[...])).astype(o_ref.dtype)


def _hgrn2(proj, lb_logits, onorm, layer, elayer, bsz, seq, a_width, b_width):
    t = proj.shape[0]
    heads = b_width // B_HEAD
    nchunk = seq // HG_CHUNK
    off = 2 * a_width // B_HEAD

    def col(which):
        return lambda b, h, c: (b * nchunk + c, off + which * heads + h)

    blk = (HG_CHUNK, B_HEAD)
    depth = lb_logits.shape[0]
    return pl.pallas_call(
        functools.partial(_hgrn2_body, layer=layer),
        grid=(bsz, heads, nchunk),
        in_specs=[
            pl.BlockSpec(blk, col(0)),
            pl.BlockSpec(blk, col(1)),
            pl.BlockSpec(blk, col(2)),
            pl.BlockSpec(blk, col(3)),
            pl.BlockSpec((depth, B_HEAD), lambda b, h, c: (0, h)),
            pl.BlockSpec((None, 1, B_HEAD), lambda b, h, c: (elayer, 0, h)),
        ],
        out_specs=pl.BlockSpec(blk, lambda b, h, c: (b * nchunk + c, h)),
        out_shape=jax.ShapeDtypeStruct((t, b_width), BF16),
        scratch_shapes=[pltpu.VMEM((B_HEAD, B_HEAD), F32)],
        compiler_params=_cparams(("parallel", "parallel", "arbitrary")),
        name="hgrn2",
    )(proj, proj, proj, proj, lb_logits, onorm)


def _rwkv_mix_body(x_ref, xp_ref, g_ref, mix_ref, *o_refs, tm, seq):
    i = pl.program_id(0)
    g = g_ref[...]
    hn = _rms(x_ref[...], g)
    hp = _rms(xp_ref[...], g)[7:8, :]
    hp = jnp.where((i * tm) % seq == 0, 0.0, hp)
    row = lax.broadcasted_iota(jnp.int32, hn.shape, 0)
    prev = jnp.where(row == 0, hp, pltpu.roll(hn, 1, 0))
    xx = prev - hn
    for j, o_ref in enumerate(o_refs):
        o_ref[...] = (hn + xx * mix_ref[j:j + 1, :]).astype(BF16)


def _rwkv_mix(h, g, mix, layer, seq, tm=256):
    t, d = h.shape
    nmix = mix.shape[1]
    sub = 8
    return pl.pallas_call(
        functools.partial(_rwkv_mix_body, tm=tm, seq=seq),
        grid=(t // tm,),
        in_specs=[
            pl.BlockSpec((tm, d), lambda i: (i, 0)),
            pl.BlockSpec((sub, d), lambda i: (jnp.maximum(i * (tm // sub) - 1, 0), 0)),
            pl.BlockSpec((1, d), lambda i: (0, 0)),
            pl.BlockSpec((None, nmix, d), lambda i: (layer, 0, 0)),
        ],
        out_specs=[pl.BlockSpec((tm, d), lambda i: (i, 0)) for _ in range(nmix)],
        out_shape=[jax.ShapeDtypeStruct((t, d), BF16) for _ in range(nmix)],
        compiler_params=_cparams(("parallel",)),
        name="rwkv_mix",
    )(h, h, g, mix)


def _rwkv_prep_body(*refs, vres):
    if vres:
        (hw_ref, ha_ref, hg_ref, hv_ref, v_ref, vf_ref, w2_ref, a2_ref, g2_ref, v2_ref,
         w0_ref, a0_ref, v0_ref, lw_ref, a_ref, g_ref, vo_ref) = refs
    else:
        (hw_ref, ha_ref, hg_ref, w2_ref, a2_ref, g2_ref, w0_ref, a0_ref,
         lw_ref, a_ref, g_ref) = refs
    z = -(w0_ref[...] + _dot(hw_ref[...].astype(BF16), w2_ref[...]))
    softplus = jnp.maximum(z, 0.0) + jnp.log(1.0 + jnp.exp(-jnp.abs(z)))
    w = -softplus - 0.5
    lw_ref[...] = -jnp.exp(w)
    a_ref[...] = _sigmoid(a0_ref[...] + _dot(ha_ref[...].astype(BF16), a2_ref[...]))
    g_ref[...] = _dot(hg_ref[...].astype(BF16), g2_ref[...])
    if vres:
        v = v_ref[...]
        mv = _sigmoid(v0_ref[...] + _dot(hv_ref[...].astype(BF16), v2_ref[...]))
        vo_ref[...] = v + (vf_ref[...] - v) * mv


def _rwkv_prep(hw, ha, hg, w2, a2, g2, w0, a0, layer, vres=None, tm=256):
    t = hw.shape[0]
    d = w2.shape[-1]

    def tok(n):
        return pl.BlockSpec((tm, n), lambda i: (i, 0))

    def wgt(k):
        return pl.BlockSpec((None, k, d), lambda i: (layer, 0, 0))

    def vec(idx):
        return pl.BlockSpec((None, 1, d), lambda i: (idx, 0, 0))

    if vres is None:
        args = (hw, ha, hg, w2, a2, g2, w0, a0)
        in_specs = [tok(hw.shape[1]), tok(ha.shape[1]), tok(hg.shape[1]),
                    wgt(w2.shape[1]), wgt(a2.shape[1]), wgt(g2.shape[1]), vec(layer), vec(layer)]
        nout = 3
    else:
        hv, v, vf, v2, v0, vl = vres
        args = (hw, ha, hg, hv, v, vf, w2, a2, g2, v2, w0, a0, v0)
        in_specs = [tok(hw.shape[1]), tok(ha.shape[1]), tok(hg.shape[1]), tok(hv.shape[1]), tok(d), tok(d),
                    wgt(w2.shape[1]), wgt(a2.shape[1]), wgt(g2.shape[1]),
                    pl.BlockSpec((None, v2.shape[1], d), lambda i: (vl, 0, 0)),
                    vec(layer), vec(layer), vec(vl)]
        nout = 4
    return pl.pallas_call(
        functools.partial(_rwkv_prep_body, vres=vres is not None),
        grid=(t // tm,),
        in_specs=in_specs,
        out_specs=[tok(d) for _ in range(nout)],
        out_shape=[jax.ShapeDtypeStruct((t, d), F32) for _ in range(nout)],
        compiler_params=_cparams(("parallel",)),
        name="rwkv_prep",
    )(*args)


RW_CHUNK = 64


def _inv_unit_lower(a, n):
    row = lax.broadcasted_iota(jnp.int32, (n, n), 0)
    col = lax.broadcasted_iota(jnp.int32, (n, n), 1)
    t = jnp.where(row == col, 1.0, 0.0) + a
    p = a
    k = 2
    while k < n:
        ps = _split2(p)
        p = _dot3s(ps, ps)
        t = t + _dot3s(_split2(t), _split2(p))
        k *= 2
    return t


def _head_sum(x, m0):
    s0 = jnp.sum(jnp.where(m0, x, 0.0), axis=-1, keepdims=True)
    s1 = jnp.sum(jnp.where(m0, 0.0, x), axis=-1, keepdims=True)
    return jnp.where(m0, s0, s1)


def _rwkv_scan_body(r_ref, k_ref, v_ref, lw_ref, a_ref, g_ref, kk_ref, ka_ref, rk_ref, gg_ref, gb_ref,
                    o_ref, st_ref):
    c = pl.program_id(2)

    @pl.when(c == 0)
    def _():
        st_ref[...] = jnp.zeros_like(st_ref)

    n = RW_CHUNK
    lane = lax.broadcasted_iota(jnp.int32, (1, LANES), 1)
    m0 = lane < C_HEAD
    r = r_ref[...]
    k = k_ref[...]
    v = v_ref[...]
    lw = lw_ref[...]
    asig = a_ref[...]

    kkr = k * kk_ref[...]
    kk = kkr / jnp.maximum(jnp.sqrt(_head_sum(kkr * kkr, m0)), 1e-12)
    kmod = k * (1.0 + (asig - 1.0) * ka_ref[...])
    a = -kk
    b = kk * asig

    tril, stril = _tri_masks(n)
    tri = jnp.where(tril, 1.0, 0.0).astype(BF16)
    cum = _cumsum_rows(tri, lw)
    cume = cum - lw
    last = cum[n - 1:n, :]
    rt = r * jnp.exp(cum)
    at = a * jnp.exp(cume)
    einv = jnp.exp(-cum)
    bt_s = _split2(b * einv)
    kt_s = _split2(kmod * einv)
    eend = jnp.exp(last - cum)
    bend = b * eend
    kend = kmod * eend
    v_s = _split2(v)

    st = st_ref[...]
    st_s = _split2(st)

    t_inv, a_rb, a_rk, av = [], [], [], []
    for hd in range(2):
        mh = m0 if hd == 0 else jnp.logical_not(m0)
        at_h = _split2(jnp.where(mh, at, 0.0))
        rt_h = _split2(jnp.where(mh, rt, 0.0))
        a_ab = jnp.where(stril, _dot3s(at_h, bt_s, _NT), 0.0)
        a_ak = jnp.where(stril, _dot3s(at_h, kt_s, _NT), 0.0)
        a_rb.append(_split2(jnp.where(tril, _dot3s(rt_h, bt_s, _NT), 0.0)))
        a_rk.append(_split2(jnp.where(tril, _dot3s(rt_h, kt_s, _NT), 0.0)))
        t_inv.append(_split2(_inv_unit_lower(a_ab, n)))
        av.append(_dot3s(_split2(a_ak), v_s))

    x = _dot3s(_split2(at), st_s, _NT) + jnp.where(m0, av[0], av[1])
    x_s = _split2(x)
    u = jnp.where(m0, _dot3s(t_inv[0], x_s), _dot3s(t_inv[1], x_s))
    u_s = _split2(u)
    y = _dot3s(_split2(rt), st_s, _NT) + jnp.where(
        m0,
        _dot3s(a_rb[0], u_s) + _dot3s(a_rk[0], v_s),
        _dot3s(a_rb[1], u_s) + _dot3s(a_rk[1], v_s))

    row = lax.broadcasted_iota(jnp.int32, (LANES, LANES), 0)
    colm = lax.broadcasted_iota(jnp.int32, (LANES, LANES), 1)
    bdiag = (row < C_HEAD) == (colm < C_HEAD)
    upd = _dot3s(u_s, _split2(bend), _TN) + _dot3s(v_s, _split2(kend), _TN)
    st_ref[...] = st * jnp.exp(last) + jnp.where(bdiag, upd, 0.0)

    inv_n = 1.0 / C_HEAD
    mu = _head_sum(y, m0) * inv_n
    dy = y - mu
    var = _head_sum(dy * dy, m0) * inv_n
    yn = dy * lax.rsqrt(var + C_GN_EPS) * gg_ref[...] + gb_ref[...]
    bonus = _head_sum(r * kmod * rk_ref[...], m0) * v
    o_ref[...] = ((yn + bonus) * g_ref[...]).astype(o_ref.dtype)


def _rwkv_scan(r, k, v, lw, a, g, kk, ka, rk, gg, gb, layer, bsz, seq):
    t, d = r.shape
    nchunk = seq // RW_CHUNK
    blk = pl.BlockSpec((RW_CHUNK, LANES), lambda b, p, c: (b * nchunk + c, p))
    vec = pl.BlockSpec((None, 1, LANES), lambda b, p, c: (layer, 0, p))
    return pl.pallas_call(
        _rwkv_scan_body,
        grid=(bsz, d // LANES, nchunk),
        in_specs=[blk] * 6 + [vec] * 5,
        out_specs=blk,
        out_shape=jax.ShapeDtypeStruct((t, d), BF16),
        scratch_shapes=[pltpu.VMEM((LANES, LANES), F32)],
        compiler_params=_cparams(("parallel", "parallel", "arbitrary")),
        name="rwkv_scan",
    )(r, k, v, lw, a, g, kk, ka, rk, gg, gb)


def _pad_cols(w, n):
    return jnp.pad(w, ((0, 0), (0, 0), (0, n - w.shape[-1])))


def _pad_rows(w, n):
    return jnp.pad(w, ((0, 0), (0, n - w.shape[1]), (0, 0)))


def kernel(x, p, norms, final_norm, ffn_wg, ffn_wu, ffn_wd, ple_wp, ple_wg, e_w_in, e_w_out, a_vnorm, a_ws, a_bs, b_onorm, b_lb_logits, c_mix, c_wr, c_wk, c_wv, c_wo, c_w0, c_w1, c_w2, c_a0, c_a1, c_a2, c_g1, c_g2, c_kk, c_ka, c_rk, c_gn_g, c_gn_b, c_v0, c_v1, c_v2):
    bsz, seq, d = x.shape
    depth = p.shape[0]
    t = bsz * seq
    a_width = a_vnorm.shape[-1]
    b_width = b_onorm.shape[-1]

    bf = lambda w: w.astype(BF16)
    wg_b, wu_b, wd_b = bf(ffn_wg), bf(ffn_wu), bf(ffn_wd)
    ple_wp_b, ple_wg_b = bf(ple_wp), bf(ple_wg)
    e_in_b = bf(e_w_in)
    e_out_b = bf(e_w_out)
    wr_b, wk_b, wv_b, wo_b = bf(c_wr), bf(c_wk), bf(c_wv), bf(c_wo)
    w1_b = bf(_pad_cols(c_w1, LORA_PAD))
    w2_b = bf(_pad_rows(c_w2, LORA_PAD))
    a1_b = bf(_pad_cols(c_a1, LORA_PAD))
    a2_b = bf(_pad_rows(c_a2, LORA_PAD))
    g1_b, g2_b = bf(c_g1), bf(c_g2)
    v1_b = bf(_pad_cols(c_v1, LORA_PAD))
    v2_b = bf(_pad_rows(c_v2, LORA_PAD))

    vec3 = lambda w: w.reshape(w.shape[0], 1, -1)
    a_vnorm3, b_onorm3 = vec3(a_vnorm), vec3(b_onorm)
    a_bs4 = a_bs.reshape(a_bs.shape + (1,))
    w0_3, a0_3, v0_3 = vec3(c_w0), vec3(c_a0), vec3(c_v0)
    kk3, ka3, rk3, gg3, gb3 = vec3(c_kk), vec3(c_ka), vec3(c_rk), vec3(c_gn_g), vec3(c_gn_b)
    fg = final_norm.reshape(1, d)

    h = x.reshape(t, d)
    p2 = p.reshape(depth, t, p.shape[-1])
    v_first = None
    for i in range(depth):
        j = i // 2
        h = _ffn(h, norms[i, 0].reshape(1, d), wg_b, wu_b, wd_b, i, 0)
        g1n = norms[i, 1].reshape(1, d)
        if i % 2 == 0:
            proj = _nmm(h, g1n, e_in_b, j)
            a_out = _gmlp(proj, a_vnorm3, a_ws, a_bs4, j, a_width)
            b_out = _hgrn2(proj, b_lb_logits, b_onorm3, i, j, bsz, seq, a_width, b_width)
            h = _mm2_res(h, a_out, b_out, e_out_b, j)
        else:
            xr, xw, xk, xv, xa, xg = _rwkv_mix(h, g1n, c_mix, j, seq)
            r = _mm(xr, wr_b, j)
            k = _mm(xk, wk_b, j)
            v = _mm(xv, wv_b, j)
            hw = _mm(xw, w1_b, j, act="tanh")
            ha = _mm(xa, a1_b, j)
            hg = _mm(xg, g1_b, j, act="sigmoid")
            if j == 0:
                lw, a, g = _rwkv_prep(hw, ha, hg, w2_b, a2_b, g2_b, w0_3, a0_3, j)
                v_first = v
            else:
                hv = _mm(xv, v1_b, j - 1)
                lw, a, g, v = _rwkv_prep(hw, ha, hg, w2_b, a2_b, g2_b, w0_3, a0_3, j,
                                         vres=(hv, v, v_first, v2_b, v0_3, j - 1))
            y = _rwkv_scan(r, k, v, lw, a, g, kk3, ka3, rk3, gg3, gb3, j, bsz, seq)
            h = _mm_res(h, y, wo_b, j)
        h = _ffn(h, norms[i, 2].reshape(1, d), wg_b, wu_b, wd_b, i, 1)
        h = _ple(h, norms[i, 3].reshape(1, d), p2, ple_wg_b, ple_wp_b, fg, i, final=(i == depth - 1))
    return h.reshape(bsz, seq, d)
```

```python
import functools

import jax
import jax.numpy as jnp
from jax import lax
from jax.experimental import pallas as pl
from jax.experimental.pallas import tpu as pltpu

F32 = jnp.float32
BF16 = jnp.bfloat16

LANES = 128
RMS_EPS = 1e-6
A_CHUNK = 128
B_HEAD = 128
B_MIN_F = 1e-30
C_HEAD = 64
C_GN_EPS = 64e-5
LORA_PAD = 128

VMEM_LIMIT = 56 * 1024 * 1024

_NN = ((1,), (0,))
_NT = ((1,), (1,))
_TN = ((0,), (0,))


def _cparams(sem):
    return pltpu.CompilerParams(dimension_semantics=sem, vmem_limit_bytes=VMEM_LIMIT)


def _rms(x, g, eps=RMS_EPS):
    return x * lax.rsqrt(jnp.mean(x * x, axis=-1, keepdims=True) + eps) * g


def _sigmoid(x):
    return 1.0 / (1.0 + jnp.exp(-x))


def _silu(x):
    return x * _sigmoid(x)


def _gelu_tanh(x):
    return 0.5 * x * (1.0 + jnp.tanh(0.7978845608028654 * (x + 0.044715 * (x * x * x))))


def _dot(a, b, dims=_NN):
    return lax.dot_general(a, b, (dims, ((), ())), preferred_element_type=F32)


def _sp(x, lo):
    hi = x.astype(BF16)
    if not lo:
        return (hi,)
    return (hi, (x - hi.astype(F32)).astype(BF16))


def _pd(a, b, dims=_NN):
    out = _dot(a[0], b[0], dims)
    extra = None
    if len(b) > 1:
        extra = _dot(a[0], b[1], dims)
    if len(a) > 1:
        t = _dot(a[1], b[0], dims)
        extra = t if extra is None else extra + t
    return out if extra is None else out + extra


def _cumsum_rows(tri_bf16, x):
    hi = x.astype(BF16)
    r1 = x - hi.astype(F32)
    mid = r1.astype(BF16)
    lo = (r1 - mid.astype(F32)).astype(BF16)
    return _dot(tri_bf16, hi) + (_dot(tri_bf16, mid) + _dot(tri_bf16, lo))


def _tri_masks(n):
    row = lax.broadcasted_iota(jnp.int32, (n, n), 0)
    col = lax.broadcasted_iota(jnp.int32, (n, n), 1)
    return col <= row, col < row


def _ffn_body(x_ref, g_ref, wg_ref, wu_ref, wd_ref, o_ref, xn_ref, acc_ref):
    j = pl.program_id(1)

    @pl.when(j == 0)
    def _():
        xn_ref[...] = _rms(x_ref[...], g_ref[...]).astype(BF16)
        acc_ref[...] = jnp.zeros_like(acc_ref)

    xn = xn_ref[...]
    gate = _dot(xn, wg_ref[...])
    up = _dot(xn, wu_ref[...])
    hid = (_silu(gate) * up).astype(BF16)
    acc_ref[...] += _dot(hid, wd_ref[...])

    @pl.when(j == pl.num_programs(1) - 1)
    def _():
        o_ref[...] = x_ref[...] + 0.5 * acc_ref[...]


def _ffn(h, g, wg, wu, wd, layer, half, tm=512, tf=512):
    t, d = h.shape
    f = wg.shape[-1]
    return pl.pallas_call(
        _ffn_body,
        grid=(t // tm, f // tf),
        in_specs=[
            pl.BlockSpec((tm, d), lambda i, j: (i, 0)),
            pl.BlockSpec((1, d), lambda i, j: (0, 0)),
            pl.BlockSpec((None, None, d, tf), lambda i, j: (layer, half, 0, j)),
            pl.BlockSpec((None, None, d, tf), lambda i, j: (layer, half, 0, j)),
            pl.BlockSpec((None, None, tf, d), lambda i, j: (layer, half, j, 0)),
        ],
        out_specs=pl.BlockSpec((tm, d), lambda i, j: (i, 0)),
        out_shape=jax.ShapeDtypeStruct((t, d), F32),
        scratch_shapes=[pltpu.VMEM((tm, d), BF16), pltpu.VMEM((tm, d), F32)],
        compiler_params=_cparams(("parallel", "arbitrary")),
        name="ffn",
    )(h, g, wg, wu, wd)


def _mm_body(x_ref, w_ref, o_ref, *, act):
    y = _dot(x_ref[...], w_ref[...])
    if act == "tanh":
        y = jnp.tanh(y)
    elif act == "sigmoid":
        y = _sigmoid(y)
    o_ref[...] = y.astype(o_ref.dtype)


def _mm(x, w, layer, act=None, tm=512, tn=512):
    t, k = x.shape
    n = w.shape[-1]
    tn = min(tn, n)
    return pl.pallas_call(
        functools.partial(_mm_body, act=act),
        grid=(t // tm, n // tn),
        in_specs=[
            pl.BlockSpec((tm, k), lambda i, j: (i, 0)),
            pl.BlockSpec((None, k, tn), lambda i, j: (layer, 0, j)),
        ],
        out_specs=pl.BlockSpec((tm, tn), lambda i, j: (i, j)),
        out_shape=jax.ShapeDtypeStruct((t, n), F32),
        compiler_params=_cparams(("parallel", "parallel")),
        name="mm",
    )(x, w)


def _mm_res_body(h_ref, x_ref, w_ref, o_ref):
    o_ref[...] = h_ref[...] + _dot(x_ref[...], w_ref[...])


def _mm_res(h, x, w, layer, tm=512, tn=512):
    t, k = x.shape
    n = w.shape[-1]
    return pl.pallas_call(
        _mm_res_body,
        grid=(t // tm, n // tn),
        in_specs=[
            pl.BlockSpec((tm, tn), lambda i, j: (i, j)),
            pl.BlockSpec((tm, k), lambda i, j: (i, 0)),
            pl.BlockSpec((None, k, tn), lambda i, j: (layer, 0, j)),
        ],
        out_specs=pl.BlockSpec((tm, tn), lambda i, j: (i, j)),
        out_shape=jax.ShapeDtypeStruct((t, n), F32),
        compiler_params=_cparams(("parallel", "parallel")),
        name="mm_res",
    )(h, x, w)


def _mm2_res_body(h_ref, xa_ref, xb_ref, wa_ref, wb_ref, o_ref):
    o_ref[...] = h_ref[...] + (_dot(xa_ref[...], wa_ref[...]) + _dot(xb_ref[...], wb_ref[...]))


def _mm2_res(h, xa, xb, w, layer, tm=512, tn=512):
    t, ka = xa.shape
    kb = xb.shape[1]
    n = w.shape[-1]
    assert ka == kb
    return pl.pallas_call(
        _mm2_res_body,
        grid=(t // tm, n // tn),
        in_specs=[
            pl.BlockSpec((tm, tn), lambda i, j: (i, j)),
            pl.BlockSpec((tm, ka), lambda i, j: (i, 0)),
            pl.BlockSpec((tm, kb), lambda i, j: (i, 0)),
            pl.BlockSpec((None, ka, tn), lambda i, j: (layer, 0, j)),
            pl.BlockSpec((None, kb, tn), lambda i, j: (layer, 1, j)),
        ],
        out_specs=pl.BlockSpec((tm, tn), lambda i, j: (i, j)),
        out_shape=jax.ShapeDtypeStruct((t, n), F32),
        compiler_params=_cparams(("parallel", "parallel")),
        name="mm2_res",
    )(h, xa, xb, w, w)


def _nmm_body(x_ref, g_ref, w_ref, o_ref, xn_ref):
    @pl.when(pl.program_id(1) == 0)
    def _():
        xn_ref[...] = _rms(x_ref[...], g_ref[...]).astype(BF16)

    o_ref[...] = _dot(xn_ref[...], w_ref[...])


def _nmm(h, g, w, layer, tm=512, tn=512):
    t, d = h.shape
    n = w.shape[-1]
    return pl.pallas_call(
        _nmm_body,
        grid=(t // tm, n // tn),
        in_specs=[
            pl.BlockSpec((tm, d), lambda i, j: (i, 0)),
            pl.BlockSpec((1, d), lambda i, j: (0, 0)),
            pl.BlockSpec((None, d, tn), lambda i, j: (layer, 0, j)),
        ],
        out_specs=pl.BlockSpec((tm, tn), lambda i, j: (i, j)),
        out_shape=jax.ShapeDtypeStruct((t, n), F32),
        scratch_shapes=[pltpu.VMEM((tm, d), BF16)],
        compiler_params=_cparams(("parallel", "arbitrary")),
        name="norm_mm",
    )(h, g, w)


def _ple_body(h_ref, g_ref, p_ref, wg_ref, wp_ref, fg_ref, o_ref, *, final):
    h = h_ref[...]
    xn = _rms(h, g_ref[...]).astype(BF16)
    gate = _sigmoid(_dot(xn, wg_ref[...]))
    pe = _dot(p_ref[...].astype(BF16), wp_ref[...])
    out = h + gate * pe
    if final:
        out = _rms(out, fg_ref[...])
    o_ref[...] = out


def _ple(h, g, p, wg, wp, fg, layer, final, tm=512):
    t, d = h.shape
    pd = p.shape[-1]
    return pl.pallas_call(
        functools.partial(_ple_body, final=final),
        grid=(t // tm,),
        in_specs=[
            pl.BlockSpec((tm, d), lambda i: (i, 0)),
            pl.BlockSpec((1, d), lambda i: (0, 0)),
            pl.BlockSpec((None, tm, pd), lambda i: (layer, i, 0)),
            pl.BlockSpec((None, d, d), lambda i: (layer, 0, 0)),
            pl.BlockSpec((None, pd, d), lambda i: (layer, 0, 0)),
            pl.BlockSpec((1, d), lambda i: (0, 0)),
        ],
        out_specs=pl.BlockSpec((tm, d), lambda i: (i, 0)),
        out_shape=jax.ShapeDtypeStruct((t, d), F32),
        compiler_params=_cparams(("parallel",)),
        name="ple",
    )(h, g, p, wg, wp, fg)


GMLP_CHUNKS = 2


def _gmlp_body(u_ref, v_ref, gain_ref, ws_ref, bs_ref, o_ref):
    groups = ws_ref.shape[0]
    tril, _ = _tri_masks(A_CHUNK)
    for g in range(groups):
        w = jnp.where(tril, ws_ref[g], 0.0).astype(BF16)
        bias = bs_ref[g]
        cols = slice(g * A_CHUNK, (g + 1) * A_CHUNK)
        for c in range(GMLP_CHUNKS):
            rows = slice(c * A_CHUNK, (c + 1) * A_CHUNK)
            u = _gelu_tanh(u_ref[rows, cols])
            v = _gelu_tanh(v_ref[rows, cols])
            vg = _rms(v, gain_ref[:, cols])
            s = _dot(w, vg.astype(BF16)) + bias
            o_ref[rows, cols] = (u * s).astype(o_ref.dtype)


def _gmlp(proj, gain, ws, bs, layer, a_width):
    t = proj.shape[0]
    groups = a_width // A_CHUNK
    tm = GMLP_CHUNKS * A_CHUNK
    return pl.pallas_call(
        _gmlp_body,
        grid=(t // tm,),
        in_specs=[
            pl.BlockSpec((tm, a_width), lambda c: (c, 0)),
            pl.BlockSpec((tm, a_width), lambda c: (c, 1)),
            pl.BlockSpec((None, 1, a_width), lambda c: (layer, 0, 0)),
            pl.BlockSpec((None, groups, A_CHUNK, A_CHUNK), lambda c: (layer, 0, 0, 0)),
            pl.BlockSpec((None, groups, A_CHUNK, 1), lambda c: (layer, 0, 0, 0)),
        ],
        out_specs=pl.BlockSpec((tm, a_width), lambda c: (c, 0)),
        out_shape=jax.ShapeDtypeStruct((t, a_width), BF16),
        compiler_params=_cparams(("parallel",)),
        name="gmlp",
    )(proj, proj, gain, ws, bs)


HG_CHUNK = 64
HG_SUB = 16


HG_HEADS = 4
P_HG = False


def _hgrn2_body(q_ref, f_ref, i_ref, g_ref, lbl_ref, on_ref, o_ref, st_ref, *, layer):
    c = pl.program_id(2)

    @pl.when(c == 0)
    def _():
        st_ref[...] = jnp.zeros_like(st_ref)

    logits = lbl_ref[...]
    e = jnp.exp(logits - jnp.max(logits, axis=0, keepdims=True))
    probs = e / jnp.sum(e, axis=0, keepdims=True)
    lb = jnp.zeros((1, logits.shape[1]), F32)
    for r in range(1, layer + 1):
        lb = lb + probs[r:r + 1, :]

    n = HG_CHUNK
    hs = range(HG_HEADS)
    cols = [slice(hd * B_HEAD, (hd + 1) * B_HEAD) for hd in hs]
    tril, _ = _tri_masks(n)
    tri = jnp.where(tril, 1.0, 0.0).astype(BF16)

    st = [st_ref[hd] for hd in hs]
    v = [i_ref[:, c] for c in cols]
    f = [lb[:, c] + (1.0 - lb[:, c]) * _sigmoid(f_ref[:, c]) for c in cols]
    kf = [1.0 - x for x in f]
    qf = [_silu(q_ref[:, c]) for c in cols]
    cum = [_cumsum_rows(tri, jnp.log(jnp.maximum(x, B_MIN_F))) for x in f]
    last = [x[n - 1:n, :] for x in cum]

    st_s = [_sp(x, P_HG) for x in st]
    v_s = [_sp(x, P_HG) for x in v]
    o = [_pd(_sp(qf[h] * jnp.exp(cum[h]), P_HG), st_s[h], _NT) for h in hs]
    for h in hs:
        kend = kf[h] * jnp.exp(last[h] - cum[h])
        st_ref[h] = st[h] * jnp.exp(last[h]) + _pd(v_s[h], _sp(kend, P_HG), _TN)

    nsub = n // HG_SUB
    trow = lax.broadcasted_iota(jnp.int32, (HG_SUB, 1), 0)
    rows = [[] for _ in hs]
    for bi in range(nsub):
        lo, hi = bi * HG_SUB, (bi + 1) * HG_SUB
        acc = [jnp.zeros((HG_SUB, B_HEAD), F32) for _ in hs]
        if bi > 0:
            ref = [cum[h][lo - 1:lo, :] for h in hs]
            qh = [_sp(qf[h][lo:hi] * jnp.exp(cum[h][lo:hi] - ref[h]), P_HG) for h in hs]
            kh = [_sp(kf[h][:lo] * jnp.exp(ref[h] - cum[h][:lo]), P_HG) for h in hs]
            att = [_sp(_pd(qh[h], kh[h], _NT), P_HG) for h in hs]
            acc = [_pd(att[h], tuple(part[:lo] for part in v_s[h])) for h in hs]
        for s in range(HG_SUB):
            for h in hs:
                c_b = cum[h][lo:hi]
                dec = jnp.exp(jnp.minimum(c_b - c_b[s:s + 1, :], 0.0))
                col = jnp.sum(qf[h][lo:hi] * dec * kf[h][lo + s:lo + s + 1, :], axis=-1, keepdims=True)
                col = jnp.where(trow >= s, col, 0.0)
                acc[h] = acc[h] + col * v[h][lo + s:lo + s + 1, :]
        for h in hs:
            rows[h].append(acc[h])

    for h in hs:
        out = o[h] + jnp.concatenate(rows[h], axis=0)
        o_ref[:, cols[h]] = (_rms(out, on_ref[:, cols[h]]) * _silu(g_ref[:, cols[h]])).astype(o_ref.dtype)


def _hgrn2(proj, lb_logits, onorm, layer, elayer, bsz, seq, a_width, b_width):
    t = proj.shape[0]
    heads = b_width // B_HEAD
    hgroups = heads // HG_HEADS
    nchunk = seq // HG_CHUNK
    width = HG_HEADS * B_HEAD
    off = 2 * a_width // width

    def col(which):
        return lambda b, h, c: (b * nchunk + c, off + which * hgroups + h)

    blk = (HG_CHUNK, width)
    depth = lb_logits.shape[0]
    return pl.pallas_call(
        functools.partial(_hgrn2_body, layer=layer),
        grid=(bsz, hgroups, nchunk),
        in_specs=[
            pl.BlockSpec(blk, col(0)),
            pl.BlockSpec(blk, col(1)),
            pl.BlockSpec(blk, col(2)),
            pl.BlockSpec(blk, col(3)),
            pl.BlockSpec((depth, width), lambda b, h, c: (0, h)),
            pl.BlockSpec((None, 1, width), lambda b, h, c: (elayer, 0, h)),
        ],
        out_specs=pl.BlockSpec(blk, lambda b, h, c: (b * nchunk + c, h)),
        out_shape=jax.ShapeDtypeStruct((t, b_width), BF16),
        scratch_shapes=[pltpu.VMEM((HG_HEADS, B_HEAD, B_HEAD), F32)],
        compiler_params=_cparams(("parallel", "parallel", "arbitrary")),
        name="hgrn2",
    )(proj, proj, proj, proj, lb_logits, onorm)


def _rwkv_mix_body(x_ref, xp_ref, g_ref, mix_ref, *o_refs, tm, seq):
    i = pl.program_id(0)
    g = g_ref[...]
    hn = _rms(x_ref[...], g)
    hp = _rms(xp_ref[...], g)[7:8, :]
    hp = jnp.where((i * tm) % seq == 0, 0.0, hp)
    row = lax.broadcasted_iota(jnp.int32, hn.shape, 0)
    prev = jnp.where(row == 0, hp, pltpu.roll(hn, 1, 0))
    xx = prev - hn
    for j, o_ref in enumerate(o_refs):
        o_ref[...] = (hn + xx * mix_ref[j:j + 1, :]).astype(BF16)


def _rwkv_mix(h, g, mix, layer, seq, tm=256):
    t, d = h.shape
    nmix = mix.shape[1]
    sub = 8
    return pl.pallas_call(
        functools.partial(_rwkv_mix_body, tm=tm, seq=seq),
        grid=(t // tm,),
        in_specs=[
            pl.BlockSpec((tm, d), lambda i: (i, 0)),
            pl.BlockSpec((sub, d), lambda i: (jnp.maximum(i * (tm // sub) - 1, 0), 0)),
            pl.BlockSpec((1, d), lambda i: (0, 0)),
            pl.BlockSpec((None, nmix, d), lambda i: (layer, 0, 0)),
        ],
        out_specs=[pl.BlockSpec((tm, d), lambda i: (i, 0)) for _ in range(nmix)],
        out_shape=[jax.ShapeDtypeStruct((t, d), BF16) for _ in range(nmix)],
        compiler_params=_cparams(("parallel",)),
        name="rwkv_mix",
    )(h, h, g, mix)


def _rwkv_prep_body(*refs, vres):
    if vres:
        (hw_ref, ha_ref, hg_ref, hv_ref, v_ref, vf_ref, w2_ref, a2_ref, g2_ref, v2_ref,
         w0_ref, a0_ref, v0_ref, lw_ref, a_ref, g_ref, vo_ref) = refs
    else:
        (hw_ref, ha_ref, hg_ref, w2_ref, a2_ref, g2_ref, w0_ref, a0_ref,
         lw_ref, a_ref, g_ref) = refs
    z = -(w0_ref[...] + _dot(hw_ref[...].astype(BF16), w2_ref[...]))
    softplus = jnp.maximum(z, 0.0) + jnp.log(1.0 + jnp.exp(-jnp.abs(z)))
    w = -softplus - 0.5
    lw_ref[...] = -jnp.exp(w)
    a_ref[...] = _sigmoid(a0_ref[...] + _dot(ha_ref[...].astype(BF16), a2_ref[...]))
    g_ref[...] = _dot(hg_ref[...].astype(BF16), g2_ref[...])
    if vres:
        v = v_ref[...]
        mv = _sigmoid(v0_ref[...] + _dot(hv_ref[...].astype(BF16), v2_ref[...]))
        vo_ref[...] = v + (vf_ref[...] - v) * mv


def _rwkv_prep(hw, ha, hg, w2, a2, g2, w0, a0, layer, vres=None, tm=256):
    t = hw.shape[0]
    d = w2.shape[-1]

    def tok(n):
        return pl.BlockSpec((tm, n), lambda i: (i, 0))

    def wgt(k):
        return pl.BlockSpec((None, k, d), lambda i: (layer, 0, 0))

    def vec(idx):
        return pl.BlockSpec((None, 1, d), lambda i: (idx, 0, 0))

    if vres is None:
        args = (hw, ha, hg, w2, a2, g2, w0, a0)
        in_specs = [tok(hw.shape[1]), tok(ha.shape[1]), tok(hg.shape[1]),
                    wgt(w2.shape[1]), wgt(a2.shape[1]), wgt(g2.shape[1]), vec(layer), vec(layer)]
        nout = 3
    else:
        hv, v, vf, v2, v0, vl = vres
        args = (hw, ha, hg, hv, v, vf, w2, a2, g2, v2, w0, a0, v0)
        in_specs = [tok(hw.shape[1]), tok(ha.shape[1]), tok(hg.shape[1]), tok(hv.shape[1]), tok(d), tok(d),
                    wgt(w2.shape[1]), wgt(a2.shape[1]), wgt(g2.shape[1]),
                    pl.BlockSpec((None, v2.shape[1], d), lambda i: (vl, 0, 0)),
                    vec(layer), vec(layer), vec(vl)]
        nout = 4
    return pl.pallas_call(
        functools.partial(_rwkv_prep_body, vres=vres is not None),
        grid=(t // tm,),
        in_specs=in_specs,
        out_specs=[tok(d) for _ in range(nout)],
        out_shape=[jax.ShapeDtypeStruct((t, d), F32) for _ in range(nout)],
        compiler_params=_cparams(("parallel",)),
        name="rwkv_prep",
    )(*args)


RW_CHUNK = 64


RW_PAIRS = 4
P_INV = False
P_ATT = False
P_APPLY = False
P_STATE = False


def _inv_unit_lower(mats, n):
    row = lax.broadcasted_iota(jnp.int32, (n, n), 0)
    col = lax.broadcasted_iota(jnp.int32, (n, n), 1)
    eye = jnp.where(row == col, 1.0, 0.0)
    ts = [eye + a for a in mats]
    ps = list(mats)
    k = 2
    while k < n:
        pss = [_sp(p, P_INV) for p in ps]
        ps = [_pd(s, s) for s in pss]
        ts = [t + _pd(_sp(t, P_INV), _sp(p, P_INV)) for t, p in zip(ts, ps)]
        k *= 2
    return ts


def _head_sum(x, m0):
    s0 = jnp.sum(jnp.where(m0, x, 0.0), axis=-1, keepdims=True)
    s1 = jnp.sum(jnp.where(m0, 0.0, x), axis=-1, keepdims=True)
    return jnp.where(m0, s0, s1)


def _rwkv_scan_body(r_ref, k_ref, v_ref, lw_ref, a_ref, g_ref, kk_ref, ka_ref, rk_ref, gg_ref, gb_ref,
                    o_ref, st_ref):
    c = pl.program_id(2)

    @pl.when(c == 0)
    def _():
        st_ref[...] = jnp.zeros_like(st_ref)

    n = RW_CHUNK
    pairs = range(RW_PAIRS)
    heads = [(p, hd) for p in pairs for hd in range(2)]
    cols = [slice(p * LANES, (p + 1) * LANES) for p in pairs]
    lane = lax.broadcasted_iota(jnp.int32, (1, LANES), 1)
    m0 = lane < C_HEAD
    m1 = jnp.logical_not(m0)
    tril, stril = _tri_masks(n)
    tri = jnp.where(tril, 1.0, 0.0).astype(BF16)

    st = [st_ref[p] for p in pairs]
    r = [r_ref[:, c] for c in cols]
    k = [k_ref[:, c] for c in cols]
    v = [v_ref[:, c] for c in cols]
    asig = [a_ref[:, c] for c in cols]

    cum = [_cumsum_rows(tri, lw_ref[:, c]) for c in cols]
    last = [x[n - 1:n, :] for x in cum]
    kkr = [k[p] * kk_ref[:, cols[p]] for p in pairs]
    kk = [x / jnp.maximum(jnp.sqrt(_head_sum(x * x, m0)), 1e-12) for x in kkr]
    kmod = [k[p] * (1.0 + (asig[p] - 1.0) * ka_ref[:, cols[p]]) for p in pairs]
    b = [kk[p] * asig[p] for p in pairs]
    rt = [r[p] * jnp.exp(cum[p]) for p in pairs]
    at = [-kk[p] * jnp.exp(cum[p] - lw_ref[:, cols[p]]) for p in pairs]
    einv = [jnp.exp(-x) for x in cum]
    bt = [_sp(b[p] * einv[p], P_ATT) for p in pairs]
    kt = [_sp(kmod[p] * einv[p], P_ATT) for p in pairs]
    eend = [jnp.exp(last[p] - cum[p]) for p in pairs]
    v_a = [_sp(x, P_APPLY) for x in v]
    st_s = [_sp(x, P_STATE) for x in st]

    at_h = [_sp(jnp.where(m0 if hd == 0 else m1, at[p], 0.0), P_ATT) for p, hd in heads]
    rt_h = [_sp(jnp.where(m0 if hd == 0 else m1, rt[p], 0.0), P_ATT) for p, hd in heads]
    a_ab = [jnp.where(stril, _pd(at_h[i], bt[p], _NT), 0.0) for i, (p, hd) in enumerate(heads)]
    t_inv = [_sp(x, P_APPLY) for x in _inv_unit_lower(a_ab, n)]
    a_ak = [_sp(jnp.where(stril, _pd(at_h[i], kt[p], _NT), 0.0), P_APPLY) for i, (p, hd) in enumerate(heads)]
    a_rb = [_sp(jnp.where(tril, _pd(rt_h[i], bt[p], _NT), 0.0), P_APPLY) for i, (p, hd) in enumerate(heads)]
    a_rk = [_sp(jnp.where(tril, _pd(rt_h[i], kt[p], _NT), 0.0), P_APPLY) for i, (p, hd) in enumerate(heads)]
    av = [_pd(a_ak[i], v_a[p]) for i, (p, hd) in enumerate(heads)]
    y_v = [_pd(a_rk[i], v_a[p]) for i, (p, hd) in enumerate(heads)]

    x = [_pd(_sp(at[p], P_STATE), st_s[p], _NT) + jnp.where(m0, av[2 * p], av[2 * p + 1]) for p in pairs]
    x_s = [_sp(t, P_APPLY) for t in x]
    u = [jnp.where(m0, _pd(t_inv[2 * p], x_s[p]), _pd(t_inv[2 * p + 1], x_s[p])) for p in pairs]
    u_a = [_sp(t, P_APPLY) for t in u]
    y = [_pd(_sp(rt[p], P_STATE), st_s[p], _NT)
         + jnp.where(m0, _pd(a_rb[2 * p], u_a[p]) + y_v[2 * p], _pd(a_rb[2 * p + 1], u_a[p]) + y_v[2 * p + 1])
         for p in pairs]

    row = lax.broadcasted_iota(jnp.int32, (LANES, LANES), 0)
    colm = lax.broadcasted_iota(jnp.int32, (LANES, LANES), 1)
    bdiag = (row < C_HEAD) == (colm < C_HEAD)
    for p in pairs:
        upd = (_pd(_sp(u[p], P_STATE), _sp(b[p] * eend[p], P_STATE), _TN)
               + _pd(_sp(v[p], P_STATE), _sp(kmod[p] * eend[p], P_STATE), _TN))
        st_ref[p] = st[p] * jnp.exp(last[p]) + jnp.where(bdiag, upd, 0.0)

    inv_n = 1.0 / C_HEAD
    for p in pairs:
        mu = _head_sum(y[p], m0) * inv_n
        dy = y[p] - mu
        var = _head_sum(dy * dy, m0) * inv_n
        yn = dy * lax.rsqrt(var + C_GN_EPS) * gg_ref[:, cols[p]] + gb_ref[:, cols[p]]
        bonus = _head_sum(r[p] * kmod[p] * rk_ref[:, cols[p]], m0) * v[p]
        o_ref[:, cols[p]] = ((yn + bonus) * g_ref[:, cols[p]]).astype(o_ref.dtype)


def _rwkv_scan(r, k, v, lw, a, g, kk, ka, rk, gg, gb, layer, bsz, seq):
    t, d = r.shape
    nchunk = seq // RW_CHUNK
    width = RW_PAIRS * LANES
    blk = pl.BlockSpec((RW_CHUNK, width), lambda b, p, c: (b * nchunk + c, p))
    vec = pl.BlockSpec((None, 1, width), lambda b, p, c: (layer, 0, p))
    return pl.pallas_call(
        _rwkv_scan_body,
        grid=(bsz, d // width, nchunk),
        in_specs=[blk] * 6 + [vec] * 5,
        out_specs=blk,
        out_shape=jax.ShapeDtypeStruct((t, d), BF16),
        scratch_shapes=[pltpu.VMEM((RW_PAIRS, LANES, LANES), F32)],
        compiler_params=_cparams(("parallel", "parallel", "arbitrary")),
        name="rwkv_scan",
    )(r, k, v, lw, a, g, kk, ka, rk, gg, gb)


def _pad_cols(w, n):
    return jnp.pad(w, ((0, 0), (0, 0), (0, n - w.shape[-1])))


def _pad_rows(w, n):
    return jnp.pad(w, ((0, 0), (0, n - w.shape[1]), (0, 0)))


def kernel(x, p, norms, final_norm, ffn_wg, ffn_wu, ffn_wd, ple_wp, ple_wg, e_w_in, e_w_out, a_vnorm, a_ws, a_bs, b_onorm, b_lb_logits, c_mix, c_wr, c_wk, c_wv, c_wo, c_w0, c_w1, c_w2, c_a0, c_a1, c_a2, c_g1, c_g2, c_kk, c_ka, c_rk, c_gn_g, c_gn_b, c_v0, c_v1, c_v2):
    bsz, seq, d = x.shape
    depth = p.shape[0]
    t = bsz * seq
    a_width = a_vnorm.shape[-1]
    b_width = b_onorm.shape[-1]

    bf = lambda w: w.astype(BF16)
    wg_b, wu_b, wd_b = bf(ffn_wg), bf(ffn_wu), bf(ffn_wd)
    ple_wp_b, ple_wg_b = bf(ple_wp), bf(ple_wg)
    e_in_b = bf(e_w_in)
    e_out_b = bf(e_w_out)
    wr_b, wk_b, wv_b, wo_b = bf(c_wr), bf(c_wk), bf(c_wv), bf(c_wo)
    w1_b = bf(_pad_cols(c_w1, LORA_PAD))
    w2_b = bf(_pad_rows(c_w2, LORA_PAD))
    a1_b = bf(_pad_cols(c_a1, LORA_PAD))
    a2_b = bf(_pad_rows(c_a2, LORA_PAD))
    g1_b, g2_b = bf(c_g1), bf(c_g2)
    v1_b = bf(_pad_cols(c_v1, LORA_PAD))
    v2_b = bf(_pad_rows(c_v2, LORA_PAD))

    vec3 = lambda w: w.reshape(w.shape[0], 1, -1)
    a_vnorm3, b_onorm3 = vec3(a_vnorm), vec3(b_onorm)
    a_bs4 = a_bs.reshape(a_bs.shape + (1,))
    w0_3, a0_3, v0_3 = vec3(c_w0), vec3(c_a0), vec3(c_v0)
    kk3, ka3, rk3, gg3, gb3 = vec3(c_kk), vec3(c_ka), vec3(c_rk), vec3(c_gn_g), vec3(c_gn_b)
    fg = final_norm.reshape(1, d)

    h = x.reshape(t, d)
    p2 = p.reshape(depth, t, p.shape[-1])
    v_first = None
    for i in range(depth):
        j = i // 2
        h = _ffn(h, norms[i, 0].reshape(1, d), wg_b, wu_b, wd_b, i, 0)
        g1n = norms[i, 1].reshape(1, d)
        if i % 2 == 0:
            proj = _nmm(h, g1n, e_in_b, j)
            a_out = _gmlp(proj, a_vnorm3, a_ws, a_bs4, j, a_width)
            b_out = _hgrn2(proj, b_lb_logits, b_onorm3, i, j, bsz, seq, a_width, b_width)
            h = _mm2_res(h, a_out, b_out, e_out_b, j)
        else:
            xr, xw, xk, xv, xa, xg = _rwkv_mix(h, g1n, c_mix, j, seq)
            r = _mm(xr, wr_b, j)
            k = _mm(xk, wk_b, j)
            v = _mm(xv, wv_b, j)
            hw = _mm(xw, w1_b, j, act="tanh")
            ha = _mm(xa, a1_b, j)
            hg = _mm(xg, g1_b, j, act="sigmoid")
            if j == 0:
                lw, a, g = _rwkv_prep(hw, ha, hg, w2_b, a2_b, g2_b, w0_3, a0_3, j)
                v_first = v
            else:
                hv = _mm(xv, v1_b, j - 1)
                lw, a, g, v = _rwkv_prep(hw, ha, hg, w2_b, a2_b, g2_b, w0_3, a0_3, j,
                                         vres=(hv, v, v_first, v2_b, v0_3, j - 1))
            y = _rwkv_scan(r, k, v, lw, a, g, kk3, ka3, rk3, gg3, gb3, j, bsz, seq)
            h = _mm_res(h, y, wo_b, j)
        h = _ffn(h, norms[i, 2].reshape(1, d), wg_b, wu_b, wd_b, i, 1)
        h = _ple(h, norms[i, 3].reshape(1, d), p2, ple_wg_b, ple_wp_b, fg, i, final=(i == depth - 1))
    return h.reshape(bsz, seq, d)
```

```python
import functools

import jax
import jax.numpy as jnp
from jax import lax
from jax.experimental import pallas as pl
from jax.experimental.pallas import tpu as pltpu

F32 = jnp.float32
BF16 = jnp.bfloat16

LANES = 128
RMS_EPS = 1e-6
A_CHUNK = 128
B_HEAD = 128
B_MIN_F = 1e-30
C_HEAD = 64
C_GN_EPS = 64e-5
LORA_PAD = 128

VMEM_LIMIT = 56 * 1024 * 1024

_NN = ((1,), (0,))
_NT = ((1,), (1,))
_TN = ((0,), (0,))


def _cparams(sem):
    return pltpu.CompilerParams(dimension_semantics=sem, vmem_limit_bytes=VMEM_LIMIT)


def _rms(x, g, eps=RMS_EPS):
    return x * lax.rsqrt(jnp.mean(x * x, axis=-1, keepdims=True) + eps) * g


def _sigmoid(x):
    return 1.0 / (1.0 + jnp.exp(-x))


def _silu(x):
    return x * _sigmoid(x)


def _gelu_tanh(x):
    return 0.5 * x * (1.0 + jnp.tanh(0.7978845608028654 * (x + 0.044715 * (x * x * x))))


def _dot(a, b, dims=_NN):
    return lax.dot_general(a, b, (dims, ((), ())), preferred_element_type=F32)


def _sp(x, lo):
    hi = x.astype(BF16)
    if not lo:
        return (hi,)
    return (hi, (x - hi.astype(F32)).astype(BF16))


def _pd(a, b, dims=_NN):
    out = _dot(a[0], b[0], dims)
    extra = None
    if len(b) > 1:
        extra = _dot(a[0], b[1], dims)
    if len(a) > 1:
        t = _dot(a[1], b[0], dims)
        extra = t if extra is None else extra + t
    return out if extra is None else out + extra


def _cumsum_rows(tri_bf16, x):
    hi = x.astype(BF16)
    r1 = x - hi.astype(F32)
    mid = r1.astype(BF16)
    lo = (r1 - mid.astype(F32)).astype(BF16)
    return _dot(tri_bf16, hi) + (_dot(tri_bf16, mid) + _dot(tri_bf16, lo))


def _tri_masks(n):
    row = lax.broadcasted_iota(jnp.int32, (n, n), 0)
    col = lax.broadcasted_iota(jnp.int32, (n, n), 1)
    return col <= row, col < row


def _ffn_body(x_ref, g_ref, wg_ref, wu_ref, wd_ref, o_ref, xn_ref, acc_ref):
    j = pl.program_id(1)

    @pl.when(j == 0)
    def _():
        xn_ref[...] = _rms(x_ref[...], g_ref[...]).astype(BF16)
        acc_ref[...] = jnp.zeros_like(acc_ref)

    xn = xn_ref[...]
    gate = _dot(xn, wg_ref[...])
    up = _dot(xn, wu_ref[...])
    hid = (_silu(gate) * up).astype(BF16)
    acc_ref[...] += _dot(hid, wd_ref[...])

    @pl.when(j == pl.num_programs(1) - 1)
    def _():
        o_ref[...] = x_ref[...] + 0.5 * acc_ref[...]


def _ffn(h, g, wg, wu, wd, layer, half, tm=512, tf=512):
    t, d = h.shape
    f = wg.shape[-1]
    return pl.pallas_call(
        _ffn_body,
        grid=(t // tm, f // tf),
        in_specs=[
            pl.BlockSpec((tm, d), lambda i, j: (i, 0)),
            pl.BlockSpec((1, d), lambda i, j: (0, 0)),
            pl.BlockSpec((None, None, d, tf), lambda i, j: (layer, half, 0, j)),
            pl.BlockSpec((None, None, d, tf), lambda i, j: (layer, half, 0, j)),
            pl.BlockSpec((None, None, tf, d), lambda i, j: (layer, half, j, 0)),
        ],
        out_specs=pl.BlockSpec((tm, d), lambda i, j: (i, 0)),
        out_shape=jax.ShapeDtypeStruct((t, d), F32),
        scratch_shapes=[pltpu.VMEM((tm, d), BF16), pltpu.VMEM((tm, d), F32)],
        compiler_params=_cparams(("parallel", "arbitrary")),
        name="ffn",
    )(h, g, wg, wu, wd)


def _mm_body(x_ref, w_ref, o_ref, *, act):
    y = _dot(x_ref[...], w_ref[...])
    if act == "tanh":
        y = jnp.tanh(y)
    elif act == "sigmoid":
        y = _sigmoid(y)
    o_ref[...] = y.astype(o_ref.dtype)


def _mm(x, w, layer, act=None, tm=1024, tn=1024):
    t, k = x.shape
    n = w.shape[-1]
    tm, tn = min(tm, t), min(tn, n)
    return pl.pallas_call(
        functools.partial(_mm_body, act=act),
        grid=(t // tm, n // tn),
        in_specs=[
            pl.BlockSpec((tm, k), lambda i, j: (i, 0)),
            pl.BlockSpec((None, k, tn), lambda i, j: (layer, 0, j)),
        ],
        out_specs=pl.BlockSpec((tm, tn), lambda i, j: (i, j)),
        out_shape=jax.ShapeDtypeStruct((t, n), F32),
        compiler_params=_cparams(("parallel", "parallel")),
        name="mm",
    )(x, w)


def _mm_res_body(h_ref, x_ref, w_ref, o_ref):
    o_ref[...] = h_ref[...] + _dot(x_ref[...], w_ref[...])


def _mm_res(h, x, w, layer, tm=1024, tn=1024):
    t, k = x.shape
    n = w.shape[-1]
    tm = min(tm, t)
    return pl.pallas_call(
        _mm_res_body,
        grid=(t // tm, n // tn),
        in_specs=[
            pl.BlockSpec((tm, tn), lambda i, j: (i, j)),
            pl.BlockSpec((tm, k), lambda i, j: (i, 0)),
            pl.BlockSpec((None, k, tn), lambda i, j: (layer, 0, j)),
        ],
        out_specs=pl.BlockSpec((tm, tn), lambda i, j: (i, j)),
        out_shape=jax.ShapeDtypeStruct((t, n), F32),
        compiler_params=_cparams(("parallel", "parallel")),
        name="mm_res",
    )(h, x, w)


def _mm2_res_body(h_ref, xa_ref, xb_ref, wa_ref, wb_ref, o_ref):
    o_ref[...] = h_ref[...] + (_dot(xa_ref[...], wa_ref[...]) + _dot(xb_ref[...], wb_ref[...]))


def _mm2_res(h, xa, xb, w, layer, tm=1024, tn=1024):
    t, ka = xa.shape
    kb = xb.shape[1]
    n = w.shape[-1]
    assert ka == kb
    tm = min(tm, t)
    return pl.pallas_call(
        _mm2_res_body,
        grid=(t // tm, n // tn),
        in_specs=[
            pl.BlockSpec((tm, tn), lambda i, j: (i, j)),
            pl.BlockSpec((tm, ka), lambda i, j: (i, 0)),
            pl.BlockSpec((tm, kb), lambda i, j: (i, 0)),
            pl.BlockSpec((None, ka, tn), lambda i, j: (layer, 0, j)),
            pl.BlockSpec((None, kb, tn), lambda i, j: (layer, 1, j)),
        ],
        out_specs=pl.BlockSpec((tm, tn), lambda i, j: (i, j)),
        out_shape=jax.ShapeDtypeStruct((t, n), F32),
        compiler_params=_cparams(("parallel", "parallel")),
        name="mm2_res",
    )(h, xa, xb, w, w)


def _nmm_body(x_ref, g_ref, w_ref, o_ref, xn_ref):
    @pl.when(pl.program_id(1) == 0)
    def _():
        xn_ref[...] = _rms(x_ref[...], g_ref[...]).astype(BF16)

    o_ref[...] = _dot(xn_ref[...], w_ref[...])


def _nmm(h, g, w, layer, tm=1024, tn=1024):
    t, d = h.shape
    n = w.shape[-1]
    tm = min(tm, t)
    return pl.pallas_call(
        _nmm_body,
        grid=(t // tm, n // tn),
        in_specs=[
            pl.BlockSpec((tm, d), lambda i, j: (i, 0)),
            pl.BlockSpec((1, d), lambda i, j: (0, 0)),
            pl.BlockSpec((None, d, tn), lambda i, j: (layer, 0, j)),
        ],
        out_specs=pl.BlockSpec((tm, tn), lambda i, j: (i, j)),
        out_shape=jax.ShapeDtypeStruct((t, n), F32),
        scratch_shapes=[pltpu.VMEM((tm, d), BF16)],
        compiler_params=_cparams(("parallel", "arbitrary")),
        name="norm_mm",
    )(h, g, w)


def _ple_body(h_ref, g_ref, p_ref, wg_ref, wp_ref, fg_ref, o_ref, *, final):
    h = h_ref[...]
    xn = _rms(h, g_ref[...]).astype(BF16)
    gate = _sigmoid(_dot(xn, wg_ref[...]))
    pe = _dot(p_ref[...].astype(BF16), wp_ref[...])
    out = h + gate * pe
    if final:
        out = _rms(out, fg_ref[...])
    o_ref[...] = out


def _ple(h, g, p, wg, wp, fg, layer, final, tm=512):
    t, d = h.shape
    pd = p.shape[-1]
    return pl.pallas_call(
        functools.partial(_ple_body, final=final),
        grid=(t // tm,),
        in_specs=[
            pl.BlockSpec((tm, d), lambda i: (i, 0)),
            pl.BlockSpec((1, d), lambda i: (0, 0)),
            pl.BlockSpec((None, tm, pd), lambda i: (layer, i, 0)),
            pl.BlockSpec((None, d, d), lambda i: (layer, 0, 0)),
            pl.BlockSpec((None, pd, d), lambda i: (layer, 0, 0)),
            pl.BlockSpec((1, d), lambda i: (0, 0)),
        ],
        out_specs=pl.BlockSpec((tm, d), lambda i: (i, 0)),
        out_shape=jax.ShapeDtypeStruct((t, d), F32),
        compiler_params=_cparams(("parallel",)),
        name="ple",
    )(h, g, p, wg, wp, fg)


GMLP_CHUNKS = 2


def _gmlp_body(u_ref, v_ref, gain_ref, ws_ref, bs_ref, o_ref):
    groups = ws_ref.shape[0]
    tril, _ = _tri_masks(A_CHUNK)
    for g in range(groups):
        w = jnp.where(tril, ws_ref[g], 0.0).astype(BF16)
        bias = bs_ref[g]
        cols = slice(g * A_CHUNK, (g + 1) * A_CHUNK)
        for c in range(GMLP_CHUNKS):
            rows = slice(c * A_CHUNK, (c + 1) * A_CHUNK)
            u = _gelu_tanh(u_ref[rows, cols])
            v = _gelu_tanh(v_ref[rows, cols])
            vg = _rms(v, gain_ref[:, cols])
            s = _dot(w, vg.astype(BF16)) + bias
            o_ref[rows, cols] = (u * s).astype(o_ref.dtype)


def _gmlp(proj, gain, ws, bs, layer, a_width):
    t = proj.shape[0]
    groups = a_width // A_CHUNK
    tm = GMLP_CHUNKS * A_CHUNK
    return pl.pallas_call(
        _gmlp_body,
        grid=(t // tm,),
        in_specs=[
            pl.BlockSpec((tm, a_width), lambda c: (c, 0)),
            pl.BlockSpec((tm, a_width), lambda c: (c, 1)),
            pl.BlockSpec((None, 1, a_width), lambda c: (layer, 0, 0)),
            pl.BlockSpec((None, groups, A_CHUNK, A_CHUNK), lambda c: (layer, 0, 0, 0)),
            pl.BlockSpec((None, groups, A_CHUNK, 1), lambda c: (layer, 0, 0, 0)),
        ],
        out_specs=pl.BlockSpec((tm, a_width), lambda c: (c, 0)),
        out_shape=jax.ShapeDtypeStruct((t, a_width), BF16),
        compiler_params=_cparams(("parallel",)),
        name="gmlp",
    )(proj, proj, gain, ws, bs)


HG_CHUNK = 64
HG_SUB = 16


HG_HEADS = 4
P_HG = False


def _hgrn2_body(q_ref, f_ref, i_ref, g_ref, lbl_ref, on_ref, o_ref, st_ref, *, layer):
    c = pl.program_id(2)

    @pl.when(c == 0)
    def _():
        st_ref[...] = jnp.zeros_like(st_ref)

    logits = lbl_ref[...]
    e = jnp.exp(logits - jnp.max(logits, axis=0, keepdims=True))
    probs = e / jnp.sum(e, axis=0, keepdims=True)
    lb = jnp.zeros((1, logits.shape[1]), F32)
    for r in range(1, layer + 1):
        lb = lb + probs[r:r + 1, :]

    n = HG_CHUNK
    hs = range(HG_HEADS)
    cols = [slice(hd * B_HEAD, (hd + 1) * B_HEAD) for hd in hs]
    tril, _ = _tri_masks(n)
    tri = jnp.where(tril, 1.0, 0.0).astype(BF16)

    st = [st_ref[hd] for hd in hs]
    v = [i_ref[:, c] for c in cols]
    f = [lb[:, c] + (1.0 - lb[:, c]) * _sigmoid(f_ref[:, c]) for c in cols]
    kf = [1.0 - x for x in f]
    qf = [_silu(q_ref[:, c]) for c in cols]
    cum = [_cumsum_rows(tri, jnp.log(jnp.maximum(x, B_MIN_F))) for x in f]
    last = [x[n - 1:n, :] for x in cum]

    st_s = [_sp(x, P_HG) for x in st]
    v_s = [_sp(x, P_HG) for x in v]
    o = [_pd(_sp(qf[h] * jnp.exp(cum[h]), P_HG), st_s[h], _NT) for h in hs]
    for h in hs:
        kend = kf[h] * jnp.exp(last[h] - cum[h])
        st_ref[h] = st[h] * jnp.exp(last[h]) + _pd(v_s[h], _sp(kend, P_HG), _TN)

    nsub = n // HG_SUB
    trow = lax.broadcasted_iota(jnp.int32, (HG_SUB, 1), 0)
    rows = [[] for _ in hs]
    for bi in range(nsub):
        lo, hi = bi * HG_SUB, (bi + 1) * HG_SUB
        acc = [jnp.zeros((HG_SUB, B_HEAD), F32) for _ in hs]
        if bi > 0:
            ref = [cum[h][lo - 1:lo, :] for h in hs]
            qh = [_sp(qf[h][lo:hi] * jnp.exp(cum[h][lo:hi] - ref[h]), P_HG) for h in hs]
            kh = [_sp(kf[h][:lo] * jnp.exp(ref[h] - cum[h][:lo]), P_HG) for h in hs]
            att = [_sp(_pd(qh[h], kh[h], _NT), P_HG) for h in hs]
            acc = [_pd(att[h], tuple(part[:lo] for part in v_s[h])) for h in hs]
        for s in range(HG_SUB):
            for h in hs:
                c_b = cum[h][lo:hi]
                dec = jnp.exp(jnp.minimum(c_b - c_b[s:s + 1, :], 0.0))
                col = jnp.sum(qf[h][lo:hi] * dec * kf[h][lo + s:lo + s + 1, :], axis=-1, keepdims=True)
                col = jnp.where(trow >= s, col, 0.0)
                acc[h] = acc[h] + col * v[h][lo + s:lo + s + 1, :]
        for h in hs:
            rows[h].append(acc[h])

    for h in hs:
        out = o[h] + jnp.concatenate(rows[h], axis=0)
        o_ref[:, cols[h]] = (_rms(out, on_ref[:, cols[h]]) * _silu(g_ref[:, cols[h]])).astype(o_ref.dtype)


def _hgrn2(proj, lb_logits, onorm, layer, elayer, bsz, seq, a_width, b_width):
    t = proj.shape[0]
    heads = b_width // B_HEAD
    hgroups = heads // HG_HEADS
    nchunk = seq // HG_CHUNK
    width = HG_HEADS * B_HEAD
    off = 2 * a_width // width

    def col(which):
        return lambda b, h, c: (b * nchunk + c, off + which * hgroups + h)

    blk = (HG_CHUNK, width)
    depth = lb_logits.shape[0]
    return pl.pallas_call(
        functools.partial(_hgrn2_body, layer=layer),
        grid=(bsz, hgroups, nchunk),
        in_specs=[
            pl.BlockSpec(blk, col(0)),
            pl.BlockSpec(blk, col(1)),
            pl.BlockSpec(blk, col(2)),
            pl.BlockSpec(blk, col(3)),
            pl.BlockSpec((depth, width), lambda b, h, c: (0, h)),
            pl.BlockSpec((None, 1, width), lambda b, h, c: (elayer, 0, h)),
        ],
        out_specs=pl.BlockSpec(blk, lambda b, h, c: (b * nchunk + c, h)),
        out_shape=jax.ShapeDtypeStruct((t, b_width), BF16),
        scratch_shapes=[pltpu.VMEM((HG_HEADS, B_HEAD, B_HEAD), F32)],
        compiler_params=_cparams(("parallel", "parallel", "arbitrary")),
        name="hgrn2",
    )(proj, proj, proj, proj, lb_logits, onorm)


def _rwkv_mix_body(x_ref, xp_ref, g_ref, mix_ref, *o_refs, tm, seq):
    i = pl.program_id(0)
    g = g_ref[...]
    hn = _rms(x_ref[...], g)
    hp = _rms(xp_ref[...], g)[7:8, :]
    hp = jnp.where((i * tm) % seq == 0, 0.0, hp)
    row = lax.broadcasted_iota(jnp.int32, hn.shape, 0)
    prev = jnp.where(row == 0, hp, pltpu.roll(hn, 1, 0))
    xx = prev - hn
    for j, o_ref in enumerate(o_refs):
        o_ref[...] = (hn + xx * mix_ref[j:j + 1, :]).astype(BF16)


def _rwkv_mix(h, g, mix, layer, seq, tm=256):
    t, d = h.shape
    nmix = mix.shape[1]
    sub = 8
    return pl.pallas_call(
        functools.partial(_rwkv_mix_body, tm=tm, seq=seq),
        grid=(t // tm,),
        in_specs=[
            pl.BlockSpec((tm, d), lambda i: (i, 0)),
            pl.BlockSpec((sub, d), lambda i: (jnp.maximum(i * (tm // sub) - 1, 0), 0)),
            pl.BlockSpec((1, d), lambda i: (0, 0)),
            pl.BlockSpec((None, nmix, d), lambda i: (layer, 0, 0)),
        ],
        out_specs=[pl.BlockSpec((tm, d), lambda i: (i, 0)) for _ in range(nmix)],
        out_shape=[jax.ShapeDtypeStruct((t, d), BF16) for _ in range(nmix)],
        compiler_params=_cparams(("parallel",)),
        name="rwkv_mix",
    )(h, h, g, mix)


def _rwkv_prep_body(*refs, vres):
    if vres:
        (hw_ref, ha_ref, hg_ref, hv_ref, v_ref, vf_ref, w2_ref, a2_ref, g2_ref, v2_ref,
         w0_ref, a0_ref, v0_ref, lw_ref, a_ref, g_ref, vo_ref) = refs
    else:
        (hw_ref, ha_ref, hg_ref, w2_ref, a2_ref, g2_ref, w0_ref, a0_ref,
         lw_ref, a_ref, g_ref) = refs
    z = -(w0_ref[...] + _dot(hw_ref[...].astype(BF16), w2_ref[...]))
    softplus = jnp.maximum(z, 0.0) + jnp.log(1.0 + jnp.exp(-jnp.abs(z)))
    w = -softplus - 0.5
    lw_ref[...] = -jnp.exp(w)
    a_ref[...] = _sigmoid(a0_ref[...] + _dot(ha_ref[...].astype(BF16), a2_ref[...]))
    g_ref[...] = _dot(hg_ref[...].astype(BF16), g2_ref[...])
    if vres:
        v = v_ref[...]
        mv = _sigmoid(v0_ref[...] + _dot(hv_ref[...].astype(BF16), v2_ref[...]))
        vo_ref[...] = v + (vf_ref[...] - v) * mv


def _rwkv_prep(hw, ha, hg, w2, a2, g2, w0, a0, layer, vres=None, tm=256):
    t = hw.shape[0]
    d = w2.shape[-1]

    def tok(n):
        return pl.BlockSpec((tm, n), lambda i: (i, 0))

    def wgt(k):
        return pl.BlockSpec((None, k, d), lambda i: (layer, 0, 0))

    def vec(idx):
        return pl.BlockSpec((None, 1, d), lambda i: (idx, 0, 0))

    if vres is None:
        args = (hw, ha, hg, w2, a2, g2, w0, a0)
        in_specs = [tok(hw.shape[1]), tok(ha.shape[1]), tok(hg.shape[1]),
                    wgt(w2.shape[1]), wgt(a2.shape[1]), wgt(g2.shape[1]), vec(layer), vec(layer)]
        nout = 3
    else:
        hv, v, vf, v2, v0, vl = vres
        args = (hw, ha, hg, hv, v, vf, w2, a2, g2, v2, w0, a0, v0)
        in_specs = [tok(hw.shape[1]), tok(ha.shape[1]), tok(hg.shape[1]), tok(hv.shape[1]), tok(d), tok(d),
                    wgt(w2.shape[1]), wgt(a2.shape[1]), wgt(g2.shape[1]),
                    pl.BlockSpec((None, v2.shape[1], d), lambda i: (vl, 0, 0)),
                    vec(layer), vec(layer), vec(vl)]
        nout = 4
    return pl.pallas_call(
        functools.partial(_rwkv_prep_body, vres=vres is not None),
        grid=(t // tm,),
        in_specs=in_specs,
        out_specs=[tok(d) for _ in range(nout)],
        out_shape=[jax.ShapeDtypeStruct((t, d), F32) for _ in range(nout)],
        compiler_params=_cparams(("parallel",)),
        name="rwkv_prep",
    )(*args)


RW_CHUNK = 64


RW_PAIRS = 8
P_INV = False
P_ATT = False
P_APPLY = False
P_STATE = False


def _inv_unit_lower(mats, n):
    row = lax.broadcasted_iota(jnp.int32, (n, n), 0)
    col = lax.broadcasted_iota(jnp.int32, (n, n), 1)
    eye = jnp.where(row == col, 1.0, 0.0)
    ts = [eye + a for a in mats]
    ps = list(mats)
    k = 2
    while k < n:
        pss = [_sp(p, P_INV) for p in ps]
        ps = [_pd(s, s) for s in pss]
        ts = [t + _pd(_sp(t, P_INV), _sp(p, P_INV)) for t, p in zip(ts, ps)]
        k *= 2
    return ts


def _head_sum(x, m0):
    s0 = jnp.sum(jnp.where(m0, x, 0.0), axis=-1, keepdims=True)
    s1 = jnp.sum(jnp.where(m0, 0.0, x), axis=-1, keepdims=True)
    return jnp.where(m0, s0, s1)


def _rwkv_scan_body(r_ref, k_ref, v_ref, lw_ref, a_ref, g_ref, kk_ref, ka_ref, rk_ref, gg_ref, gb_ref,
                    o_ref, st_ref):
    c = pl.program_id(2)

    @pl.when(c == 0)
    def _():
        st_ref[...] = jnp.zeros_like(st_ref)

    n = RW_CHUNK
    pairs = range(RW_PAIRS)
    heads = [(p, hd) for p in pairs for hd in range(2)]
    cols = [slice(p * LANES, (p + 1) * LANES) for p in pairs]
    lane = lax.broadcasted_iota(jnp.int32, (1, LANES), 1)
    m0 = lane < C_HEAD
    m1 = jnp.logical_not(m0)
    tril, stril = _tri_masks(n)
    tri = jnp.where(tril, 1.0, 0.0).astype(BF16)

    st = [st_ref[p] for p in pairs]
    r = [r_ref[:, c] for c in cols]
    k = [k_ref[:, c] for c in cols]
    v = [v_ref[:, c] for c in cols]
    asig = [a_ref[:, c] for c in cols]

    cum = [_cumsum_rows(tri, lw_ref[:, c]) for c in cols]
    last = [x[n - 1:n, :] for x in cum]
    kkr = [k[p] * kk_ref[:, cols[p]] for p in pairs]
    kk = [x / jnp.maximum(jnp.sqrt(_head_sum(x * x, m0)), 1e-12) for x in kkr]
    kmod = [k[p] * (1.0 + (asig[p] - 1.0) * ka_ref[:, cols[p]]) for p in pairs]
    b = [kk[p] * asig[p] for p in pairs]
    rt = [r[p] * jnp.exp(cum[p]) for p in pairs]
    at = [-kk[p] * jnp.exp(cum[p] - lw_ref[:, cols[p]]) for p in pairs]
    einv = [jnp.exp(-x) for x in cum]
    bt = [_sp(b[p] * einv[p], P_ATT) for p in pairs]
    kt = [_sp(kmod[p] * einv[p], P_ATT) for p in pairs]
    eend = [jnp.exp(last[p] - cum[p]) for p in pairs]
    v_a = [_sp(x, P_APPLY) for x in v]
    st_s = [_sp(x, P_STATE) for x in st]

    at_h = [_sp(jnp.where(m0 if hd == 0 else m1, at[p], 0.0), P_ATT) for p, hd in heads]
    rt_h = [_sp(jnp.where(m0 if hd == 0 else m1, rt[p], 0.0), P_ATT) for p, hd in heads]
    a_ab = [jnp.where(stril, _pd(at_h[i], bt[p], _NT), 0.0) for i, (p, hd) in enumerate(heads)]
    t_inv = [_sp(x, P_APPLY) for x in _inv_unit_lower(a_ab, n)]
    a_ak = [_sp(jnp.where(stril, _pd(at_h[i], kt[p], _NT), 0.0), P_APPLY) for i, (p, hd) in enumerate(heads)]
    a_rb = [_sp(jnp.where(tril, _pd(rt_h[i], bt[p], _NT), 0.0), P_APPLY) for i, (p, hd) in enumerate(heads)]
    a_rk = [_sp(jnp.where(tril, _pd(rt_h[i], kt[p], _NT), 0.0), P_APPLY) for i, (p, hd) in enumerate(heads)]
    av = [_pd(a_ak[i], v_a[p]) for i, (p, hd) in enumerate(heads)]
    y_v = [_pd(a_rk[i], v_a[p]) for i, (p, hd) in enumerate(heads)]

    x = [_pd(_sp(at[p], P_STATE), st_s[p], _NT) + jnp.where(m0, av[2 * p], av[2 * p + 1]) for p in pairs]
    x_s = [_sp(t, P_APPLY) for t in x]
    u = [jnp.where(m0, _pd(t_inv[2 * p], x_s[p]), _pd(t_inv[2 * p + 1], x_s[p])) for p in pairs]
    u_a = [_sp(t, P_APPLY) for t in u]
    y = [_pd(_sp(rt[p], P_STATE), st_s[p], _NT)
         + jnp.where(m0, _pd(a_rb[2 * p], u_a[p]) + y_v[2 * p], _pd(a_rb[2 * p + 1], u_a[p]) + y_v[2 * p + 1])
         for p in pairs]

    row = lax.broadcasted_iota(jnp.int32, (LANES, LANES), 0)
    colm = lax.broadcasted_iota(jnp.int32, (LANES, LANES), 1)
    bdiag = (row < C_HEAD) == (colm < C_HEAD)
    for p in pairs:
        upd = (_pd(_sp(u[p], P_STATE), _sp(b[p] * eend[p], P_STATE), _TN)
               + _pd(_sp(v[p], P_STATE), _sp(kmod[p] * eend[p], P_STATE), _TN))
        st_ref[p] = st[p] * jnp.exp(last[p]) + jnp.where(bdiag, upd, 0.0)

    inv_n = 1.0 / C_HEAD
    for p in pairs:
        mu = _head_sum(y[p], m0) * inv_n
        dy = y[p] - mu
        var = _head_sum(dy * dy, m0) * inv_n
        yn = dy * lax.rsqrt(var + C_GN_EPS) * gg_ref[:, cols[p]] + gb_ref[:, cols[p]]
        bonus = _head_sum(r[p] * kmod[p] * rk_ref[:, cols[p]], m0) * v[p]
        o_ref[:, cols[p]] = ((yn + bonus) * g_ref[:, cols[p]]).astype(o_ref.dtype)


def _rwkv_scan(r, k, v, lw, a, g, kk, ka, rk, gg, gb, layer, bsz, seq):
    t, d = r.shape
    nchunk = seq // RW_CHUNK
    width = RW_PAIRS * LANES
    blk = pl.BlockSpec((RW_CHUNK, width), lambda b, p, c: (b * nchunk + c, p))
    vec = pl.BlockSpec((None, 1, width), lambda b, p, c: (layer, 0, p))
    return pl.pallas_call(
        _rwkv_scan_body,
        grid=(bsz, d // width, nchunk),
        in_specs=[blk] * 6 + [vec] * 5,
        out_specs=blk,
        out_shape=jax.ShapeDtypeStruct((t, d), BF16),
        scratch_shapes=[pltpu.VMEM((RW_PAIRS, LANES, LANES), F32)],
        compiler_params=_cparams(("parallel", "parallel", "arbitrary")),
        name="rwkv_scan",
    )(r, k, v, lw, a, g, kk, ka, rk, gg, gb)


def _pad_cols(w, n):
    return jnp.pad(w, ((0, 0), (0, 0), (0, n - w.shape[-1])))


def _pad_rows(w, n):
    return jnp.pad(w, ((0, 0), (0, n - w.shape[1]), (0, 0)))


def kernel(x, p, norms, final_norm, ffn_wg, ffn_wu, ffn_wd, ple_wp, ple_wg, e_w_in, e_w_out, a_vnorm, a_ws, a_bs, b_onorm, b_lb_logits, c_mix, c_wr, c_wk, c_wv, c_wo, c_w0, c_w1, c_w2, c_a0, c_a1, c_a2, c_g1, c_g2, c_kk, c_ka, c_rk, c_gn_g, c_gn_b, c_v0, c_v1, c_v2):
    bsz, seq, d = x.shape
    depth = p.shape[0]
    t = bsz * seq
    a_width = a_vnorm.shape[-1]
    b_width = b_onorm.shape[-1]

    bf = lambda w: w.astype(BF16)
    wg_b, wu_b, wd_b = bf(ffn_wg), bf(ffn_wu), bf(ffn_wd)
    ple_wp_b, ple_wg_b = bf(ple_wp), bf(ple_wg)
    e_in_b = bf(e_w_in)
    e_out_b = bf(e_w_out)
    wr_b, wk_b, wv_b, wo_b = bf(c_wr), bf(c_wk), bf(c_wv), bf(c_wo)
    w1_b = bf(_pad_cols(c_w1, LORA_PAD))
    w2_b = bf(_pad_rows(c_w2, LORA_PAD))
    a1_b = bf(_pad_cols(c_a1, LORA_PAD))
    a2_b = bf(_pad_rows(c_a2, LORA_PAD))
    g1_b, g2_b = bf(c_g1), bf(c_g2)
    v1_b = bf(_pad_cols(c_v1, LORA_PAD))
    v2_b = bf(_pad_rows(c_v2, LORA_PAD))

    vec3 = lambda w: w.reshape(w.shape[0], 1, -1)
    a_vnorm3, b_onorm3 = vec3(a_vnorm), vec3(b_onorm)
    a_bs4 = a_bs.reshape(a_bs.shape + (1,))
    w0_3, a0_3, v0_3 = vec3(c_w0), vec3(c_a0), vec3(c_v0)
    kk3, ka3, rk3, gg3, gb3 = vec3(c_kk), vec3(c_ka), vec3(c_rk), vec3(c_gn_g), vec3(c_gn_b)
    fg = final_norm.reshape(1, d)

    h = x.reshape(t, d)
    p2 = p.reshape(depth, t, p.shape[-1])
    v_first = None
    for i in range(depth):
        j = i // 2
        h = _ffn(h, norms[i, 0].reshape(1, d), wg_b, wu_b, wd_b, i, 0)
        g1n = norms[i, 1].reshape(1, d)
        if i % 2 == 0:
            proj = _nmm(h, g1n, e_in_b, j)
            a_out = _gmlp(proj, a_vnorm3, a_ws, a_bs4, j, a_width)
            b_out = _hgrn2(proj, b_lb_logits, b_onorm3, i, j, bsz, seq, a_width, b_width)
            h = _mm2_res(h, a_out, b_out, e_out_b, j)
        else:
            xr, xw, xk, xv, xa, xg = _rwkv_mix(h, g1n, c_mix, j, seq)
            r = _mm(xr, wr_b, j)
            k = _mm(xk, wk_b, j)
            v = _mm(xv, wv_b, j)
            hw = _mm(xw, w1_b, j, act="tanh")
            ha = _mm(xa, a1_b, j)
            hg = _mm(xg, g1_b, j, act="sigmoid")
            if j == 0:
                lw, a, g = _rwkv_prep(hw, ha, hg, w2_b, a2_b, g2_b, w0_3, a0_3, j)
                v_first = v
            else:
                hv = _mm(xv, v1_b, j - 1)
                lw, a, g, v = _rwkv_prep(hw, ha, hg, w2_b, a2_b, g2_b, w0_3, a0_3, j,
                                         vres=(hv, v, v_first, v2_b, v0_3, j - 1))
            y = _rwkv_scan(r, k, v, lw, a, g, kk3, ka3, rk3, gg3, gb3, j, bsz, seq)
            h = _mm_res(h, y, wo_b, j)
        h = _ffn(h, norms[i, 2].reshape(1, d), wg_b, wu_b, wd_b, i, 1)
        h = _ple(h, norms[i, 3].reshape(1, d), p2, ple_wg_b, ple_wp_b, fg, i, final=(i == depth - 1))
    return h.reshape(bsz, seq, d)
```

```python
import functools

import jax
import jax.numpy as jnp
from jax import lax
from jax.experimental import pallas as pl
from jax.experimental.pallas import tpu as pltpu

F32 = jnp.float32
BF16 = jnp.bfloat16

LANES = 128
RMS_EPS = 1e-6
A_CHUNK = 128
B_HEAD = 128
B_MIN_F = 1e-30
C_HEAD = 64
C_GN_EPS = 64e-5
LORA_PAD = 128

VMEM_LIMIT = 56 * 1024 * 1024

_NN = ((1,), (0,))
_NT = ((1,), (1,))
_TN = ((0,), (0,))


def _cparams(sem):
    return pltpu.CompilerParams(dimension_semantics=sem, vmem_limit_bytes=VMEM_LIMIT)


def _rms(x, g, eps=RMS_EPS):
    return x * lax.rsqrt(jnp.mean(x * x, axis=-1, keepdims=True) + eps) * g


def _sigmoid(x):
    return 1.0 / (1.0 + jnp.exp(-x))


def _silu(x):
    return x * _sigmoid(x)


def _gelu_tanh(x):
    return 0.5 * x * (1.0 + jnp.tanh(0.7978845608028654 * (x + 0.044715 * (x * x * x))))


def _dot(a, b, dims=_NN):
    return lax.dot_general(a, b, (dims, ((), ())), preferred_element_type=F32)


def _sp(x, lo):
    hi = x.astype(BF16)
    if not lo:
        return (hi,)
    return (hi, (x - hi.astype(F32)).astype(BF16))


def _pd(a, b, dims=_NN):
    out = _dot(a[0], b[0], dims)
    extra = None
    if len(b) > 1:
        extra = _dot(a[0], b[1], dims)
    if len(a) > 1:
        t = _dot(a[1], b[0], dims)
        extra = t if extra is None else extra + t
    return out if extra is None else out + extra


def _cumsum_rows(tri_bf16, x):
    hi = x.astype(BF16)
    r1 = x - hi.astype(F32)
    mid = r1.astype(BF16)
    lo = (r1 - mid.astype(F32)).astype(BF16)
    return _dot(tri_bf16, hi) + (_dot(tri_bf16, mid) + _dot(tri_bf16, lo))


def _tri_masks(n):
    row = lax.broadcasted_iota(jnp.int32, (n, n), 0)
    col = lax.broadcasted_iota(jnp.int32, (n, n), 1)
    return col <= row, col < row


FFN_DOWN_COLS = 512


def _ffn_body(x_ref, g_ref, wg_ref, wu_ref, wd_ref, o_ref, xn_ref):
    j = pl.program_id(1)

    @pl.when(j == 0)
    def _():
        x = x_ref[...]
        xn_ref[...] = _rms(x, g_ref[...]).astype(BF16)
        o_ref[...] = x

    xn = xn_ref[...]
    gate = _dot(xn, wg_ref[...].astype(BF16))
    up = _dot(xn, wu_ref[...].astype(BF16))
    hid = (0.5 * _silu(gate) * up).astype(BF16)
    for c in range(0, o_ref.shape[1], FFN_DOWN_COLS):
        cols = slice(c, c + FFN_DOWN_COLS)
        o_ref[:, cols] += _dot(hid, wd_ref[:, cols].astype(BF16))


def _ffn(h, g, wg, wu, wd, layer, half, tm=1024, tf=256):
    t, d = h.shape
    f = wg.shape[-1]
    tm = min(tm, t)
    return pl.pallas_call(
        _ffn_body,
        grid=(t // tm, f // tf),
        in_specs=[
            pl.BlockSpec((tm, d), lambda i, j: (i, 0)),
            pl.BlockSpec((1, d), lambda i, j: (0, 0)),
            pl.BlockSpec((None, None, d, tf), lambda i, j: (layer, half, 0, j)),
            pl.BlockSpec((None, None, d, tf), lambda i, j: (layer, half, 0, j)),
            pl.BlockSpec((None, None, tf, d), lambda i, j: (layer, half, j, 0)),
        ],
        out_specs=pl.BlockSpec((tm, d), lambda i, j: (i, 0)),
        out_shape=jax.ShapeDtypeStruct((t, d), F32),
        scratch_shapes=[pltpu.VMEM((tm, d), BF16)],
        compiler_params=_cparams(("parallel", "arbitrary")),
        name="ffn",
    )(h, g, wg, wu, wd)


def _mm_body(x_ref, w_ref, o_ref, *, act):
    y = _dot(x_ref[...], w_ref[...])
    if act == "tanh":
        y = jnp.tanh(y)
    elif act == "sigmoid":
        y = _sigmoid(y)
    o_ref[...] = y.astype(o_ref.dtype)


def _mm(x, w, layer, act=None, tm=1024, tn=1024):
    t, k = x.shape
    n = w.shape[-1]
    tm, tn = min(tm, t), min(tn, n)
    return pl.pallas_call(
        functools.partial(_mm_body, act=act),
        grid=(t // tm, n // tn),
        in_specs=[
            pl.BlockSpec((tm, k), lambda i, j: (i, 0)),
            pl.BlockSpec((None, k, tn), lambda i, j: (layer, 0, j)),
        ],
        out_specs=pl.BlockSpec((tm, tn), lambda i, j: (i, j)),
        out_shape=jax.ShapeDtypeStruct((t, n), F32),
        compiler_params=_cparams(("parallel", "parallel")),
        name="mm",
    )(x, w)


def _mm_res_body(h_ref, x_ref, w_ref, o_ref):
    o_ref[...] = h_ref[...] + _dot(x_ref[...], w_ref[...])


def _mm_res(h, x, w, layer, tm=1024, tn=1024):
    t, k = x.shape
    n = w.shape[-1]
    tm = min(tm, t)
    return pl.pallas_call(
        _mm_res_body,
        grid=(t // tm, n // tn),
        in_specs=[
            pl.BlockSpec((tm, tn), lambda i, j: (i, j)),
            pl.BlockSpec((tm, k), lambda i, j: (i, 0)),
            pl.BlockSpec((None, k, tn), lambda i, j: (layer, 0, j)),
        ],
        out_specs=pl.BlockSpec((tm, tn), lambda i, j: (i, j)),
        out_shape=jax.ShapeDtypeStruct((t, n), F32),
        compiler_params=_cparams(("parallel", "parallel")),
        name="mm_res",
    )(h, x, w)


def _mm2_res_body(h_ref, xa_ref, xb_ref, wa_ref, wb_ref, o_ref):
    o_ref[...] = h_ref[...] + (_dot(xa_ref[...], wa_ref[...]) + _dot(xb_ref[...], wb_ref[...]))


def _mm2_res(h, xa, xb, w, layer, tm=1024, tn=1024):
    t, ka = xa.shape
    kb = xb.shape[1]
    n = w.shape[-1]
    assert ka == kb
    tm = min(tm, t)
    return pl.pallas_call(
        _mm2_res_body,
        grid=(t // tm, n // tn),
        in_specs=[
            pl.BlockSpec((tm, tn), lambda i, j: (i, j)),
            pl.BlockSpec((tm, ka), lambda i, j: (i, 0)),
            pl.BlockSpec((tm, kb), lambda i, j: (i, 0)),
            pl.BlockSpec((None, ka, tn), lambda i, j: (layer, 0, j)),
            pl.BlockSpec((None, kb, tn), lambda i, j: (layer, 1, j)),
        ],
        out_specs=pl.BlockSpec((tm, tn), lambda i, j: (i, j)),
        out_shape=jax.ShapeDtypeStruct((t, n), F32),
        compiler_params=_cparams(("parallel", "parallel")),
        name="mm2_res",
    )(h, xa, xb, w, w)


def _nmm_body(x_ref, g_ref, w_ref, o_ref, xn_ref):
    @pl.when(pl.program_id(1) == 0)
    def _():
        xn_ref[...] = _rms(x_ref[...], g_ref[...]).astype(BF16)

    o_ref[...] = _dot(xn_ref[...], w_ref[...])


def _nmm(h, g, w, layer, tm=1024, tn=1024):
    t, d = h.shape
    n = w.shape[-1]
    tm = min(tm, t)
    return pl.pallas_call(
        _nmm_body,
        grid=(t // tm, n // tn),
        in_specs=[
            pl.BlockSpec((tm, d), lambda i, j: (i, 0)),
            pl.BlockSpec((1, d), lambda i, j: (0, 0)),
            pl.BlockSpec((None, d, tn), lambda i, j: (layer, 0, j)),
        ],
        out_specs=pl.BlockSpec((tm, tn), lambda i, j: (i, j)),
        out_shape=jax.ShapeDtypeStruct((t, n), F32),
        scratch_shapes=[pltpu.VMEM((tm, d), BF16)],
        compiler_params=_cparams(("parallel", "arbitrary")),
        name="norm_mm",
    )(h, g, w)


def _ple_body(h_ref, g_ref, p_ref, wg_ref, wp_ref, fg_ref, o_ref, *, final):
    h = h_ref[...]
    xn = _rms(h, g_ref[...]).astype(BF16)
    gate = _sigmoid(_dot(xn, wg_ref[...]))
    pe = _dot(p_ref[...].astype(BF16), wp_ref[...])
    out = h + gate * pe
    if final:
        out = _rms(out, fg_ref[...])
    o_ref[...] = out


def _ple(h, g, p, wg, wp, fg, layer, final, tm=512):
    t, d = h.shape
    pd = p.shape[-1]
    return pl.pallas_call(
        functools.partial(_ple_body, final=final),
        grid=(t // tm,),
        in_specs=[
            pl.BlockSpec((tm, d), lambda i: (i, 0)),
            pl.BlockSpec((1, d), lambda i: (0, 0)),
            pl.BlockSpec((None, tm, pd), lambda i: (layer, i, 0)),
            pl.BlockSpec((None, d, d), lambda i: (layer, 0, 0)),
            pl.BlockSpec((None, pd, d), lambda i: (layer, 0, 0)),
            pl.BlockSpec((1, d), lambda i: (0, 0)),
        ],
        out_specs=pl.BlockSpec((tm, d), lambda i: (i, 0)),
        out_shape=jax.ShapeDtypeStruct((t, d), F32),
        compiler_params=_cparams(("parallel",)),
        name="ple",
    )(h, g, p, wg, wp, fg)


GMLP_CHUNKS = 2


def _gmlp_body(u_ref, v_ref, gain_ref, ws_ref, bs_ref, o_ref):
    groups = ws_ref.shape[0]
    tril, _ = _tri_masks(A_CHUNK)
    for g in range(groups):
        w = jnp.where(tril, ws_ref[g], 0.0).astype(BF16)
        bias = bs_ref[g]
        cols = slice(g * A_CHUNK, (g + 1) * A_CHUNK)
        for c in range(GMLP_CHUNKS):
            rows = slice(c * A_CHUNK, (c + 1) * A_CHUNK)
            u = _gelu_tanh(u_ref[rows, cols])
            v = _gelu_tanh(v_ref[rows, cols])
            vg = _rms(v, gain_ref[:, cols])
            s = _dot(w, vg.astype(BF16)) + bias
            o_ref[rows, cols] = (u * s).astype(o_ref.dtype)


def _gmlp(proj, gain, ws, bs, layer, a_width):
    t = proj.shape[0]
    groups = a_width // A_CHUNK
    tm = GMLP_CHUNKS * A_CHUNK
    return pl.pallas_call(
        _gmlp_body,
        grid=(t // tm,),
        in_specs=[
            pl.BlockSpec((tm, a_width), lambda c: (c, 0)),
            pl.BlockSpec((tm, a_width), lambda c: (c, 1)),
            pl.BlockSpec((None, 1, a_width), lambda c: (layer, 0, 0)),
            pl.BlockSpec((None, groups, A_CHUNK, A_CHUNK), lambda c: (layer, 0, 0, 0)),
            pl.BlockSpec((None, groups, A_CHUNK, 1), lambda c: (layer, 0, 0, 0)),
        ],
        out_specs=pl.BlockSpec((tm, a_width), lambda c: (c, 0)),
        out_shape=jax.ShapeDtypeStruct((t, a_width), BF16),
        compiler_params=_cparams(("parallel",)),
        name="gmlp",
    )(proj, proj, gain, ws, bs)


HG_CHUNK = 64
HG_SUB = 16


HG_HEADS = 4
P_HG = False


def _hgrn2_body(q_ref, f_ref, i_ref, g_ref, lbl_ref, on_ref, o_ref, st_ref, *, layer):
    c = pl.program_id(2)

    @pl.when(c == 0)
    def _():
        st_ref[...] = jnp.zeros_like(st_ref)

    logits = lbl_ref[...]
    e = jnp.exp(logits - jnp.max(logits, axis=0, keepdims=True))
    probs = e / jnp.sum(e, axis=0, keepdims=True)
    lb = jnp.zeros((1, logits.shape[1]), F32)
    for r in range(1, layer + 1):
        lb = lb + probs[r:r + 1, :]

    n = HG_CHUNK
    hs = range(HG_HEADS)
    cols = [slice(hd * B_HEAD, (hd + 1) * B_HEAD) for hd in hs]
    tril, _ = _tri_masks(n)
    tri = jnp.where(tril, 1.0, 0.0).astype(BF16)

    st = [st_ref[hd] for hd in hs]
    v = [i_ref[:, c] for c in cols]
    f = [lb[:, c] + (1.0 - lb[:, c]) * _sigmoid(f_ref[:, c]) for c in cols]
    kf = [1.0 - x for x in f]
    qf = [_silu(q_ref[:, c]) for c in cols]
    cum = [_cumsum_rows(tri, jnp.log(jnp.maximum(x, B_MIN_F))) for x in f]
    last = [x[n - 1:n, :] for x in cum]

    st_s = [_sp(x, P_HG) for x in st]
    v_s = [_sp(x, P_HG) for x in v]
    o = [_pd(_sp(qf[h] * jnp.exp(cum[h]), P_HG), st_s[h], _NT) for h in hs]
    for h in hs:
        kend = kf[h] * jnp.exp(last[h] - cum[h])
        st_ref[h] = st[h] * jnp.exp(last[h]) + _pd(v_s[h], _sp(kend, P_HG), _TN)

    nsub = n // HG_SUB
    trow = lax.broadcasted_iota(jnp.int32, (HG_SUB, 1), 0)
    rows = [[] for _ in hs]
    for bi in range(nsub):
        lo, hi = bi * HG_SUB, (bi + 1) * HG_SUB
        acc = [jnp.zeros((HG_SUB, B_HEAD), F32) for _ in hs]
        if bi > 0:
            ref = [cum[h][lo - 1:lo, :] for h in hs]
            qh = [_sp(qf[h][lo:hi] * jnp.exp(cum[h][lo:hi] - ref[h]), P_HG) for h in hs]
            kh = [_sp(kf[h][:lo] * jnp.exp(ref[h] - cum[h][:lo]), P_HG) for h in hs]
            att = [_sp(_pd(qh[h], kh[h], _NT), P_HG) for h in hs]
            acc = [_pd(att[h], tuple(part[:lo] for part in v_s[h])) for h in hs]
        for s in range(HG_SUB):
            for h in hs:
                c_b = cum[h][lo:hi]
                dec = jnp.exp(jnp.minimum(c_b - c_b[s:s + 1, :], 0.0))
                col = jnp.sum(qf[h][lo:hi] * dec * kf[h][lo + s:lo + s + 1, :], axis=-1, keepdims=True)
                col = jnp.where(trow >= s, col, 0.0)
                acc[h] = acc[h] + col * v[h][lo + s:lo + s + 1, :]
        for h in hs:
            rows[h].append(acc[h])

    for h in hs:
        out = o[h] + jnp.concatenate(rows[h], axis=0)
        o_ref[:, cols[h]] = (_rms(out, on_ref[:, cols[h]]) * _silu(g_ref[:, cols[h]])).astype(o_ref.dtype)


def _hgrn2(proj, lb_logits, onorm, layer, elayer, bsz, seq, a_width, b_width):
    t = proj.shape[0]
    heads = b_width // B_HEAD
    hgroups = heads // HG_HEADS
    nchunk = seq // HG_CHUNK
    width = HG_HEADS * B_HEAD
    off = 2 * a_width // width

    def col(which):
        return lambda b, h, c: (b * nchunk + c, off + which * hgroups + h)

    blk = (HG_CHUNK, width)
    depth = lb_logits.shape[0]
    return pl.pallas_call(
        functools.partial(_hgrn2_body, layer=layer),
        grid=(bsz, hgroups, nchunk),
        in_specs=[
            pl.BlockSpec(blk, col(0)),
            pl.BlockSpec(blk, col(1)),
            pl.BlockSpec(blk, col(2)),
            pl.BlockSpec(blk, col(3)),
            pl.BlockSpec((depth, width), lambda b, h, c: (0, h)),
            pl.BlockSpec((None, 1, width), lambda b, h, c: (elayer, 0, h)),
        ],
        out_specs=pl.BlockSpec(blk, lambda b, h, c: (b * nchunk + c, h)),
        out_shape=jax.ShapeDtypeStruct((t, b_width), BF16),
        scratch_shapes=[pltpu.VMEM((HG_HEADS, B_HEAD, B_HEAD), F32)],
        compiler_params=_cparams(("parallel", "parallel", "arbitrary")),
        name="hgrn2",
    )(proj, proj, proj, proj, lb_logits, onorm)


def _rwkv_mix_body(x_ref, xp_ref, g_ref, mix_ref, *o_refs, tm, seq):
    i = pl.program_id(0)
    g = g_ref[...]
    hn = _rms(x_ref[...], g)
    hp = _rms(xp_ref[...], g)[7:8, :]
    hp = jnp.where((i * tm) % seq == 0, 0.0, hp)
    row = lax.broadcasted_iota(jnp.int32, hn.shape, 0)
    prev = jnp.where(row == 0, hp, pltpu.roll(hn, 1, 0))
    xx = prev - hn
    for j, o_ref in enumerate(o_refs):
        o_ref[...] = (hn + xx * mix_ref[j:j + 1, :]).astype(BF16)


def _rwkv_mix(h, g, mix, layer, seq, tm=256):
    t, d = h.shape
    nmix = mix.shape[1]
    sub = 8
    return pl.pallas_call(
        functools.partial(_rwkv_mix_body, tm=tm, seq=seq),
        grid=(t // tm,),
        in_specs=[
            pl.BlockSpec((tm, d), lambda i: (i, 0)),
            pl.BlockSpec((sub, d), lambda i: (jnp.maximum(i * (tm // sub) - 1, 0), 0)),
            pl.BlockSpec((1, d), lambda i: (0, 0)),
            pl.BlockSpec((None, nmix, d), lambda i: (layer, 0, 0)),
        ],
        out_specs=[pl.BlockSpec((tm, d), lambda i: (i, 0)) for _ in range(nmix)],
        out_shape=[jax.ShapeDtypeStruct((t, d), BF16) for _ in range(nmix)],
        compiler_params=_cparams(("parallel",)),
        name="rwkv_mix",
    )(h, h, g, mix)


def _rwkv_prep_body(*refs, vres):
    if vres:
        (hw_ref, ha_ref, hg_ref, hv_ref, v_ref, vf_ref, w2_ref, a2_ref, g2_ref, v2_ref,
         w0_ref, a0_ref, v0_ref, lw_ref, a_ref, g_ref, vo_ref) = refs
    else:
        (hw_ref, ha_ref, hg_ref, w2_ref, a2_ref, g2_ref, w0_ref, a0_ref,
         lw_ref, a_ref, g_ref) = refs
    z = -(w0_ref[...] + _dot(hw_ref[...].astype(BF16), w2_ref[...]))
    softplus = jnp.maximum(z, 0.0) + jnp.log(1.0 + jnp.exp(-jnp.abs(z)))
    w = -softplus - 0.5
    lw_ref[...] = -jnp.exp(w)
    a_ref[...] = _sigmoid(a0_ref[...] + _dot(ha_ref[...].astype(BF16), a2_ref[...]))
    g_ref[...] = _dot(hg_ref[...].astype(BF16), g2_ref[...])
    if vres:
        v = v_ref[...]
        mv = _sigmoid(v0_ref[...] + _dot(hv_ref[...].astype(BF16), v2_ref[...]))
        vo_ref[...] = v + (vf_ref[...] - v) * mv


def _rwkv_prep(hw, ha, hg, w2, a2, g2, w0, a0, layer, vres=None, tm=256):
    t = hw.shape[0]
    d = w2.shape[-1]

    def tok(n):
        return pl.BlockSpec((tm, n), lambda i: (i, 0))

    def wgt(k):
        return pl.BlockSpec((None, k, d), lambda i: (layer, 0, 0))

    def vec(idx):
        return pl.BlockSpec((None, 1, d), lambda i: (idx, 0, 0))

    if vres is None:
        args = (hw, ha, hg, w2, a2, g2, w0, a0)
        in_specs = [tok(hw.shape[1]), tok(ha.shape[1]), tok(hg.shape[1]),
                    wgt(w2.shape[1]), wgt(a2.shape[1]), wgt(g2.shape[1]), vec(layer), vec(layer)]
        nout = 3
    else:
        hv, v, vf, v2, v0, vl = vres
        args = (hw, ha, hg, hv, v, vf, w2, a2, g2, v2, w0, a0, v0)
        in_specs = [tok(hw.shape[1]), tok(ha.shape[1]), tok(hg.shape[1]), tok(hv.shape[1]), tok(d), tok(d),
                    wgt(w2.shape[1]), wgt(a2.shape[1]), wgt(g2.shape[1]),
                    pl.BlockSpec((None, v2.shape[1], d), lambda i: (vl, 0, 0)),
                    vec(layer), vec(layer), vec(vl)]
        nout = 4
    return pl.pallas_call(
        functools.partial(_rwkv_prep_body, vres=vres is not None),
        grid=(t // tm,),
        in_specs=in_specs,
        out_specs=[tok(d) for _ in range(nout)],
        out_shape=[jax.ShapeDtypeStruct((t, d), F32) for _ in range(nout)],
        compiler_params=_cparams(("parallel",)),
        name="rwkv_prep",
    )(*args)


RW_CHUNK = 64


RW_PAIRS = 8
P_INV = False
P_ATT = False
P_APPLY = False
P_STATE = False


def _inv_unit_lower(mats, n):
    row = lax.broadcasted_iota(jnp.int32, (n, n), 0)
    col = lax.broadcasted_iota(jnp.int32, (n, n), 1)
    eye = jnp.where(row == col, 1.0, 0.0)
    ts = [eye + a for a in mats]
    ps = list(mats)
    k = 2
    while k < n:
        pss = [_sp(p, P_INV) for p in ps]
        ps = [_pd(s, s) for s in pss]
        ts = [t + _pd(_sp(t, P_INV), _sp(p, P_INV)) for t, p in zip(ts, ps)]
        k *= 2
    return ts


def _head_sum(x, m0):
    s0 = jnp.sum(jnp.where(m0, x, 0.0), axis=-1, keepdims=True)
    s1 = jnp.sum(jnp.where(m0, 0.0, x), axis=-1, keepdims=True)
    return jnp.where(m0, s0, s1)


def _rwkv_scan_body(r_ref, k_ref, v_ref, lw_ref, a_ref, g_ref, kk_ref, ka_ref, rk_ref, gg_ref, gb_ref,
                    o_ref, st_ref):
    c = pl.program_id(2)

    @pl.when(c == 0)
    def _():
        st_ref[...] = jnp.zeros_like(st_ref)

    n = RW_CHUNK
    pairs = range(RW_PAIRS)
    heads = [(p, hd) for p in pairs for hd in range(2)]
    cols = [slice(p * LANES, (p + 1) * LANES) for p in pairs]
    lane = lax.broadcasted_iota(jnp.int32, (1, LANES), 1)
    m0 = lane < C_HEAD
    m1 = jnp.logical_not(m0)
    tril, stril = _tri_masks(n)
    tri = jnp.where(tril, 1.0, 0.0).astype(BF16)

    st = [st_ref[p] for p in pairs]
    r = [r_ref[:, c] for c in cols]
    k = [k_ref[:, c] for c in cols]
    v = [v_ref[:, c] for c in cols]
    asig = [a_ref[:, c] for c in cols]

    cum = [_cumsum_rows(tri, lw_ref[:, c]) for c in cols]
    last = [x[n - 1:n, :] for x in cum]
    kkr = [k[p] * kk_ref[:, cols[p]] for p in pairs]
    kk = [x / jnp.maximum(jnp.sqrt(_head_sum(x * x, m0)), 1e-12) for x in kkr]
    kmod = [k[p] * (1.0 + (asig[p] - 1.0) * ka_ref[:, cols[p]]) for p in pairs]
    b = [kk[p] * asig[p] for p in pairs]
    rt = [r[p] * jnp.exp(cum[p]) for p in pairs]
    at = [-kk[p] * jnp.exp(cum[p] - lw_ref[:, cols[p]]) for p in pairs]
    einv = [jnp.exp(-x) for x in cum]
    bt = [_sp(b[p] * einv[p], P_ATT) for p in pairs]
    kt = [_sp(kmod[p] * einv[p], P_ATT) for p in pairs]
    eend = [jnp.exp(last[p] - cum[p]) for p in pairs]
    v_a = [_sp(x, P_APPLY) for x in v]
    st_s = [_sp(x, P_STATE) for x in st]

    at_h = [_sp(jnp.where(m0 if hd == 0 else m1, at[p], 0.0), P_ATT) for p, hd in heads]
    rt_h = [_sp(jnp.where(m0 if hd == 0 else m1, rt[p], 0.0), P_ATT) for p, hd in heads]
    a_ab = [jnp.where(stril, _pd(at_h[i], bt[p], _NT), 0.0) for i, (p, hd) in enumerate(heads)]
    t_inv = [_sp(x, P_APPLY) for x in _inv_unit_lower(a_ab, n)]
    a_ak = [_sp(jnp.where(stril, _pd(at_h[i], kt[p], _NT), 0.0), P_APPLY) for i, (p, hd) in enumerate(heads)]
    a_rb = [_sp(jnp.where(tril, _pd(rt_h[i], bt[p], _NT), 0.0), P_APPLY) for i, (p, hd) in enumerate(heads)]
    a_rk = [_sp(jnp.where(tril, _pd(rt_h[i], kt[p], _NT), 0.0), P_APPLY) for i, (p, hd) in enumerate(heads)]
    av = [_pd(a_ak[i], v_a[p]) for i, (p, hd) in enumerate(heads)]
    y_v = [_pd(a_rk[i], v_a[p]) for i, (p, hd) in enumerate(heads)]

    x = [_pd(_sp(at[p], P_STATE), st_s[p], _NT) + jnp.where(m0, av[2 * p], av[2 * p + 1]) for p in pairs]
    x_s = [_sp(t, P_APPLY) for t in x]
    u = [jnp.where(m0, _pd(t_inv[2 * p], x_s[p]), _pd(t_inv[2 * p + 1], x_s[p])) for p in pairs]
    u_a = [_sp(t, P_APPLY) for t in u]
    y = [_pd(_sp(rt[p], P_STATE), st_s[p], _NT)
         + jnp.where(m0, _pd(a_rb[2 * p], u_a[p]) + y_v[2 * p], _pd(a_rb[2 * p + 1], u_a[p]) + y_v[2 * p + 1])
         for p in pairs]

    row = lax.broadcasted_iota(jnp.int32, (LANES, LANES), 0)
    colm = lax.broadcasted_iota(jnp.int32, (LANES, LANES), 1)
    bdiag = (row < C_HEAD) == (colm < C_HEAD)
    for p in pairs:
        upd = (_pd(_sp(u[p], P_STATE), _sp(b[p] * eend[p], P_STATE), _TN)
               + _pd(_sp(v[p], P_STATE), _sp(kmod[p] * eend[p], P_STATE), _TN))
        st_ref[p] = st[p] * jnp.exp(last[p]) + jnp.where(bdiag, upd, 0.0)

    inv_n = 1.0 / C_HEAD
    for p in pairs:
        mu = _head_sum(y[p], m0) * inv_n
        dy = y[p] - mu
        var = _head_sum(dy * dy, m0) * inv_n
        yn = dy * lax.rsqrt(var + C_GN_EPS) * gg_ref[:, cols[p]] + gb_ref[:, cols[p]]
        bonus = _head_sum(r[p] * kmod[p] * rk_ref[:, cols[p]], m0) * v[p]
        o_ref[:, cols[p]] = ((yn + bonus) * g_ref[:, cols[p]]).astype(o_ref.dtype)


def _rwkv_scan(r, k, v, lw, a, g, kk, ka, rk, gg, gb, layer, bsz, seq):
    t, d = r.shape
    nchunk = seq // RW_CHUNK
    width = RW_PAIRS * LANES
    blk = pl.BlockSpec((RW_CHUNK, width), lambda b, p, c: (b * nchunk + c, p))
    vec = pl.BlockSpec((None, 1, width), lambda b, p, c: (layer, 0, p))
    return pl.pallas_call(
        _rwkv_scan_body,
        grid=(bsz, d // width, nchunk),
        in_specs=[blk] * 6 + [vec] * 5,
        out_specs=blk,
        out_shape=jax.ShapeDtypeStruct((t, d), BF16),
        scratch_shapes=[pltpu.VMEM((RW_PAIRS, LANES, LANES), F32)],
        compiler_params=_cparams(("parallel", "parallel", "arbitrary")),
        name="rwkv_scan",
    )(r, k, v, lw, a, g, kk, ka, rk, gg, gb)


def _pad_cols(w, n):
    return jnp.pad(w, ((0, 0), (0, 0), (0, n - w.shape[-1])))


def _pad_rows(w, n):
    return jnp.pad(w, ((0, 0), (0, n - w.shape[1]), (0, 0)))


def kernel(x, p, norms, final_norm, ffn_wg, ffn_wu, ffn_wd, ple_wp, ple_wg, e_w_in, e_w_out, a_vnorm, a_ws, a_bs, b_onorm, b_lb_logits, c_mix, c_wr, c_wk, c_wv, c_wo, c_w0, c_w1, c_w2, c_a0, c_a1, c_a2, c_g1, c_g2, c_kk, c_ka, c_rk, c_gn_g, c_gn_b, c_v0, c_v1, c_v2):
    bsz, seq, d = x.shape
    depth = p.shape[0]
    t = bsz * seq
    a_width = a_vnorm.shape[-1]
    b_width = b_onorm.shape[-1]

    bf = lambda w: w.astype(BF16)
    wg_b, wu_b, wd_b = ffn_wg, ffn_wu, ffn_wd
    ple_wp_b, ple_wg_b = bf(ple_wp), bf(ple_wg)
    e_in_b = bf(e_w_in)
    e_out_b = bf(e_w_out)
    wr_b, wk_b, wv_b, wo_b = bf(c_wr), bf(c_wk), bf(c_wv), bf(c_wo)
    w1_b = bf(_pad_cols(c_w1, LORA_PAD))
    w2_b = bf(_pad_rows(c_w2, LORA_PAD))
    a1_b = bf(_pad_cols(c_a1, LORA_PAD))
    a2_b = bf(_pad_rows(c_a2, LORA_PAD))
    g1_b, g2_b = bf(c_g1), bf(c_g2)
    v1_b = bf(_pad_cols(c_v1, LORA_PAD))
    v2_b = bf(_pad_rows(c_v2, LORA_PAD))

    vec3 = lambda w: w.reshape(w.shape[0], 1, -1)
    a_vnorm3, b_onorm3 = vec3(a_vnorm), vec3(b_onorm)
    a_bs4 = a_bs.reshape(a_bs.shape + (1,))
    w0_3, a0_3, v0_3 = vec3(c_w0), vec3(c_a0), vec3(c_v0)
    kk3, ka3, rk3, gg3, gb3 = vec3(c_kk), vec3(c_ka), vec3(c_rk), vec3(c_gn_g), vec3(c_gn_b)
    fg = final_norm.reshape(1, d)

    h = x.reshape(t, d)
    p2 = p.reshape(depth, t, p.shape[-1])
    v_first = None
    for i in range(depth):
        j = i // 2
        h = _ffn(h, norms[i, 0].reshape(1, d), wg_b, wu_b, wd_b, i, 0)
        g1n = norms[i, 1].reshape(1, d)
        if i % 2 == 0:
            proj = _nmm(h, g1n, e_in_b, j)
            a_out = _gmlp(proj, a_vnorm3, a_ws, a_bs4, j, a_width)
            b_out = _hgrn2(proj, b_lb_logits, b_onorm3, i, j, bsz, seq, a_width, b_width)
            h = _mm2_res(h, a_out, b_out, e_out_b, j)
        else:
            xr, xw, xk, xv, xa, xg = _rwkv_mix(h, g1n, c_mix, j, seq)
            r = _mm(xr, wr_b, j)
            k = _mm(xk, wk_b, j)
            v = _mm(xv, wv_b, j)
            hw = _mm(xw, w1_b, j, act="tanh")
            ha = _mm(xa, a1_b, j)
            hg = _mm(xg, g1_b, j, act="sigmoid")
            if j == 0:
                lw, a, g = _rwkv_prep(hw, ha, hg, w2_b, a2_b, g2_b, w0_3, a0_3, j)
                v_first = v
            else:
                hv = _mm(xv, v1_b, j - 1)
                lw, a, g, v = _rwkv_prep(hw, ha, hg, w2_b, a2_b, g2_b, w0_3, a0_3, j,
                                         vres=(hv, v, v_first, v2_b, v0_3, j - 1))
            y = _rwkv_scan(r, k, v, lw, a, g, kk3, ka3, rk3, gg3, gb3, j, bsz, seq)
            h = _mm_res(h, y, wo_b, j)
        h = _ffn(h, norms[i, 2].reshape(1, d), wg_b, wu_b, wd_b, i, 1)
        h = _ple(h, norms[i, 3].reshape(1, d), p2, ple_wg_b, ple_wp_b, fg, i, final=(i == depth - 1))
    return h.reshape(bsz, seq, d)
```

```python
import functools

import jax
import jax.numpy as jnp
from jax import lax
from jax.experimental import pallas as pl
from jax.experimental.pallas import tpu as pltpu

F32 = jnp.float32
BF16 = jnp.bfloat16

LANES = 128
RMS_EPS = 1e-6
A_CHUNK = 128
B_HEAD = 128
B_MIN_F = 1e-30
C_HEAD = 64
C_GN_EPS = 64e-5
LORA_PAD = 128

VMEM_LIMIT = 56 * 1024 * 1024

_NN = ((1,), (0,))
_NT = ((1,), (1,))
_TN = ((0,), (0,))


def _cparams(sem):
    return pltpu.CompilerParams(dimension_semantics=sem, vmem_limit_bytes=VMEM_LIMIT)


def _rms(x, g, eps=RMS_EPS):
    return x * lax.rsqrt(jnp.mean(x * x, axis=-1, keepdims=True) + eps) * g


def _sigmoid(x):
    return 1.0 / (1.0 + jnp.exp(-x))


def _silu(x):
    return x * _sigmoid(x)


def _gelu_tanh(x):
    return 0.5 * x * (1.0 + jnp.tanh(0.7978845608028654 * (x + 0.044715 * (x * x * x))))


def _dot(a, b, dims=_NN):
    return lax.dot_general(a, b, (dims, ((), ())), preferred_element_type=F32)


def _sp(x, lo):
    hi = x.astype(BF16)
    if not lo:
        return (hi,)
    return (hi, (x - hi.astype(F32)).astype(BF16))


def _pd(a, b, dims=_NN):
    out = _dot(a[0], b[0], dims)
    extra = None
    if len(b) > 1:
        extra = _dot(a[0], b[1], dims)
    if len(a) > 1:
        t = _dot(a[1], b[0], dims)
        extra = t if extra is None else extra + t
    return out if extra is None else out + extra


def _cumsum_rows(tri_bf16, x):
    hi = x.astype(BF16)
    r1 = x - hi.astype(F32)
    mid = r1.astype(BF16)
    lo = (r1 - mid.astype(F32)).astype(BF16)
    return _dot(tri_bf16, hi) + (_dot(tri_bf16, mid) + _dot(tri_bf16, lo))


def _tri_masks(n):
    row = lax.broadcasted_iota(jnp.int32, (n, n), 0)
    col = lax.broadcasted_iota(jnp.int32, (n, n), 1)
    return col <= row, col < row


FFN_DOWN_COLS = 512


def _ffn_body(x_ref, g_ref, wg_ref, wu_ref, wd_ref, o_ref, xn_ref):
    j = pl.program_id(1)

    @pl.when(j == 0)
    def _():
        x = x_ref[...]
        xn_ref[...] = _rms(x, g_ref[...]).astype(BF16)
        o_ref[...] = x

    xn = xn_ref[...]
    gate = _dot(xn, wg_ref[...].astype(BF16))
    up = _dot(xn, wu_ref[...].astype(BF16))
    hid = (0.5 * _silu(gate) * up).astype(BF16)
    for c in range(0, o_ref.shape[1], FFN_DOWN_COLS):
        cols = slice(c, c + FFN_DOWN_COLS)
        o_ref[:, cols] += _dot(hid, wd_ref[:, cols].astype(BF16))


def _ffn(h, g, wg, wu, wd, layer, half, tm=1024, tf=256):
    t, d = h.shape
    f = wg.shape[-1]
    tm = min(tm, t)
    return pl.pallas_call(
        _ffn_body,
        grid=(t // tm, f // tf),
        in_specs=[
            pl.BlockSpec((tm, d), lambda i, j: (i, 0)),
            pl.BlockSpec((1, d), lambda i, j: (0, 0)),
            pl.BlockSpec((None, None, d, tf), lambda i, j: (layer, half, 0, j)),
            pl.BlockSpec((None, None, d, tf), lambda i, j: (layer, half, 0, j)),
            pl.BlockSpec((None, None, tf, d), lambda i, j: (layer, half, j, 0)),
        ],
        out_specs=pl.BlockSpec((tm, d), lambda i, j: (i, 0)),
        out_shape=jax.ShapeDtypeStruct((t, d), F32),
        scratch_shapes=[pltpu.VMEM((tm, d), BF16)],
        compiler_params=_cparams(("parallel", "arbitrary")),
        name="ffn",
    )(h, g, wg, wu, wd)


def _mm_body(x_ref, w_ref, o_ref, *, act):
    y = _dot(x_ref[...], w_ref[...].astype(BF16))
    if act == "tanh":
        y = jnp.tanh(y)
    elif act == "sigmoid":
        y = _sigmoid(y)
    o_ref[...] = y.astype(o_ref.dtype)


def _mm(x, w, layer, act=None, tm=1024, tn=1024):
    t, k = x.shape
    n = w.shape[-1]
    tm, tn = min(tm, t), min(tn, n)
    return pl.pallas_call(
        functools.partial(_mm_body, act=act),
        grid=(t // tm, n // tn),
        in_specs=[
            pl.BlockSpec((tm, k), lambda i, j: (i, 0)),
            pl.BlockSpec((None, k, tn), lambda i, j: (layer, 0, j)),
        ],
        out_specs=pl.BlockSpec((tm, tn), lambda i, j: (i, j)),
        out_shape=jax.ShapeDtypeStruct((t, n), F32),
        compiler_params=_cparams(("parallel", "parallel")),
        name="mm",
    )(x, w)


def _mm_res_body(h_ref, x_ref, w_ref, o_ref):
    o_ref[...] = h_ref[...] + _dot(x_ref[...], w_ref[...].astype(BF16))


def _mm_res(h, x, w, layer, tm=1024, tn=1024):
    t, k = x.shape
    n = w.shape[-1]
    tm = min(tm, t)
    return pl.pallas_call(
        _mm_res_body,
        grid=(t // tm, n // tn),
        in_specs=[
            pl.BlockSpec((tm, tn), lambda i, j: (i, j)),
            pl.BlockSpec((tm, k), lambda i, j: (i, 0)),
            pl.BlockSpec((None, k, tn), lambda i, j: (layer, 0, j)),
        ],
        out_specs=pl.BlockSpec((tm, tn), lambda i, j: (i, j)),
        out_shape=jax.ShapeDtypeStruct((t, n), F32),
        compiler_params=_cparams(("parallel", "parallel")),
        name="mm_res",
    )(h, x, w)


def _mm2_res_body(h_ref, xa_ref, xb_ref, wa_ref, wb_ref, o_ref):
    o_ref[...] = h_ref[...] + (_dot(xa_ref[...], wa_ref[...].astype(BF16))
                               + _dot(xb_ref[...], wb_ref[...].astype(BF16)))


def _mm2_res(h, xa, xb, w, layer, tm=1024, tn=1024):
    t, ka = xa.shape
    kb = xb.shape[1]
    n = w.shape[-1]
    assert ka == kb
    tm = min(tm, t)
    return pl.pallas_call(
        _mm2_res_body,
        grid=(t // tm, n // tn),
        in_specs=[
            pl.BlockSpec((tm, tn), lambda i, j: (i, j)),
            pl.BlockSpec((tm, ka), lambda i, j: (i, 0)),
            pl.BlockSpec((tm, kb), lambda i, j: (i, 0)),
            pl.BlockSpec((None, ka, tn), lambda i, j: (layer, 0, j)),
            pl.BlockSpec((None, kb, tn), lambda i, j: (layer, 1, j)),
        ],
        out_specs=pl.BlockSpec((tm, tn), lambda i, j: (i, j)),
        out_shape=jax.ShapeDtypeStruct((t, n), F32),
        compiler_params=_cparams(("parallel", "parallel")),
        name="mm2_res",
    )(h, xa, xb, w, w)


def _nmm_body(x_ref, g_ref, w_ref, o_ref, xn_ref):
    @pl.when(pl.program_id(1) == 0)
    def _():
        xn_ref[...] = _rms(x_ref[...], g_ref[...]).astype(BF16)

    o_ref[...] = _dot(xn_ref[...], w_ref[...].astype(BF16))


def _nmm(h, g, w, layer, tm=1024, tn=1024):
    t, d = h.shape
    n = w.shape[-1]
    tm = min(tm, t)
    return pl.pallas_call(
        _nmm_body,
        grid=(t // tm, n // tn),
        in_specs=[
            pl.BlockSpec((tm, d), lambda i, j: (i, 0)),
            pl.BlockSpec((1, d), lambda i, j: (0, 0)),
            pl.BlockSpec((None, d, tn), lambda i, j: (layer, 0, j)),
        ],
        out_specs=pl.BlockSpec((tm, tn), lambda i, j: (i, j)),
        out_shape=jax.ShapeDtypeStruct((t, n), F32),
        scratch_shapes=[pltpu.VMEM((tm, d), BF16)],
        compiler_params=_cparams(("parallel", "arbitrary")),
        name="norm_mm",
    )(h, g, w)


def _ple_body(h_ref, g_ref, p_ref, wg_ref, wp_ref, fg_ref, o_ref, *, final):
    h = h_ref[...]
    xn = _rms(h, g_ref[...]).astype(BF16)
    gate = _sigmoid(_dot(xn, wg_ref[...]))
    pe = _dot(p_ref[...].astype(BF16), wp_ref[...])
    out = h + gate * pe
    if final:
        out = _rms(out, fg_ref[...])
    o_ref[...] = out


def _ple(h, g, p, wg, wp, fg, layer, final, tm=512):
    t, d = h.shape
    pd = p.shape[-1]
    return pl.pallas_call(
        functools.partial(_ple_body, final=final),
        grid=(t // tm,),
        in_specs=[
            pl.BlockSpec((tm, d), lambda i: (i, 0)),
            pl.BlockSpec((1, d), lambda i: (0, 0)),
            pl.BlockSpec((None, tm, pd), lambda i: (layer, i, 0)),
            pl.BlockSpec((None, d, d), lambda i: (layer, 0, 0)),
            pl.BlockSpec((None, pd, d), lambda i: (layer, 0, 0)),
            pl.BlockSpec((1, d), lambda i: (0, 0)),
        ],
        out_specs=pl.BlockSpec((tm, d), lambda i: (i, 0)),
        out_shape=jax.ShapeDtypeStruct((t, d), F32),
        compiler_params=_cparams(("parallel",)),
        name="ple",
    )(h, g, p, wg, wp, fg)


GMLP_CHUNKS = 2


def _gmlp_body(u_ref, v_ref, gain_ref, ws_ref, bs_ref, o_ref):
    groups = ws_ref.shape[0]
    tril, _ = _tri_masks(A_CHUNK)
    for g in range(groups):
        w = jnp.where(tril, ws_ref[g], 0.0).astype(BF16)
        bias = bs_ref[g]
        cols = slice(g * A_CHUNK, (g + 1) * A_CHUNK)
        for c in range(GMLP_CHUNKS):
            rows = slice(c * A_CHUNK, (c + 1) * A_CHUNK)
            u = _gelu_tanh(u_ref[rows, cols])
            v = _gelu_tanh(v_ref[rows, cols])
            vg = _rms(v, gain_ref[:, cols])
            s = _dot(w, vg.astype(BF16)) + bias
            o_ref[rows, cols] = (u * s).astype(o_ref.dtype)


def _gmlp(proj, gain, ws, bs, layer, a_width):
    t = proj.shape[0]
    groups = a_width // A_CHUNK
    tm = GMLP_CHUNKS * A_CHUNK
    return pl.pallas_call(
        _gmlp_body,
        grid=(t // tm,),
        in_specs=[
            pl.BlockSpec((tm, a_width), lambda c: (c, 0)),
            pl.BlockSpec((tm, a_width), lambda c: (c, 1)),
            pl.BlockSpec((None, 1, a_width), lambda c: (layer, 0, 0)),
            pl.BlockSpec((None, groups, A_CHUNK, A_CHUNK), lambda c: (layer, 0, 0, 0)),
            pl.BlockSpec((None, groups, A_CHUNK, 1), lambda c: (layer, 0, 0, 0)),
        ],
        out_specs=pl.BlockSpec((tm, a_width), lambda c: (c, 0)),
        out_shape=jax.ShapeDtypeStruct((t, a_width), BF16),
        compiler_params=_cparams(("parallel",)),
        name="gmlp",
    )(proj, proj, gain, ws, bs)


HG_CHUNK = 64
HG_SUB = 16


HG_HEADS = 4
P_HG = False


def _hgrn2_body(q_ref, f_ref, i_ref, g_ref, lbl_ref, on_ref, o_ref, st_ref, *, layer):
    c = pl.program_id(2)

    @pl.when(c == 0)
    def _():
        st_ref[...] = jnp.zeros_like(st_ref)

    logits = lbl_ref[...]
    e = jnp.exp(logits - jnp.max(logits, axis=0, keepdims=True))
    probs = e / jnp.sum(e, axis=0, keepdims=True)
    lb = jnp.zeros((1, logits.shape[1]), F32)
    for r in range(1, layer + 1):
        lb = lb + probs[r:r + 1, :]

    n = HG_CHUNK
    hs = range(HG_HEADS)
    cols = [slice(hd * B_HEAD, (hd + 1) * B_HEAD) for hd in hs]
    tril, _ = _tri_masks(n)
    tri = jnp.where(tril, 1.0, 0.0).astype(BF16)

    st = [st_ref[hd] for hd in hs]
    v = [i_ref[:, c] for c in cols]
    f = [lb[:, c] + (1.0 - lb[:, c]) * _sigmoid(f_ref[:, c]) for c in cols]
    kf = [1.0 - x for x in f]
    qf = [_silu(q_ref[:, c]) for c in cols]
    cum = [_cumsum_rows(tri, jnp.log(jnp.maximum(x, B_MIN_F))) for x in f]
    last = [x[n - 1:n, :] for x in cum]

    st_s = [_sp(x, P_HG) for x in st]
    v_s = [_sp(x, P_HG) for x in v]
    o = [_pd(_sp(qf[h] * jnp.exp(cum[h]), P_HG), st_s[h], _NT) for h in hs]
    for h in hs:
        kend = kf[h] * jnp.exp(last[h] - cum[h])
        st_ref[h] = st[h] * jnp.exp(last[h]) + _pd(v_s[h], _sp(kend, P_HG), _TN)

    nsub = n // HG_SUB
    trow = lax.broadcasted_iota(jnp.int32, (HG_SUB, 1), 0)
    rows = [[] for _ in hs]
    for bi in range(nsub):
        lo, hi = bi * HG_SUB, (bi + 1) * HG_SUB
        acc = [jnp.zeros((HG_SUB, B_HEAD), F32) for _ in hs]
        if bi > 0:
            ref = [cum[h][lo - 1:lo, :] for h in hs]
            qh = [_sp(qf[h][lo:hi] * jnp.exp(cum[h][lo:hi] - ref[h]), P_HG) for h in hs]
            kh = [_sp(kf[h][:lo] * jnp.exp(ref[h] - cum[h][:lo]), P_HG) for h in hs]
            att = [_sp(_pd(qh[h], kh[h], _NT), P_HG) for h in hs]
            acc = [_pd(att[h], tuple(part[:lo] for part in v_s[h])) for h in hs]
        for s in range(HG_SUB):
            for h in hs:
                c_b = cum[h][lo:hi]
                dec = jnp.exp(jnp.minimum(c_b - c_b[s:s + 1, :], 0.0))
                col = jnp.sum(qf[h][lo:hi] * dec * kf[h][lo + s:lo + s + 1, :], axis=-1, keepdims=True)
                col = jnp.where(trow >= s, col, 0.0)
                acc[h] = acc[h] + col * v[h][lo + s:lo + s + 1, :]
        for h in hs:
            rows[h].append(acc[h])

    for h in hs:
        out = o[h] + jnp.concatenate(rows[h], axis=0)
        o_ref[:, cols[h]] = (_rms(out, on_ref[:, cols[h]]) * _silu(g_ref[:, cols[h]])).astype(o_ref.dtype)


def _hgrn2(proj, lb_logits, onorm, layer, elayer, bsz, seq, a_width, b_width):
    t = proj.shape[0]
    heads = b_width // B_HEAD
    hgroups = heads // HG_HEADS
    nchunk = seq // HG_CHUNK
    width = HG_HEADS * B_HEAD
    off = 2 * a_width // width

    def col(which):
        return lambda b, h, c: (b * nchunk + c, off + which * hgroups + h)

    blk = (HG_CHUNK, width)
    depth = lb_logits.shape[0]
    return pl.pallas_call(
        functools.partial(_hgrn2_body, layer=layer),
        grid=(bsz, hgroups, nchunk),
        in_specs=[
            pl.BlockSpec(blk, col(0)),
            pl.BlockSpec(blk, col(1)),
            pl.BlockSpec(blk, col(2)),
            pl.BlockSpec(blk, col(3)),
            pl.BlockSpec((depth, width), lambda b, h, c: (0, h)),
            pl.BlockSpec((None, 1, width), lambda b, h, c: (elayer, 0, h)),
        ],
        out_specs=pl.BlockSpec(blk, lambda b, h, c: (b * nchunk + c, h)),
        out_shape=jax.ShapeDtypeStruct((t, b_width), BF16),
        scratch_shapes=[pltpu.VMEM((HG_HEADS, B_HEAD, B_HEAD), F32)],
        compiler_params=_cparams(("parallel", "parallel", "arbitrary")),
        name="hgrn2",
    )(proj, proj, proj, proj, lb_logits, onorm)


def _rwkv_mix_body(x_ref, xp_ref, g_ref, mix_ref, *o_refs, tm, seq):
    i = pl.program_id(0)
    g = g_ref[...]
    hn = _rms(x_ref[...], g)
    hp = _rms(xp_ref[...], g)[7:8, :]
    hp = jnp.where((i * tm) % seq == 0, 0.0, hp)
    row = lax.broadcasted_iota(jnp.int32, hn.shape, 0)
    prev = jnp.where(row == 0, hp, pltpu.roll(hn, 1, 0))
    xx = prev - hn
    for j, o_ref in enumerate(o_refs):
        o_ref[...] = (hn + xx * mix_ref[j:j + 1, :]).astype(BF16)


def _rwkv_mix(h, g, mix, layer, seq, tm=256):
    t, d = h.shape
    nmix = mix.shape[1]
    sub = 8
    return pl.pallas_call(
        functools.partial(_rwkv_mix_body, tm=tm, seq=seq),
        grid=(t // tm,),
        in_specs=[
            pl.BlockSpec((tm, d), lambda i: (i, 0)),
            pl.BlockSpec((sub, d), lambda i: (jnp.maximum(i * (tm // sub) - 1, 0), 0)),
            pl.BlockSpec((1, d), lambda i: (0, 0)),
            pl.BlockSpec((None, nmix, d), lambda i: (layer, 0, 0)),
        ],
        out_specs=[pl.BlockSpec((tm, d), lambda i: (i, 0)) for _ in range(nmix)],
        out_shape=[jax.ShapeDtypeStruct((t, d), BF16) for _ in range(nmix)],
        compiler_params=_cparams(("parallel",)),
        name="rwkv_mix",
    )(h, h, g, mix)


def _rwkv_prep_body(*refs, vres):
    if vres:
        (hw_ref, ha_ref, hg_ref, hv_ref, v_ref, vf_ref, w2_ref, a2_ref, g2_ref, v2_ref,
         w0_ref, a0_ref, v0_ref, lw_ref, a_ref, g_ref, vo_ref) = refs
    else:
        (hw_ref, ha_ref, hg_ref, w2_ref, a2_ref, g2_ref, w0_ref, a0_ref,
         lw_ref, a_ref, g_ref) = refs
    z = -(w0_ref[...] + _dot(hw_ref[...].astype(BF16), w2_ref[...]))
    softplus = jnp.maximum(z, 0.0) + jnp.log(1.0 + jnp.exp(-jnp.abs(z)))
    w = -softplus - 0.5
    lw_ref[...] = -jnp.exp(w)
    a_ref[...] = _sigmoid(a0_ref[...] + _dot(ha_ref[...].astype(BF16), a2_ref[...]))
    g_ref[...] = _dot(hg_ref[...].astype(BF16), g2_ref[...])
    if vres:
        v = v_ref[...]
        mv = _sigmoid(v0_ref[...] + _dot(hv_ref[...].astype(BF16), v2_ref[...]))
        vo_ref[...] = v + (vf_ref[...] - v) * mv


def _rwkv_prep(hw, ha, hg, w2, a2, g2, w0, a0, layer, vres=None, tm=256):
    t = hw.shape[0]
    d = w2.shape[-1]

    def tok(n):
        return pl.BlockSpec((tm, n), lambda i: (i, 0))

    def wgt(k):
        return pl.BlockSpec((None, k, d), lambda i: (layer, 0, 0))

    def vec(idx):
        return pl.BlockSpec((None, 1, d), lambda i: (idx, 0, 0))

    if vres is None:
        args = (hw, ha, hg, w2, a2, g2, w0, a0)
        in_specs = [tok(hw.shape[1]), tok(ha.shape[1]), tok(hg.shape[1]),
                    wgt(w2.shape[1]), wgt(a2.shape[1]), wgt(g2.shape[1]), vec(layer), vec(layer)]
        nout = 3
    else:
        hv, v, vf, v2, v0, vl = vres
        args = (hw, ha, hg, hv, v, vf, w2, a2, g2, v2, w0, a0, v0)
        in_specs = [tok(hw.shape[1]), tok(ha.shape[1]), tok(hg.shape[1]), tok(hv.shape[1]), tok(d), tok(d),
                    wgt(w2.shape[1]), wgt(a2.shape[1]), wgt(g2.shape[1]),
                    pl.BlockSpec((None, v2.shape[1], d), lambda i: (vl, 0, 0)),
                    vec(layer), vec(layer), vec(vl)]
        nout = 4
    return pl.pallas_call(
        functools.partial(_rwkv_prep_body, vres=vres is not None),
        grid=(t // tm,),
        in_specs=in_specs,
        out_specs=[tok(d) for _ in range(nout)],
        out_shape=[jax.ShapeDtypeStruct((t, d), F32) for _ in range(nout)],
        compiler_params=_cparams(("parallel",)),
        name="rwkv_prep",
    )(*args)


RW_CHUNK = 64


RW_PAIRS = 8


def _head_sum(x, m0):
    s0 = jnp.sum(jnp.where(m0, x, 0.0), axis=-1, keepdims=True)
    s1 = jnp.sum(jnp.where(m0, 0.0, x), axis=-1, keepdims=True)
    return jnp.where(m0, s0, s1)


def _rwkv_scan_body(r_ref, k_ref, v_ref, lw_ref, a_ref, g_ref, kk_ref, ka_ref, rk_ref, gg_ref, gb_ref,
                    o_ref, st_ref):
    c = pl.program_id(2)

    @pl.when(c == 0)
    def _():
        st_ref[...] = jnp.zeros_like(st_ref)

    n = RW_CHUNK
    pairs = range(RW_PAIRS)
    heads = [(p, hd) for p in pairs for hd in range(2)]
    cols = [slice(p * LANES, (p + 1) * LANES) for p in pairs]
    lane = lax.broadcasted_iota(jnp.int32, (1, LANES), 1)
    m0 = lane < C_HEAD
    m1 = jnp.logical_not(m0)
    tril, stril = _tri_masks(n)
    tri = jnp.where(tril, 1.0, 0.0).astype(BF16)
    eye = jnp.where(tril & jnp.logical_not(stril), 1.0, 0.0)

    def hsel(x, hd):
        return jnp.where(m0 if hd == 0 else m1, x, jnp.zeros_like(x))

    st = [st_ref[p] for p in pairs]
    r = [r_ref[:, c] for c in cols]
    k = [k_ref[:, c] for c in cols]
    v = [v_ref[:, c] for c in cols]
    asig = [a_ref[:, c] for c in cols]

    cum = [_cumsum_rows(tri, lw_ref[:, c]) for c in cols]
    last = [x[n - 1:n, :] for x in cum]
    kkr = [k[p] * kk_ref[:, cols[p]] for p in pairs]
    kk = [x / jnp.maximum(jnp.sqrt(_head_sum(x * x, m0)), 1e-12) for x in kkr]
    kmod = [k[p] * (1.0 + (asig[p] - 1.0) * ka_ref[:, cols[p]]) for p in pairs]
    b = [kk[p] * asig[p] for p in pairs]
    rt = [(r[p] * jnp.exp(cum[p])).astype(BF16) for p in pairs]
    at = [(-kk[p] * jnp.exp(cum[p] - lw_ref[:, cols[p]])).astype(BF16) for p in pairs]
    einv = [jnp.exp(-x) for x in cum]
    bt = [(b[p] * einv[p]).astype(BF16) for p in pairs]
    kt = [(kmod[p] * einv[p]).astype(BF16) for p in pairs]
    eend = [jnp.exp(last[p] - cum[p]) for p in pairs]
    v_b = [x.astype(BF16) for x in v]
    st_b = [x.astype(BF16) for x in st]

    at_h = [hsel(at[p], hd) for p, hd in heads]
    rt_h = [hsel(rt[p], hd) for p, hd in heads]
    pw = [jnp.where(stril, _dot(at_h[i], bt[p], _NT), 0.0) for i, (p, hd) in enumerate(heads)]
    t_inv = [eye + x for x in pw]
    step = 2
    while step < n:
        pw_b = [x.astype(BF16) for x in pw]
        pw = [_dot(x, x) for x in pw_b]
        t_inv = [t + _dot(t.astype(BF16), x.astype(BF16)) for t, x in zip(t_inv, pw)]
        step *= 2
    t_inv = [x.astype(BF16) for x in t_inv]

    a_ak = [jnp.where(stril, _dot(at_h[i], kt[p], _NT), 0.0).astype(BF16) for i, (p, hd) in enumerate(heads)]
    a_rb = [jnp.where(tril, _dot(rt_h[i], bt[p], _NT), 0.0).astype(BF16) for i, (p, hd) in enumerate(heads)]
    a_rk = [jnp.where(tril, _dot(rt_h[i], kt[p], _NT), 0.0).astype(BF16) for i, (p, hd) in enumerate(heads)]
    av = [_dot(a_ak[i], v_b[p]) for i, (p, hd) in enumerate(heads)]
    y_v = [_dot(a_rk[i], v_b[p]) for i, (p, hd) in enumerate(heads)]

    x = [(_dot(at[p], st_b[p], _NT) + jnp.where(m0, av[2 * p], av[2 * p + 1])).astype(BF16) for p in pairs]
    u = [jnp.where(m0, _dot(t_inv[2 * p], x[p]), _dot(t_inv[2 * p + 1], x[p])) for p in pairs]
    u_b = [t.astype(BF16) for t in u]
    y = [_dot(rt[p], st_b[p], _NT)
         + jnp.where(m0, _dot(a_rb[2 * p], u_b[p]) + y_v[2 * p], _dot(a_rb[2 * p + 1], u_b[p]) + y_v[2 * p + 1])
         for p in pairs]

    row = lax.broadcasted_iota(jnp.int32, (LANES, LANES), 0)
    colm = lax.broadcasted_iota(jnp.int32, (LANES, LANES), 1)
    bdiag = (row < C_HEAD) == (colm < C_HEAD)
    for p in pairs:
        uv = jnp.concatenate([u_b[p], v_b[p]], axis=0)
        bk = jnp.concatenate([(b[p] * eend[p]).astype(BF16), (kmod[p] * eend[p]).astype(BF16)], axis=0)
        st_ref[p] = st[p] * jnp.exp(last[p]) + jnp.where(bdiag, _dot(uv, bk, _TN), 0.0)

    inv_n = 1.0 / C_HEAD
    for p in pairs:
        yp = y[p]
        mu = _head_sum(yp, m0) * inv_n
        dy = yp - mu
        var = _head_sum(dy * dy, m0) * inv_n
        yn = dy * lax.rsqrt(var + C_GN_EPS) * gg_ref[:, cols[p]] + gb_ref[:, cols[p]]
        bonus = _head_sum(r[p] * kmod[p] * rk_ref[:, cols[p]], m0) * v[p]
        o_ref[:, cols[p]] = ((yn + bonus) * g_ref[:, cols[p]]).astype(o_ref.dtype)


def _rwkv_scan(r, k, v, lw, a, g, kk, ka, rk, gg, gb, layer, bsz, seq):
    t, d = r.shape
    nchunk = seq // RW_CHUNK
    width = RW_PAIRS * LANES
    blk = pl.BlockSpec((RW_CHUNK, width), lambda b, p, c: (b * nchunk + c, p))
    vec = pl.BlockSpec((None, 1, width), lambda b, p, c: (layer, 0, p))
    return pl.pallas_call(
        _rwkv_scan_body,
        grid=(bsz, d // width, nchunk),
        in_specs=[blk] * 6 + [vec] * 5,
        out_specs=blk,
        out_shape=jax.ShapeDtypeStruct((t, d), BF16),
        scratch_shapes=[pltpu.VMEM((RW_PAIRS, LANES, LANES), F32)],
        compiler_params=_cparams(("parallel", "parallel", "arbitrary")),
        name="rwkv_scan",
    )(r, k, v, lw, a, g, kk, ka, rk, gg, gb)


def _pad_cols(w, n):
    return jnp.pad(w, ((0, 0), (0, 0), (0, n - w.shape[-1])))


def _pad_rows(w, n):
    return jnp.pad(w, ((0, 0), (0, n - w.shape[1]), (0, 0)))


def kernel(x, p, norms, final_norm, ffn_wg, ffn_wu, ffn_wd, ple_wp, ple_wg, e_w_in, e_w_out, a_vnorm, a_ws, a_bs, b_onorm, b_lb_logits, c_mix, c_wr, c_wk, c_wv, c_wo, c_w0, c_w1, c_w2, c_a0, c_a1, c_a2, c_g1, c_g2, c_kk, c_ka, c_rk, c_gn_g, c_gn_b, c_v0, c_v1, c_v2):
    bsz, seq, d = x.shape
    depth = p.shape[0]
    t = bsz * seq
    a_width = a_vnorm.shape[-1]
    b_width = b_onorm.shape[-1]

    bf = lambda w: w.astype(BF16)
    wg_b, wu_b, wd_b = ffn_wg, ffn_wu, ffn_wd
    ple_wp_b, ple_wg_b = bf(ple_wp), bf(ple_wg)
    e_in_b = e_w_in
    e_out_b = e_w_out
    wr_b, wk_b, wv_b, wo_b = c_wr, c_wk, c_wv, c_wo
    w1_b = bf(_pad_cols(c_w1, LORA_PAD))
    w2_b = bf(_pad_rows(c_w2, LORA_PAD))
    a1_b = bf(_pad_cols(c_a1, LORA_PAD))
    a2_b = bf(_pad_rows(c_a2, LORA_PAD))
    g1_b, g2_b = bf(c_g1), bf(c_g2)
    v1_b = bf(_pad_cols(c_v1, LORA_PAD))
    v2_b = bf(_pad_rows(c_v2, LORA_PAD))

    vec3 = lambda w: w.reshape(w.shape[0], 1, -1)
    a_vnorm3, b_onorm3 = vec3(a_vnorm), vec3(b_onorm)
    a_bs4 = a_bs.reshape(a_bs.shape + (1,))
    w0_3, a0_3, v0_3 = vec3(c_w0), vec3(c_a0), vec3(c_v0)
    kk3, ka3, rk3, gg3, gb3 = vec3(c_kk), vec3(c_ka), vec3(c_rk), vec3(c_gn_g), vec3(c_gn_b)
    fg = final_norm.reshape(1, d)

    h = x.reshape(t, d)
    p2 = p.reshape(depth, t, p.shape[-1])
    v_first = None
    for i in range(depth):
        j = i // 2
        h = _ffn(h, norms[i, 0].reshape(1, d), wg_b, wu_b, wd_b, i, 0)
        g1n = norms[i, 1].reshape(1, d)
        if i % 2 == 0:
            proj = _nmm(h, g1n, e_in_b, j)
            a_out = _gmlp(proj, a_vnorm3, a_ws, a_bs4, j, a_width)
            b_out = _hgrn2(proj, b_lb_logits, b_onorm3, i, j, bsz, seq, a_width, b_width)
            h = _mm2_res(h, a_out, b_out, e_out_b, j)
        else:
            xr, xw, xk, xv, xa, xg = _rwkv_mix(h, g1n, c_mix, j, seq)
            r = _mm(xr, wr_b, j)
            k = _mm(xk, wk_b, j)
            v = _mm(xv, wv_b, j)
            hw = _mm(xw, w1_b, j, act="tanh")
            ha = _mm(xa, a1_b, j)
            hg = _mm(xg, g1_b, j, act="sigmoid")
            if j == 0:
                lw, a, g = _rwkv_prep(hw, ha, hg, w2_b, a2_b, g2_b, w0_3, a0_3, j)
                v_first = v
            else:
                hv = _mm(xv, v1_b, j - 1)
                lw, a, g, v = _rwkv_prep(hw, ha, hg, w2_b, a2_b, g2_b, w0_3, a0_3, j,
                                         vres=(hv, v, v_first, v2_b, v0_3, j - 1))
            y = _rwkv_scan(r, k, v, lw, a, g, kk3, ka3, rk3, gg3, gb3, j, bsz, seq)
            h = _mm_res(h, y, wo_b, j)
        h = _ffn(h, norms[i, 2].reshape(1, d), wg_b, wu_b, wd_b, i, 1)
        h = _ple(h, norms[i, 3].reshape(1, d), p2, ple_wg_b, ple_wp_b, fg, i, final=(i == depth - 1))
    return h.reshape(bsz, seq, d)
```

```python
import functools

import jax
import jax.numpy as jnp
from jax import lax
from jax.experimental import pallas as pl
from jax.experimental.pallas import tpu as pltpu

F32 = jnp.float32
BF16 = jnp.bfloat16

LANES = 128
RMS_EPS = 1e-6
A_CHUNK = 128
B_HEAD = 128
B_MIN_F = 1e-30
C_HEAD = 64
C_GN_EPS = 64e-5
LORA_PAD = 128

VMEM_LIMIT = 56 * 1024 * 1024

_NN = ((1,), (0,))
_NT = ((1,), (1,))
_TN = ((0,), (0,))


def _cparams(sem):
    return pltpu.CompilerParams(dimension_semantics=sem, vmem_limit_bytes=VMEM_LIMIT)


def _rms(x, g, eps=RMS_EPS):
    return x * lax.rsqrt(jnp.mean(x * x, axis=-1, keepdims=True) + eps) * g


def _sigmoid(x):
    return 1.0 / (1.0 + jnp.exp(-x))


def _silu(x):
    return x * _sigmoid(x)


def _gelu_tanh(x):
    return 0.5 * x * (1.0 + jnp.tanh(0.7978845608028654 * (x + 0.044715 * (x * x * x))))


def _dot(a, b, dims=_NN):
    return lax.dot_general(a, b, (dims, ((), ())), preferred_element_type=F32)


def _sp(x, lo):
    hi = x.astype(BF16)
    if not lo:
        return (hi,)
    return (hi, (x - hi.astype(F32)).astype(BF16))


def _pd(a, b, dims=_NN):
    out = _dot(a[0], b[0], dims)
    extra = None
    if len(b) > 1:
        extra = _dot(a[0], b[1], dims)
    if len(a) > 1:
        t = _dot(a[1], b[0], dims)
        extra = t if extra is None else extra + t
    return out if extra is None else out + extra


def _cumsum_rows(tri_bf16, x):
    hi = x.astype(BF16)
    r1 = x - hi.astype(F32)
    mid = r1.astype(BF16)
    lo = (r1 - mid.astype(F32)).astype(BF16)
    return _dot(tri_bf16, hi) + (_dot(tri_bf16, mid) + _dot(tri_bf16, lo))


def _tri_masks(n):
    row = lax.broadcasted_iota(jnp.int32, (n, n), 0)
    col = lax.broadcasted_iota(jnp.int32, (n, n), 1)
    return col <= row, col < row


FFN_DOWN_COLS = 512


def _ffn_body(x_ref, g_ref, wg_ref, wu_ref, wd_ref, o_ref, xn_ref):
    j = pl.program_id(1)

    @pl.when(j == 0)
    def _():
        x = x_ref[...]
        xn_ref[...] = _rms(x, g_ref[...]).astype(BF16)
        o_ref[...] = x

    xn = xn_ref[...]
    gate = _dot(xn, wg_ref[...].astype(BF16))
    up = _dot(xn, wu_ref[...].astype(BF16))
    hid = (0.5 * _silu(gate) * up).astype(BF16)
    for c in range(0, o_ref.shape[1], FFN_DOWN_COLS):
        cols = slice(c, c + FFN_DOWN_COLS)
        o_ref[:, cols] += _dot(hid, wd_ref[:, cols].astype(BF16))


def _ffn(h, g, wg, wu, wd, layer, half, tm=1024, tf=256):
    t, d = h.shape
    f = wg.shape[-1]
    tm = min(tm, t)
    return pl.pallas_call(
        _ffn_body,
        grid=(t // tm, f // tf),
        in_specs=[
            pl.BlockSpec((tm, d), lambda i, j: (i, 0)),
            pl.BlockSpec((1, d), lambda i, j: (0, 0)),
            pl.BlockSpec((None, None, d, tf), lambda i, j: (layer, half, 0, j)),
            pl.BlockSpec((None, None, d, tf), lambda i, j: (layer, half, 0, j)),
            pl.BlockSpec((None, None, tf, d), lambda i, j: (layer, half, j, 0)),
        ],
        out_specs=pl.BlockSpec((tm, d), lambda i, j: (i, 0)),
        out_shape=jax.ShapeDtypeStruct((t, d), F32),
        scratch_shapes=[pltpu.VMEM((tm, d), BF16)],
        compiler_params=_cparams(("parallel", "arbitrary")),
        name="ffn",
    )(h, g, wg, wu, wd)


def _mm_body(x_ref, w_ref, o_ref, *, act):
    y = _dot(x_ref[...], w_ref[...].astype(BF16))
    if act == "tanh":
        y = jnp.tanh(y)
    elif act == "sigmoid":
        y = _sigmoid(y)
    o_ref[...] = y.astype(o_ref.dtype)


def _mm(x, w, layer, act=None, tm=1024, tn=1024):
    t, k = x.shape
    n = w.shape[-1]
    tm, tn = min(tm, t), min(tn, n)
    return pl.pallas_call(
        functools.partial(_mm_body, act=act),
        grid=(t // tm, n // tn),
        in_specs=[
            pl.BlockSpec((tm, k), lambda i, j: (i, 0)),
            pl.BlockSpec((None, k, tn), lambda i, j: (layer, 0, j)),
        ],
        out_specs=pl.BlockSpec((tm, tn), lambda i, j: (i, j)),
        out_shape=jax.ShapeDtypeStruct((t, n), F32),
        compiler_params=_cparams(("parallel", "parallel")),
        name="mm",
    )(x, w)


def _mm_res_body(h_ref, x_ref, w_ref, o_ref):
    o_ref[...] = h_ref[...] + _dot(x_ref[...], w_ref[...].astype(BF16))


def _mm_res(h, x, w, layer, tm=1024, tn=1024):
    t, k = x.shape
    n = w.shape[-1]
    tm = min(tm, t)
    return pl.pallas_call(
        _mm_res_body,
        grid=(t // tm, n // tn),
        in_specs=[
            pl.BlockSpec((tm, tn), lambda i, j: (i, j)),
            pl.BlockSpec((tm, k), lambda i, j: (i, 0)),
            pl.BlockSpec((None, k, tn), lambda i, j: (layer, 0, j)),
        ],
        out_specs=pl.BlockSpec((tm, tn), lambda i, j: (i, j)),
        out_shape=jax.ShapeDtypeStruct((t, n), F32),
        compiler_params=_cparams(("parallel", "parallel")),
        name="mm_res",
    )(h, x, w)


def _mm2_res_body(h_ref, xa_ref, xb_ref, wa_ref, wb_ref, o_ref):
    o_ref[...] = h_ref[...] + (_dot(xa_ref[...], wa_ref[...].astype(BF16))
                               + _dot(xb_ref[...], wb_ref[...].astype(BF16)))


def _mm2_res(h, xa, xb, w, layer, tm=1024, tn=1024):
    t, ka = xa.shape
    kb = xb.shape[1]
    n = w.shape[-1]
    assert ka == kb
    tm = min(tm, t)
    return pl.pallas_call(
        _mm2_res_body,
        grid=(t // tm, n // tn),
        in_specs=[
            pl.BlockSpec((tm, tn), lambda i, j: (i, j)),
            pl.BlockSpec((tm, ka), lambda i, j: (i, 0)),
            pl.BlockSpec((tm, kb), lambda i, j: (i, 0)),
            pl.BlockSpec((None, ka, tn), lambda i, j: (layer, 0, j)),
            pl.BlockSpec((None, kb, tn), lambda i, j: (layer, 1, j)),
        ],
        out_specs=pl.BlockSpec((tm, tn), lambda i, j: (i, j)),
        out_shape=jax.ShapeDtypeStruct((t, n), F32),
        compiler_params=_cparams(("parallel", "parallel")),
        name="mm2_res",
    )(h, xa, xb, w, w)


def _nmm_body(x_ref, g_ref, w_ref, o_ref, xn_ref):
    @pl.when(pl.program_id(1) == 0)
    def _():
        xn_ref[...] = _rms(x_ref[...], g_ref[...]).astype(BF16)

    o_ref[...] = _dot(xn_ref[...], w_ref[...].astype(BF16))


def _nmm(h, g, w, layer, tm=1024, tn=1024):
    t, d = h.shape
    n = w.shape[-1]
    tm = min(tm, t)
    return pl.pallas_call(
        _nmm_body,
        grid=(t // tm, n // tn),
        in_specs=[
            pl.BlockSpec((tm, d), lambda i, j: (i, 0)),
            pl.BlockSpec((1, d), lambda i, j: (0, 0)),
            pl.BlockSpec((None, d, tn), lambda i, j: (layer, 0, j)),
        ],
        out_specs=pl.BlockSpec((tm, tn), lambda i, j: (i, j)),
        out_shape=jax.ShapeDtypeStruct((t, n), F32),
        scratch_shapes=[pltpu.VMEM((tm, d), BF16)],
        compiler_params=_cparams(("parallel", "arbitrary")),
        name="norm_mm",
    )(h, g, w)


def _ple_body(h_ref, g_ref, p_ref, wg_ref, wp_ref, fg_ref, o_ref, *, final):
    h = h_ref[...]
    xn = _rms(h, g_ref[...]).astype(BF16)
    gate = _sigmoid(_dot(xn, wg_ref[...]))
    pe = _dot(p_ref[...].astype(BF16), wp_ref[...])
    out = h + gate * pe
    if final:
        out = _rms(out, fg_ref[...])
    o_ref[...] = out


def _ple(h, g, p, wg, wp, fg, layer, final, tm=512):
    t, d = h.shape
    pd = p.shape[-1]
    return pl.pallas_call(
        functools.partial(_ple_body, final=final),
        grid=(t // tm,),
        in_specs=[
            pl.BlockSpec((tm, d), lambda i: (i, 0)),
            pl.BlockSpec((1, d), lambda i: (0, 0)),
            pl.BlockSpec((None, tm, pd), lambda i: (layer, i, 0)),
            pl.BlockSpec((None, d, d), lambda i: (layer, 0, 0)),
            pl.BlockSpec((None, pd, d), lambda i: (layer, 0, 0)),
            pl.BlockSpec((1, d), lambda i: (0, 0)),
        ],
        out_specs=pl.BlockSpec((tm, d), lambda i: (i, 0)),
        out_shape=jax.ShapeDtypeStruct((t, d), F32),
        compiler_params=_cparams(("parallel",)),
        name="ple",
    )(h, g, p, wg, wp, fg)


GMLP_CHUNKS = 2


def _gmlp_body(u_ref, v_ref, gain_ref, ws_ref, bs_ref, o_ref):
    groups = ws_ref.shape[0]
    tril, _ = _tri_masks(A_CHUNK)
    for g in range(groups):
        w = jnp.where(tril, ws_ref[g], 0.0).astype(BF16)
        bias = bs_ref[g]
        cols = slice(g * A_CHUNK, (g + 1) * A_CHUNK)
        for c in range(GMLP_CHUNKS):
            rows = slice(c * A_CHUNK, (c + 1) * A_CHUNK)
            u = _gelu_tanh(u_ref[rows, cols])
            v = _gelu_tanh(v_ref[rows, cols])
            vg = _rms(v, gain_ref[:, cols])
            s = _dot(w, vg.astype(BF16)) + bias
            o_ref[rows, cols] = (u * s).astype(o_ref.dtype)


def _gmlp(proj, gain, ws, bs, layer, a_width):
    t = proj.shape[0]
    groups = a_width // A_CHUNK
    tm = GMLP_CHUNKS * A_CHUNK
    return pl.pallas_call(
        _gmlp_body,
        grid=(t // tm,),
        in_specs=[
            pl.BlockSpec((tm, a_width), lambda c: (c, 0)),
            pl.BlockSpec((tm, a_width), lambda c: (c, 1)),
            pl.BlockSpec((None, 1, a_width), lambda c: (layer, 0, 0)),
            pl.BlockSpec((None, groups, A_CHUNK, A_CHUNK), lambda c: (layer, 0, 0, 0)),
            pl.BlockSpec((None, groups, A_CHUNK, 1), lambda c: (layer, 0, 0, 0)),
        ],
        out_specs=pl.BlockSpec((tm, a_width), lambda c: (c, 0)),
        out_shape=jax.ShapeDtypeStruct((t, a_width), BF16),
        compiler_params=_cparams(("parallel",)),
        name="gmlp",
    )(proj, proj, gain, ws, bs)


HG_CHUNK = 64
HG_SUB = 16


HG_HEADS = 8
P_HG = False


def _hgrn2_body(q_ref, f_ref, i_ref, g_ref, lbl_ref, on_ref, o_ref, st_ref, *, layer):
    c = pl.program_id(2)

    @pl.when(c == 0)
    def _():
        st_ref[...] = jnp.zeros_like(st_ref)

    logits = lbl_ref[...]
    e = jnp.exp(logits - jnp.max(logits, axis=0, keepdims=True))
    probs = e / jnp.sum(e, axis=0, keepdims=True)
    lb = jnp.zeros((1, logits.shape[1]), F32)
    for r in range(1, layer + 1):
        lb = lb + probs[r:r + 1, :]

    n = HG_CHUNK
    hs = range(HG_HEADS)
    cols = [slice(hd * B_HEAD, (hd + 1) * B_HEAD) for hd in hs]
    tril, _ = _tri_masks(n)
    tri = jnp.where(tril, 1.0, 0.0).astype(BF16)

    st = [st_ref[hd] for hd in hs]
    v = [i_ref[:, c] for c in cols]
    f = [lb[:, c] + (1.0 - lb[:, c]) * _sigmoid(f_ref[:, c]) for c in cols]
    kf = [1.0 - x for x in f]
    qf = [_silu(q_ref[:, c]) for c in cols]
    cum = [_cumsum_rows(tri, jnp.log(jnp.maximum(x, B_MIN_F))) for x in f]
    last = [x[n - 1:n, :] for x in cum]

    st_s = [_sp(x, P_HG) for x in st]
    v_s = [_sp(x, P_HG) for x in v]
    o = [_pd(_sp(qf[h] * jnp.exp(cum[h]), P_HG), st_s[h], _NT) for h in hs]
    for h in hs:
        kend = kf[h] * jnp.exp(last[h] - cum[h])
        st_ref[h] = st[h] * jnp.exp(last[h]) + _pd(v_s[h], _sp(kend, P_HG), _TN)

    nsub = n // HG_SUB
    trow = lax.broadcasted_iota(jnp.int32, (HG_SUB, 1), 0)
    rows = [[] for _ in hs]
    for bi in range(nsub):
        lo, hi = bi * HG_SUB, (bi + 1) * HG_SUB
        acc = [jnp.zeros((HG_SUB, B_HEAD), F32) for _ in hs]
        if bi > 0:
            ref = [cum[h][lo - 1:lo, :] for h in hs]
            qh = [_sp(qf[h][lo:hi] * jnp.exp(cum[h][lo:hi] - ref[h]), P_HG) for h in hs]
            kh = [_sp(kf[h][:lo] * jnp.exp(ref[h] - cum[h][:lo]), P_HG) for h in hs]
            att = [_sp(_pd(qh[h], kh[h], _NT), P_HG) for h in hs]
            acc = [_pd(att[h], tuple(part[:lo] for part in v_s[h])) for h in hs]
        for s in range(HG_SUB):
            for h in hs:
                c_b = cum[h][lo:hi]
                dec = jnp.exp(jnp.minimum(c_b - c_b[s:s + 1, :], 0.0))
                col = jnp.sum(qf[h][lo:hi] * dec * kf[h][lo + s:lo + s + 1, :], axis=-1, keepdims=True)
                col = jnp.where(trow >= s, col, 0.0)
                acc[h] = acc[h] + col * v[h][lo + s:lo + s + 1, :]
        for h in hs:
            rows[h].append(acc[h])

    for h in hs:
        out = o[h] + jnp.concatenate(rows[h], axis=0)
        o_ref[:, cols[h]] = (_rms(out, on_ref[:, cols[h]]) * _silu(g_ref[:, cols[h]])).astype(o_ref.dtype)


def _hgrn2(proj, lb_logits, onorm, layer, elayer, bsz, seq, a_width, b_width):
    t = proj.shape[0]
    heads = b_width // B_HEAD
    hgroups = heads // HG_HEADS
    nchunk = seq // HG_CHUNK
    width = HG_HEADS * B_HEAD
    off = 2 * a_width // width

    def col(which):
        return lambda b, h, c: (b * nchunk + c, off + which * hgroups + h)

    blk = (HG_CHUNK, width)
    depth = lb_logits.shape[0]
    return pl.pallas_call(
        functools.partial(_hgrn2_body, layer=layer),
        grid=(bsz, hgroups, nchunk),
        in_specs=[
            pl.BlockSpec(blk, col(0)),
            pl.BlockSpec(blk, col(1)),
            pl.BlockSpec(blk, col(2)),
            pl.BlockSpec(blk, col(3)),
            pl.BlockSpec((depth, width), lambda b, h, c: (0, h)),
            pl.BlockSpec((None, 1, width), lambda b, h, c: (elayer, 0, h)),
        ],
        out_specs=pl.BlockSpec(blk, lambda b, h, c: (b * nchunk + c, h)),
        out_shape=jax.ShapeDtypeStruct((t, b_width), BF16),
        scratch_shapes=[pltpu.VMEM((HG_HEADS, B_HEAD, B_HEAD), F32)],
        compiler_params=_cparams(("parallel", "parallel", "arbitrary")),
        name="hgrn2",
    )(proj, proj, proj, proj, lb_logits, onorm)


def _rwkv_mix_body(x_ref, xp_ref, g_ref, mix_ref, *o_refs, tm, seq):
    i = pl.program_id(0)
    g = g_ref[...]
    hn = _rms(x_ref[...], g)
    hp = _rms(xp_ref[...], g)[7:8, :]
    hp = jnp.where((i * tm) % seq == 0, 0.0, hp)
    row = lax.broadcasted_iota(jnp.int32, hn.shape, 0)
    prev = jnp.where(row == 0, hp, pltpu.roll(hn, 1, 0))
    xx = prev - hn
    for j, o_ref in enumerate(o_refs):
        o_ref[...] = (hn + xx * mix_ref[j:j + 1, :]).astype(BF16)


def _rwkv_mix(h, g, mix, layer, seq, tm=256):
    t, d = h.shape
    nmix = mix.shape[1]
    sub = 8
    return pl.pallas_call(
        functools.partial(_rwkv_mix_body, tm=tm, seq=seq),
        grid=(t // tm,),
        in_specs=[
            pl.BlockSpec((tm, d), lambda i: (i, 0)),
            pl.BlockSpec((sub, d), lambda i: (jnp.maximum(i * (tm // sub) - 1, 0), 0)),
            pl.BlockSpec((1, d), lambda i: (0, 0)),
            pl.BlockSpec((None, nmix, d), lambda i: (layer, 0, 0)),
        ],
        out_specs=[pl.BlockSpec((tm, d), lambda i: (i, 0)) for _ in range(nmix)],
        out_shape=[jax.ShapeDtypeStruct((t, d), BF16) for _ in range(nmix)],
        compiler_params=_cparams(("parallel",)),
        name="rwkv_mix",
    )(h, h, g, mix)


def _rwkv_prep_body(*refs, vres):
    if vres:
        (hw_ref, ha_ref, hg_ref, hv_ref, v_ref, vf_ref, w2_ref, a2_ref, g2_ref, v2_ref,
         w0_ref, a0_ref, v0_ref, lw_ref, a_ref, g_ref, vo_ref) = refs
    else:
        (hw_ref, ha_ref, hg_ref, w2_ref, a2_ref, g2_ref, w0_ref, a0_ref,
         lw_ref, a_ref, g_ref) = refs
    z = -(w0_ref[...] + _dot(hw_ref[...].astype(BF16), w2_ref[...]))
    softplus = jnp.maximum(z, 0.0) + jnp.log(1.0 + jnp.exp(-jnp.abs(z)))
    w = -softplus - 0.5
    lw_ref[...] = -jnp.exp(w)
    a_ref[...] = _sigmoid(a0_ref[...] + _dot(ha_ref[...].astype(BF16), a2_ref[...]))
    g_ref[...] = _dot(hg_ref[...].astype(BF16), g2_ref[...])
    if vres:
        v = v_ref[...]
        mv = _sigmoid(v0_ref[...] + _dot(hv_ref[...].astype(BF16), v2_ref[...]))
        vo_ref[...] = v + (vf_ref[...] - v) * mv


def _rwkv_prep(hw, ha, hg, w2, a2, g2, w0, a0, layer, vres=None, tm=256):
    t = hw.shape[0]
    d = w2.shape[-1]

    def tok(n):
        return pl.BlockSpec((tm, n), lambda i: (i, 0))

    def wgt(k):
        return pl.BlockSpec((None, k, d), lambda i: (layer, 0, 0))

    def vec(idx):
        return pl.BlockSpec((None, 1, d), lambda i: (idx, 0, 0))

    if vres is None:
        args = (hw, ha, hg, w2, a2, g2, w0, a0)
        in_specs = [tok(hw.shape[1]), tok(ha.shape[1]), tok(hg.shape[1]),
                    wgt(w2.shape[1]), wgt(a2.shape[1]), wgt(g2.shape[1]), vec(layer), vec(layer)]
        nout = 3
    else:
        hv, v, vf, v2, v0, vl = vres
        args = (hw, ha, hg, hv, v, vf, w2, a2, g2, v2, w0, a0, v0)
        in_specs = [tok(hw.shape[1]), tok(ha.shape[1]), tok(hg.shape[1]), tok(hv.shape[1]), tok(d), tok(d),
                    wgt(w2.shape[1]), wgt(a2.shape[1]), wgt(g2.shape[1]),
                    pl.BlockSpec((None, v2.shape[1], d), lambda i: (vl, 0, 0)),
                    vec(layer), vec(layer), vec(vl)]
        nout = 4
    return pl.pallas_call(
        functools.partial(_rwkv_prep_body, vres=vres is not None),
        grid=(t // tm,),
        in_specs=in_specs,
        out_specs=[tok(d) for _ in range(nout)],
        out_shape=[jax.ShapeDtypeStruct((t, d), F32) for _ in range(nout)],
        compiler_params=_cparams(("parallel",)),
        name="rwkv_prep",
    )(*args)


RW_CHUNK = 64


RW_PAIRS = 16


def _head_sum(x, m0):
    s0 = jnp.sum(jnp.where(m0, x, 0.0), axis=-1, keepdims=True)
    s1 = jnp.sum(jnp.where(m0, 0.0, x), axis=-1, keepdims=True)
    return jnp.where(m0, s0, s1)


def _rwkv_scan_body(r_ref, k_ref, v_ref, lw_ref, a_ref, g_ref, kk_ref, ka_ref, rk_ref, gg_ref, gb_ref,
                    o_ref, st_ref):
    c = pl.program_id(2)

    @pl.when(c == 0)
    def _():
        st_ref[...] = jnp.zeros_like(st_ref)

    n = RW_CHUNK
    pairs = range(RW_PAIRS)
    heads = [(p, hd) for p in pairs for hd in range(2)]
    cols = [slice(p * LANES, (p + 1) * LANES) for p in pairs]
    lane = lax.broadcasted_iota(jnp.int32, (1, LANES), 1)
    m0 = lane < C_HEAD
    m1 = jnp.logical_not(m0)
    tril, stril = _tri_masks(n)
    tri = jnp.where(tril, 1.0, 0.0).astype(BF16)
    eye = jnp.where(tril & jnp.logical_not(stril), 1.0, 0.0)

    def hsel(x, hd):
        return jnp.where(m0 if hd == 0 else m1, x, jnp.zeros_like(x))

    st = [st_ref[p] for p in pairs]
    r = [r_ref[:, c] for c in cols]
    k = [k_ref[:, c] for c in cols]
    v = [v_ref[:, c] for c in cols]
    asig = [a_ref[:, c] for c in cols]

    cum = [_cumsum_rows(tri, lw_ref[:, c]) for c in cols]
    last = [x[n - 1:n, :] for x in cum]
    kkr = [k[p] * kk_ref[:, cols[p]] for p in pairs]
    kk = [x / jnp.maximum(jnp.sqrt(_head_sum(x * x, m0)), 1e-12) for x in kkr]
    kmod = [k[p] * (1.0 + (asig[p] - 1.0) * ka_ref[:, cols[p]]) for p in pairs]
    b = [kk[p] * asig[p] for p in pairs]
    rt = [(r[p] * jnp.exp(cum[p])).astype(BF16) for p in pairs]
    at = [(-kk[p] * jnp.exp(cum[p] - lw_ref[:, cols[p]])).astype(BF16) for p in pairs]
    einv = [jnp.exp(-x) for x in cum]
    bt = [(b[p] * einv[p]).astype(BF16) for p in pairs]
    kt = [(kmod[p] * einv[p]).astype(BF16) for p in pairs]
    eend = [jnp.exp(last[p] - cum[p]) for p in pairs]
    v_b = [x.astype(BF16) for x in v]
    st_b = [x.astype(BF16) for x in st]

    at_h = [hsel(at[p], hd) for p, hd in heads]
    rt_h = [hsel(rt[p], hd) for p, hd in heads]
    pw = [jnp.where(stril, _dot(at_h[i], bt[p], _NT), 0.0) for i, (p, hd) in enumerate(heads)]
    t_inv = [eye + x for x in pw]
    fill = {}
    fillers = [
        lambda: fill.update(a_ak=[jnp.where(stril, _dot(at_h[i], kt[p], _NT), 0.0).astype(BF16)
                                  for i, (p, hd) in enumerate(heads)]),
        lambda: fill.update(a_rb=[jnp.where(tril, _dot(rt_h[i], bt[p], _NT), 0.0).astype(BF16)
                                  for i, (p, hd) in enumerate(heads)]),
        lambda: fill.update(a_rk=[jnp.where(tril, _dot(rt_h[i], kt[p], _NT), 0.0).astype(BF16)
                                  for i, (p, hd) in enumerate(heads)]),
        lambda: fill.update(av=[_dot(fill["a_ak"][i], v_b[p]) for i, (p, hd) in enumerate(heads)]),
        lambda: fill.update(y_v=[_dot(fill["a_rk"][i], v_b[p]) for i, (p, hd) in enumerate(heads)]),
    ]
    step = 2
    while step < n:
        pw_b = [x.astype(BF16) for x in pw]
        pw = [_dot(x, x) for x in pw_b]
        if fillers:
            fillers.pop(0)()
        t_inv = [t + _dot(t.astype(BF16), x.astype(BF16)) for t, x in zip(t_inv, pw)]
        step *= 2
    for filler in fillers:
        filler()
    t_inv = [x.astype(BF16) for x in t_inv]
    a_rb, av, y_v = fill["a_rb"], fill["av"], fill["y_v"]

    x = [(_dot(at[p], st_b[p], _NT) + jnp.where(m0, av[2 * p], av[2 * p + 1])).astype(BF16) for p in pairs]
    u = [jnp.where(m0, _dot(t_inv[2 * p], x[p]), _dot(t_inv[2 * p + 1], x[p])) for p in pairs]
    u_b = [t.astype(BF16) for t in u]
    y = [_dot(rt[p], st_b[p], _NT)
         + jnp.where(m0, _dot(a_rb[2 * p], u_b[p]) + y_v[2 * p], _dot(a_rb[2 * p + 1], u_b[p]) + y_v[2 * p + 1])
         for p in pairs]

    row = lax.broadcasted_iota(jnp.int32, (LANES, LANES), 0)
    colm = lax.broadcasted_iota(jnp.int32, (LANES, LANES), 1)
    bdiag = (row < C_HEAD) == (colm < C_HEAD)
    for p in pairs:
        uv = jnp.concatenate([u_b[p], v_b[p]], axis=0)
        bk = jnp.concatenate([(b[p] * eend[p]).astype(BF16), (kmod[p] * eend[p]).astype(BF16)], axis=0)
        st_ref[p] = st[p] * jnp.exp(last[p]) + jnp.where(bdiag, _dot(uv, bk, _TN), 0.0)

    inv_n = 1.0 / C_HEAD
    for p in pairs:
        yp = y[p]
        mu = _head_sum(yp, m0) * inv_n
        dy = yp - mu
        var = _head_sum(dy * dy, m0) * inv_n
        yn = dy * lax.rsqrt(var + C_GN_EPS) * gg_ref[:, cols[p]] + gb_ref[:, cols[p]]
        bonus = _head_sum(r[p] * kmod[p] * rk_ref[:, cols[p]], m0) * v[p]
        o_ref[:, cols[p]] = ((yn + bonus) * g_ref[:, cols[p]]).astype(o_ref.dtype)


def _rwkv_scan(r, k, v, lw, a, g, kk, ka, rk, gg, gb, layer, bsz, seq):
    t, d = r.shape
    nchunk = seq // RW_CHUNK
    width = RW_PAIRS * LANES
    blk = pl.BlockSpec((RW_CHUNK, width), lambda b, p, c: (b * nchunk + c, p))
    vec = pl.BlockSpec((None, 1, width), lambda b, p, c: (layer, 0, p))
    return pl.pallas_call(
        _rwkv_scan_body,
        grid=(bsz, d // width, nchunk),
        in_specs=[blk] * 6 + [vec] * 5,
        out_specs=blk,
        out_shape=jax.ShapeDtypeStruct((t, d), BF16),
        scratch_shapes=[pltpu.VMEM((RW_PAIRS, LANES, LANES), F32)],
        compiler_params=_cparams(("parallel", "parallel", "arbitrary")),
        name="rwkv_scan",
    )(r, k, v, lw, a, g, kk, ka, rk, gg, gb)


def _pad_cols(w, n):
    return jnp.pad(w, ((0, 0), (0, 0), (0, n - w.shape[-1])))


def _pad_rows(w, n):
    return jnp.pad(w, ((0, 0), (0, n - w.shape[1]), (0, 0)))


def kernel(x, p, norms, final_norm, ffn_wg, ffn_wu, ffn_wd, ple_wp, ple_wg, e_w_in, e_w_out, a_vnorm, a_ws, a_bs, b_onorm, b_lb_logits, c_mix, c_wr, c_wk, c_wv, c_wo, c_w0, c_w1, c_w2, c_a0, c_a1, c_a2, c_g1, c_g2, c_kk, c_ka, c_rk, c_gn_g, c_gn_b, c_v0, c_v1, c_v2):
    bsz, seq, d = x.shape
    depth = p.shape[0]
    t = bsz * seq
    a_width = a_vnorm.shape[-1]
    b_width = b_onorm.shape[-1]

    bf = lambda w: w.astype(BF16)
    wg_b, wu_b, wd_b = ffn_wg, ffn_wu, ffn_wd
    ple_wp_b, ple_wg_b = bf(ple_wp), bf(ple_wg)
    e_in_b = e_w_in
    e_out_b = e_w_out
    wr_b, wk_b, wv_b, wo_b = c_wr, c_wk, c_wv, c_wo
    w1_b = bf(_pad_cols(c_w1, LORA_PAD))
    w2_b = bf(_pad_rows(c_w2, LORA_PAD))
    a1_b = bf(_pad_cols(c_a1, LORA_PAD))
    a2_b = bf(_pad_rows(c_a2, LORA_PAD))
    g1_b, g2_b = bf(c_g1), bf(c_g2)
    v1_b = bf(_pad_cols(c_v1, LORA_PAD))
    v2_b = bf(_pad_rows(c_v2, LORA_PAD))

    vec3 = lambda w: w.reshape(w.shape[0], 1, -1)
    a_vnorm3, b_onorm3 = vec3(a_vnorm), vec3(b_onorm)
    a_bs4 = a_bs.reshape(a_bs.shape + (1,))
    w0_3, a0_3, v0_3 = vec3(c_w0), vec3(c_a0), vec3(c_v0)
    kk3, ka3, rk3, gg3, gb3 = vec3(c_kk), vec3(c_ka), vec3(c_rk), vec3(c_gn_g), vec3(c_gn_b)
    fg = final_norm.reshape(1, d)

    h = x.reshape(t, d)
    p2 = p.reshape(depth, t, p.shape[-1])
    v_first = None
    for i in range(depth):
        j = i // 2
        h = _ffn(h, norms[i, 0].reshape(1, d), wg_b, wu_b, wd_b, i, 0)
        g1n = norms[i, 1].reshape(1, d)
        if i % 2 == 0:
            proj = _nmm(h, g1n, e_in_b, j)
            a_out = _gmlp(proj, a_vnorm3, a_ws, a_bs4, j, a_width)
            b_out = _hgrn2(proj, b_lb_logits, b_onorm3, i, j, bsz, seq, a_width, b_width)
            h = _mm2_res(h, a_out, b_out, e_out_b, j)
        else:
            xr, xw, xk, xv, xa, xg = _rwkv_mix(h, g1n, c_mix, j, seq)
            r = _mm(xr, wr_b, j)
            k = _mm(xk, wk_b, j)
            v = _mm(xv, wv_b, j)
            hw = _mm(xw, w1_b, j, act="tanh")
            ha = _mm(xa, a1_b, j)
            hg = _mm(xg, g1_b, j, act="sigmoid")
            if j == 0:
                lw, a, g = _rwkv_prep(hw, ha, hg, w2_b, a2_b, g2_b, w0_3, a0_3, j)
                v_first = v
            else:
                hv = _mm(xv, v1_b, j - 1)
                lw, a, g, v = _rwkv_prep(hw, ha, hg, w2_b, a2_b, g2_b, w0_3, a0_3, j,
                                         vres=(hv, v, v_first, v2_b, v0_3, j - 1))
            y = _rwkv_scan(r, k, v, lw, a, g, kk3, ka3, rk3, gg3, gb3, j, bsz, seq)
            h = _mm_res(h, y, wo_b, j)
        h = _ffn(h, norms[i, 2].reshape(1, d), wg_b, wu_b, wd_b, i, 1)
        h = _ple(h, norms[i, 3].reshape(1, d), p2, ple_wg_b, ple_wp_b, fg, i, final=(i == depth - 1))
    return h.reshape(bsz, seq, d)
```

```python
import functools

import jax
import jax.numpy as jnp
from jax import lax
from jax.experimental import pallas as pl
from jax.experimental.pallas import tpu as pltpu

F32 = jnp.float32
BF16 = jnp.bfloat16

LANES = 128
RMS_EPS = 1e-6
A_CHUNK = 128
B_HEAD = 128
B_MIN_F = 1e-30
C_HEAD = 64
C_GN_EPS = 64e-5
LORA_PAD = 128

VMEM_LIMIT = 56 * 1024 * 1024

_NN = ((1,), (0,))
_NT = ((1,), (1,))
_TN = ((0,), (0,))


def _cparams(sem):
    return pltpu.CompilerParams(dimension_semantics=sem, vmem_limit_bytes=VMEM_LIMIT)


def _rms(x, g, eps=RMS_EPS):
    return x * lax.rsqrt(jnp.mean(x * x, axis=-1, keepdims=True) + eps) * g


def _sigmoid(x):
    return 1.0 / (1.0 + jnp.exp(-x))


def _silu(x):
    return x * _sigmoid(x)


def _gelu_tanh(x):
    return 0.5 * x * (1.0 + jnp.tanh(0.7978845608028654 * (x + 0.044715 * (x * x * x))))


def _dot(a, b, dims=_NN):
    return lax.dot_general(a, b, (dims, ((), ())), preferred_element_type=F32)


def _sp(x, lo):
    hi = x.astype(BF16)
    if not lo:
        return (hi,)
    return (hi, (x - hi.astype(F32)).astype(BF16))


def _pd(a, b, dims=_NN):
    out = _dot(a[0], b[0], dims)
    extra = None
    if len(b) > 1:
        extra = _dot(a[0], b[1], dims)
    if len(a) > 1:
        t = _dot(a[1], b[0], dims)
        extra = t if extra is None else extra + t
    return out if extra is None else out + extra


def _cumsum_rows(tri_bf16, x):
    hi = x.astype(BF16)
    r1 = x - hi.astype(F32)
    mid = r1.astype(BF16)
    lo = (r1 - mid.astype(F32)).astype(BF16)
    return _dot(tri_bf16, hi) + (_dot(tri_bf16, mid) + _dot(tri_bf16, lo))


def _tri_masks(n):
    row = lax.broadcasted_iota(jnp.int32, (n, n), 0)
    col = lax.broadcasted_iota(jnp.int32, (n, n), 1)
    return col <= row, col < row


FFN_DOWN_COLS = 512


def _ffn_body(x_ref, g_ref, wg_ref, wu_ref, wd_ref, o_ref, xn_ref):
    j = pl.program_id(1)

    @pl.when(j == 0)
    def _():
        x = x_ref[...]
        xn_ref[...] = _rms(x, g_ref[...]).astype(BF16)
        o_ref[...] = x

    xn = xn_ref[...]
    gate = _dot(xn, wg_ref[...].astype(BF16))
    up = _dot(xn, wu_ref[...].astype(BF16))
    hid = (0.5 * _silu(gate) * up).astype(BF16)
    for c in range(0, o_ref.shape[1], FFN_DOWN_COLS):
        cols = slice(c, c + FFN_DOWN_COLS)
        o_ref[:, cols] += _dot(hid, wd_ref[:, cols].astype(BF16))


def _ffn(h, g, wg, wu, wd, layer, half, tm=1024, tf=256):
    t, d = h.shape
    f = wg.shape[-1]
    tm = min(tm, t)
    return pl.pallas_call(
        _ffn_body,
        grid=(t // tm, f // tf),
        in_specs=[
            pl.BlockSpec((tm, d), lambda i, j: (i, 0)),
            pl.BlockSpec((1, d), lambda i, j: (0, 0)),
            pl.BlockSpec((None, None, d, tf), lambda i, j: (layer, half, 0, j)),
            pl.BlockSpec((None, None, d, tf), lambda i, j: (layer, half, 0, j)),
            pl.BlockSpec((None, None, tf, d), lambda i, j: (layer, half, j, 0)),
        ],
        out_specs=pl.BlockSpec((tm, d), lambda i, j: (i, 0)),
        out_shape=jax.ShapeDtypeStruct((t, d), F32),
        scratch_shapes=[pltpu.VMEM((tm, d), BF16)],
        compiler_params=_cparams(("parallel", "arbitrary")),
        name="ffn",
    )(h, g, wg, wu, wd)


def _mm_body(x_ref, w_ref, o_ref, *, act):
    y = _dot(x_ref[...], w_ref[...].astype(BF16))
    if act == "tanh":
        y = jnp.tanh(y)
    elif act == "sigmoid":
        y = _sigmoid(y)
    o_ref[...] = y.astype(o_ref.dtype)


def _mm(x, w, layer, act=None, tm=1024, tn=1024):
    t, k = x.shape
    n = w.shape[-1]
    tm, tn = min(tm, t), min(tn, n)
    return pl.pallas_call(
        functools.partial(_mm_body, act=act),
        grid=(t // tm, n // tn),
        in_specs=[
            pl.BlockSpec((tm, k), lambda i, j: (i, 0)),
            pl.BlockSpec((None, k, tn), lambda i, j: (layer, 0, j)),
        ],
        out_specs=pl.BlockSpec((tm, tn), lambda i, j: (i, j)),
        out_shape=jax.ShapeDtypeStruct((t, n), F32),
        compiler_params=_cparams(("parallel", "parallel")),
        name="mm",
    )(x, w)


def _mm_res_body(h_ref, x_ref, w_ref, o_ref):
    o_ref[...] = h_ref[...] + _dot(x_ref[...], w_ref[...].astype(BF16))


def _mm_res(h, x, w, layer, tm=1024, tn=1024):
    t, k = x.shape
    n = w.shape[-1]
    tm = min(tm, t)
    return pl.pallas_call(
        _mm_res_body,
        grid=(t // tm, n // tn),
        in_specs=[
            pl.BlockSpec((tm, tn), lambda i, j: (i, j)),
            pl.BlockSpec((tm, k), lambda i, j: (i, 0)),
            pl.BlockSpec((None, k, tn), lambda i, j: (layer, 0, j)),
        ],
        out_specs=pl.BlockSpec((tm, tn), lambda i, j: (i, j)),
        out_shape=jax.ShapeDtypeStruct((t, n), F32),
        compiler_params=_cparams(("parallel", "parallel")),
        name="mm_res",
    )(h, x, w)


def _mm2_res_body(h_ref, xa_ref, xb_ref, wa_ref, wb_ref, o_ref):
    o_ref[...] = h_ref[...] + (_dot(xa_ref[...], wa_ref[...].astype(BF16))
                               + _dot(xb_ref[...], wb_ref[...].astype(BF16)))


def _mm2_res(h, xa, xb, w, layer, tm=1024, tn=1024):
    t, ka = xa.shape
    kb = xb.shape[1]
    n = w.shape[-1]
    assert ka == kb
    tm = min(tm, t)
    return pl.pallas_call(
        _mm2_res_body,
        grid=(t // tm, n // tn),
        in_specs=[
            pl.BlockSpec((tm, tn), lambda i, j: (i, j)),
            pl.BlockSpec((tm, ka), lambda i, j: (i, 0)),
            pl.BlockSpec((tm, kb), lambda i, j: (i, 0)),
            pl.BlockSpec((None, ka, tn), lambda i, j: (layer, 0, j)),
            pl.BlockSpec((None, kb, tn), lambda i, j: (layer, 1, j)),
        ],
        out_specs=pl.BlockSpec((tm, tn), lambda i, j: (i, j)),
        out_shape=jax.ShapeDtypeStruct((t, n), F32),
        compiler_params=_cparams(("parallel", "parallel")),
        name="mm2_res",
    )(h, xa, xb, w, w)


def _nmm_body(x_ref, g_ref, w_ref, o_ref, xn_ref):
    @pl.when(pl.program_id(1) == 0)
    def _():
        xn_ref[...] = _rms(x_ref[...], g_ref[...]).astype(BF16)

    o_ref[...] = _dot(xn_ref[...], w_ref[...].astype(BF16))


def _nmm(h, g, w, layer, tm=1024, tn=1024):
    t, d = h.shape
    n = w.shape[-1]
    tm = min(tm, t)
    return pl.pallas_call(
        _nmm_body,
        grid=(t // tm, n // tn),
        in_specs=[
            pl.BlockSpec((tm, d), lambda i, j: (i, 0)),
            pl.BlockSpec((1, d), lambda i, j: (0, 0)),
            pl.BlockSpec((None, d, tn), lambda i, j: (layer, 0, j)),
        ],
        out_specs=pl.BlockSpec((tm, tn), lambda i, j: (i, j)),
        out_shape=jax.ShapeDtypeStruct((t, n), F32),
        scratch_shapes=[pltpu.VMEM((tm, d), BF16)],
        compiler_params=_cparams(("parallel", "arbitrary")),
        name="norm_mm",
    )(h, g, w)


def _ple_body(h_ref, g_ref, p_ref, wg_ref, wp_ref, fg_ref, o_ref, *, final):
    h = h_ref[...]
    xn = _rms(h, g_ref[...]).astype(BF16)
    gate = _sigmoid(_dot(xn, wg_ref[...]))
    pe = _dot(p_ref[...].astype(BF16), wp_ref[...])
    out = h + gate * pe
    if final:
        out = _rms(out, fg_ref[...])
    o_ref[...] = out


def _ple(h, g, p, wg, wp, fg, layer, final, tm=512):
    t, d = h.shape
    pd = p.shape[-1]
    return pl.pallas_call(
        functools.partial(_ple_body, final=final),
        grid=(t // tm,),
        in_specs=[
            pl.BlockSpec((tm, d), lambda i: (i, 0)),
            pl.BlockSpec((1, d), lambda i: (0, 0)),
            pl.BlockSpec((None, tm, pd), lambda i: (layer, i, 0)),
            pl.BlockSpec((None, d, d), lambda i: (layer, 0, 0)),
            pl.BlockSpec((None, pd, d), lambda i: (layer, 0, 0)),
            pl.BlockSpec((1, d), lambda i: (0, 0)),
        ],
        out_specs=pl.BlockSpec((tm, d), lambda i: (i, 0)),
        out_shape=jax.ShapeDtypeStruct((t, d), F32),
        compiler_params=_cparams(("parallel",)),
        name="ple",
    )(h, g, p, wg, wp, fg)


GMLP_CHUNKS = 2


def _gmlp_body(u_ref, v_ref, gain_ref, ws_ref, bs_ref, o_ref):
    groups = ws_ref.shape[0]
    tril, _ = _tri_masks(A_CHUNK)
    for g in range(groups):
        w = jnp.where(tril, ws_ref[g], 0.0).astype(BF16)
        bias = bs_ref[g]
        cols = slice(g * A_CHUNK, (g + 1) * A_CHUNK)
        for c in range(GMLP_CHUNKS):
            rows = slice(c * A_CHUNK, (c + 1) * A_CHUNK)
            u = _gelu_tanh(u_ref[rows, cols])
            v = _gelu_tanh(v_ref[rows, cols])
            vg = _rms(v, gain_ref[:, cols])
            s = _dot(w, vg.astype(BF16)) + bias
            o_ref[rows, cols] = (u * s).astype(o_ref.dtype)


def _gmlp(proj, gain, ws, bs, layer, a_width):
    t = proj.shape[0]
    groups = a_width // A_CHUNK
    tm = GMLP_CHUNKS * A_CHUNK
    return pl.pallas_call(
        _gmlp_body,
        grid=(t // tm,),
        in_specs=[
            pl.BlockSpec((tm, a_width), lambda c: (c, 0)),
            pl.BlockSpec((tm, a_width), lambda c: (c, 1)),
            pl.BlockSpec((None, 1, a_width), lambda c: (layer, 0, 0)),
            pl.BlockSpec((None, groups, A_CHUNK, A_CHUNK), lambda c: (layer, 0, 0, 0)),
            pl.BlockSpec((None, groups, A_CHUNK, 1), lambda c: (layer, 0, 0, 0)),
        ],
        out_specs=pl.BlockSpec((tm, a_width), lambda c: (c, 0)),
        out_shape=jax.ShapeDtypeStruct((t, a_width), BF16),
        compiler_params=_cparams(("parallel",)),
        name="gmlp",
    )(proj, proj, gain, ws, bs)


HG_CHUNK = 64
HG_SUB = 16


HG_HEADS = 8
P_HG = False


def _hgrn2_body(q_ref, f_ref, i_ref, g_ref, lbl_ref, on_ref, o_ref, st_ref, *, layer):
    c = pl.program_id(2)

    @pl.when(c == 0)
    def _():
        st_ref[...] = jnp.zeros_like(st_ref)

    logits = lbl_ref[...]
    e = jnp.exp(logits - jnp.max(logits, axis=0, keepdims=True))
    probs = e / jnp.sum(e, axis=0, keepdims=True)
    lb = jnp.zeros((1, logits.shape[1]), F32)
    for r in range(1, layer + 1):
        lb = lb + probs[r:r + 1, :]

    n = HG_CHUNK
    hs = range(HG_HEADS)
    cols = [slice(hd * B_HEAD, (hd + 1) * B_HEAD) for hd in hs]
    tril, _ = _tri_masks(n)
    tri = jnp.where(tril, 1.0, 0.0).astype(BF16)

    st = [st_ref[hd] for hd in hs]
    v = [i_ref[:, c] for c in cols]
    f = [lb[:, c] + (1.0 - lb[:, c]) * _sigmoid(f_ref[:, c]) for c in cols]
    kf = [1.0 - x for x in f]
    qf = [_silu(q_ref[:, c]) for c in cols]
    cum = [_cumsum_rows(tri, jnp.log(jnp.maximum(x, B_MIN_F))) for x in f]
    last = [x[n - 1:n, :] for x in cum]

    st_s = [_sp(x, P_HG) for x in st]
    v_s = [_sp(x, P_HG) for x in v]
    o = [_pd(_sp(qf[h] * jnp.exp(cum[h]), P_HG), st_s[h], _NT) for h in hs]
    for h in hs:
        kend = kf[h] * jnp.exp(last[h] - cum[h])
        st_ref[h] = st[h] * jnp.exp(last[h]) + _pd(v_s[h], _sp(kend, P_HG), _TN)

    nsub = n // HG_SUB
    trow = lax.broadcasted_iota(jnp.int32, (HG_SUB, 1), 0)
    rows = [[] for _ in hs]
    for bi in range(nsub):
        lo, hi = bi * HG_SUB, (bi + 1) * HG_SUB
        acc = [jnp.zeros((HG_SUB, B_HEAD), F32) for _ in hs]
        if bi > 0:
            ref = [cum[h][lo - 1:lo, :] for h in hs]
            qh = [_sp(qf[h][lo:hi] * jnp.exp(cum[h][lo:hi] - ref[h]), P_HG) for h in hs]
            kh = [_sp(kf[h][:lo] * jnp.exp(ref[h] - cum[h][:lo]), P_HG) for h in hs]
            att = [_sp(_pd(qh[h], kh[h], _NT), P_HG) for h in hs]
            acc = [_pd(att[h], tuple(part[:lo] for part in v_s[h])) for h in hs]
        for s in range(HG_SUB):
            for h in hs:
                c_b = cum[h][lo:hi]
                dec = jnp.exp(jnp.minimum(c_b - c_b[s:s + 1, :], 0.0))
                col = jnp.sum(qf[h][lo:hi] * dec * kf[h][lo + s:lo + s + 1, :], axis=-1, keepdims=True)
                col = jnp.where(trow >= s, col, 0.0)
                acc[h] = acc[h] + col * v[h][lo + s:lo + s + 1, :]
        for h in hs:
            rows[h].append(acc[h])

    for h in hs:
        out = o[h] + jnp.concatenate(rows[h], axis=0)
        o_ref[:, cols[h]] = (_rms(out, on_ref[:, cols[h]]) * _silu(g_ref[:, cols[h]])).astype(o_ref.dtype)


def _hgrn2(proj, lb_logits, onorm, layer, elayer, bsz, seq, a_width, b_width):
    t = proj.shape[0]
    heads = b_width // B_HEAD
    hgroups = heads // HG_HEADS
    nchunk = seq // HG_CHUNK
    width = HG_HEADS * B_HEAD
    off = 2 * a_width // width

    def col(which):
        return lambda b, h, c: (b * nchunk + c, off + which * hgroups + h)

    blk = (HG_CHUNK, width)
    depth = lb_logits.shape[0]
    return pl.pallas_call(
        functools.partial(_hgrn2_body, layer=layer),
        grid=(bsz, hgroups, nchunk),
        in_specs=[
            pl.BlockSpec(blk, col(0)),
            pl.BlockSpec(blk, col(1)),
            pl.BlockSpec(blk, col(2)),
            pl.BlockSpec(blk, col(3)),
            pl.BlockSpec((depth, width), lambda b, h, c: (0, h)),
            pl.BlockSpec((None, 1, width), lambda b, h, c: (elayer, 0, h)),
        ],
        out_specs=pl.BlockSpec(blk, lambda b, h, c: (b * nchunk + c, h)),
        out_shape=jax.ShapeDtypeStruct((t, b_width), BF16),
        scratch_shapes=[pltpu.VMEM((HG_HEADS, B_HEAD, B_HEAD), F32)],
        compiler_params=_cparams(("parallel", "parallel", "arbitrary")),
        name="hgrn2",
    )(proj, proj, proj, proj, lb_logits, onorm)


MIX_R, MIX_W, MIX_K, MIX_V, MIX_A, MIX_G = range(6)
SUBLANES = 8


def _rwkv_proj_body(*refs, tm, seq, ntile, vres):
    if vres:
        (x_ref, xp_ref, g_ref, mix_ref, w_ref, w1_ref, a1_ref, g1_ref, v1_ref,
         o_ref, hw_ref, ha_ref, hg_ref, hv_ref, xs_ref) = refs
    else:
        (x_ref, xp_ref, g_ref, mix_ref, w_ref, w1_ref, a1_ref, g1_ref,
         o_ref, hw_ref, ha_ref, hg_ref, xs_ref) = refs
    i = pl.program_id(0)
    j = pl.program_id(1)

    @pl.when(j == 0)
    def _():
        g = g_ref[...]
        hn = _rms(x_ref[...], g)
        hp = _rms(xp_ref[...], g)[SUBLANES - 1:SUBLANES, :]
        hp = jnp.where((i * tm) % seq == 0, 0.0, hp)
        row = lax.broadcasted_iota(jnp.int32, hn.shape, 0)
        xx = jnp.where(row == 0, hp, pltpu.roll(hn, 1, 0)) - hn

        def mixed(m):
            return (hn + xx * mix_ref[m:m + 1, :]).astype(BF16)

        xs_ref[0] = mixed(MIX_R)
        xs_ref[1] = mixed(MIX_K)
        xv = mixed(MIX_V)
        xs_ref[2] = xv
        hw_ref[...] = jnp.tanh(_dot(mixed(MIX_W), w1_ref[...]))
        ha_ref[...] = _dot(mixed(MIX_A), a1_ref[...])
        hg_ref[...] = _sigmoid(_dot(mixed(MIX_G), g1_ref[...]))
        if vres:
            hv_ref[...] = _dot(xv, v1_ref[...])

    o_ref[...] = _dot(xs_ref[j // ntile], w_ref[...])


def _rwkv_proj(h, g, mix, wcat, w1, a1, g1, v1, layer, vlayer, seq, tm=512, tn=1024):
    t, d = h.shape
    tm = min(tm, t)
    ntile = d // tn
    vres = v1 is not None

    def lora(w, lyr):
        return pl.BlockSpec((None, d, w.shape[-1]), lambda i, j: (lyr, 0, 0))

    def lora_out(w):
        return pl.BlockSpec((tm, w.shape[-1]), lambda i, j: (i, 0))

    loras = [w1, a1, g1] + ([v1] if vres else [])
    in_specs = [
        pl.BlockSpec((tm, d), lambda i, j: (i, 0)),
        pl.BlockSpec((SUBLANES, d), lambda i, j: (jnp.maximum(i * (tm // SUBLANES) - 1, 0), 0)),
        pl.BlockSpec((1, d), lambda i, j: (0, 0)),
        pl.BlockSpec((None, mix.shape[1], d), lambda i, j: (layer, 0, 0)),
        pl.BlockSpec((None, d, tn), lambda i, j: (layer, 0, j)),
        lora(w1, layer), lora(a1, layer), lora(g1, layer),
    ] + ([lora(v1, vlayer)] if vres else [])
    return pl.pallas_call(
        functools.partial(_rwkv_proj_body, tm=tm, seq=seq, ntile=ntile, vres=vres),
        grid=(t // tm, 3 * ntile),
        in_specs=in_specs,
        out_specs=[pl.BlockSpec((tm, tn), lambda i, j: (i, j))] + [lora_out(w) for w in loras],
        out_shape=[jax.ShapeDtypeStruct((t, 3 * d), F32)]
        + [jax.ShapeDtypeStruct((t, w.shape[-1]), F32) for w in loras],
        scratch_shapes=[pltpu.VMEM((3, tm, d), BF16)],
        compiler_params=_cparams(("parallel", "arbitrary")),
        name="rwkv_proj",
    )(h, h, g, mix, wcat, *loras)


def _rwkv_prep_body(*refs, vres):
    if vres:
        (hw_ref, ha_ref, hg_ref, hv_ref, v_ref, vf_ref, w2_ref, a2_ref, g2_ref, v2_ref,
         w0_ref, a0_ref, v0_ref, lw_ref, a_ref, g_ref, vo_ref) = refs
    else:
        (hw_ref, ha_ref, hg_ref, w2_ref, a2_ref, g2_ref, w0_ref, a0_ref,
         lw_ref, a_ref, g_ref) = refs
    z = -(w0_ref[...] + _dot(hw_ref[...].astype(BF16), w2_ref[...]))
    softplus = jnp.maximum(z, 0.0) + jnp.log(1.0 + jnp.exp(-jnp.abs(z)))
    w = -softplus - 0.5
    lw_ref[...] = -jnp.exp(w)
    a_ref[...] = _sigmoid(a0_ref[...] + _dot(ha_ref[...].astype(BF16), a2_ref[...]))
    g_ref[...] = _dot(hg_ref[...].astype(BF16), g2_ref[...])
    if vres:
        v = v_ref[...]
        mv = _sigmoid(v0_ref[...] + _dot(hv_ref[...].astype(BF16), v2_ref[...]))
        vo_ref[...] = v + (vf_ref[...] - v) * mv


def _rwkv_prep(hw, ha, hg, w2, a2, g2, w0, a0, layer, vres=None, tm=256):
    t = hw.shape[0]
    d = w2.shape[-1]

    def tok(n):
        return pl.BlockSpec((tm, n), lambda i: (i, 0))

    def wgt(k):
        return pl.BlockSpec((None, k, d), lambda i: (layer, 0, 0))

    def vec(idx):
        return pl.BlockSpec((None, 1, d), lambda i: (idx, 0, 0))

    if vres is None:
        args = (hw, ha, hg, w2, a2, g2, w0, a0)
        in_specs = [tok(hw.shape[1]), tok(ha.shape[1]), tok(hg.shape[1]),
                    wgt(w2.shape[1]), wgt(a2.shape[1]), wgt(g2.shape[1]), vec(layer), vec(layer)]
        nout = 3
    else:
        hv, v, vf, v2, v0, vl = vres
        args = (hw, ha, hg, hv, v, vf, w2, a2, g2, v2, w0, a0, v0)
        vcol = pl.BlockSpec((tm, d), lambda i: (i, 2))
        in_specs = [tok(hw.shape[1]), tok(ha.shape[1]), tok(hg.shape[1]), tok(hv.shape[1]), vcol, vcol,
                    wgt(w2.shape[1]), wgt(a2.shape[1]), wgt(g2.shape[1]),
                    pl.BlockSpec((None, v2.shape[1], d), lambda i: (vl, 0, 0)),
                    vec(layer), vec(layer), vec(vl)]
        nout = 4
    return pl.pallas_call(
        functools.partial(_rwkv_prep_body, vres=vres is not None),
        grid=(t // tm,),
        in_specs=in_specs,
        out_specs=[tok(d) for _ in range(nout)],
        out_shape=[jax.ShapeDtypeStruct((t, d), F32) for _ in range(nout)],
        compiler_params=_cparams(("parallel",)),
        name="rwkv_prep",
    )(*args)


RW_CHUNK = 64


RW_PAIRS = 16


def _head_sum(x, m0):
    s0 = jnp.sum(jnp.where(m0, x, 0.0), axis=-1, keepdims=True)
    s1 = jnp.sum(jnp.where(m0, 0.0, x), axis=-1, keepdims=True)
    return jnp.where(m0, s0, s1)


def _rwkv_scan_body(r_ref, k_ref, v_ref, lw_ref, a_ref, g_ref, kk_ref, ka_ref, rk_ref, gg_ref, gb_ref,
                    o_ref, st_ref):
    c = pl.program_id(2)

    @pl.when(c == 0)
    def _():
        st_ref[...] = jnp.zeros_like(st_ref)

    n = RW_CHUNK
    pairs = range(RW_PAIRS)
    heads = [(p, hd) for p in pairs for hd in range(2)]
    cols = [slice(p * LANES, (p + 1) * LANES) for p in pairs]
    lane = lax.broadcasted_iota(jnp.int32, (1, LANES), 1)
    m0 = lane < C_HEAD
    m1 = jnp.logical_not(m0)
    tril, stril = _tri_masks(n)
    tri = jnp.where(tril, 1.0, 0.0).astype(BF16)
    eye = jnp.where(tril & jnp.logical_not(stril), 1.0, 0.0)

    def hsel(x, hd):
        return jnp.where(m0 if hd == 0 else m1, x, jnp.zeros_like(x))

    st = [st_ref[p] for p in pairs]
    r = [r_ref[:, c] for c in cols]
    k = [k_ref[:, c] for c in cols]
    v = [v_ref[:, c] for c in cols]
    asig = [a_ref[:, c] for c in cols]

    cum = [_cumsum_rows(tri, lw_ref[:, c]) for c in cols]
    last = [x[n - 1:n, :] for x in cum]
    kkr = [k[p] * kk_ref[:, cols[p]] for p in pairs]
    kk = [x / jnp.maximum(jnp.sqrt(_head_sum(x * x, m0)), 1e-12) for x in kkr]
    kmod = [k[p] * (1.0 + (asig[p] - 1.0) * ka_ref[:, cols[p]]) for p in pairs]
    b = [kk[p] * asig[p] for p in pairs]
    rt = [(r[p] * jnp.exp(cum[p])).astype(BF16) for p in pairs]
    at = [(-kk[p] * jnp.exp(cum[p] - lw_ref[:, cols[p]])).astype(BF16) for p in pairs]
    einv = [jnp.exp(-x) for x in cum]
    bt = [(b[p] * einv[p]).astype(BF16) for p in pairs]
    kt = [(kmod[p] * einv[p]).astype(BF16) for p in pairs]
    eend = [jnp.exp(last[p] - cum[p]) for p in pairs]
    v_b = [x.astype(BF16) for x in v]
    st_b = [x.astype(BF16) for x in st]

    at_h = [hsel(at[p], hd) for p, hd in heads]
    rt_h = [hsel(rt[p], hd) for p, hd in heads]
    pw = [jnp.where(stril, _dot(at_h[i], bt[p], _NT), 0.0) for i, (p, hd) in enumerate(heads)]
    t_inv = [eye + x for x in pw]
    fill = {}
    fillers = [
        lambda: fill.update(a_ak=[jnp.where(stril, _dot(at_h[i], kt[p], _NT), 0.0).astype(BF16)
                                  for i, (p, hd) in enumerate(heads)]),
        lambda: fill.update(a_rb=[jnp.where(tril, _dot(rt_h[i], bt[p], _NT), 0.0).astype(BF16)
                                  for i, (p, hd) in enumerate(heads)]),
        lambda: fill.update(a_rk=[jnp.where(tril, _dot(rt_h[i], kt[p], _NT), 0.0).astype(BF16)
                                  for i, (p, hd) in enumerate(heads)]),
        lambda: fill.update(av=[_dot(fill["a_ak"][i], v_b[p]) for i, (p, hd) in enumerate(heads)]),
        lambda: fill.update(y_v=[_dot(fill["a_rk"][i], v_b[p]) for i, (p, hd) in enumerate(heads)]),
    ]
    step = 2
    while step < n:
        pw_b = [x.astype(BF16) for x in pw]
        pw = [_dot(x, x) for x in pw_b]
        if fillers:
            fillers.pop(0)()
        t_inv = [t + _dot(t.astype(BF16), x.astype(BF16)) for t, x in zip(t_inv, pw)]
        step *= 2
    for filler in fillers:
        filler()
    t_inv = [x.astype(BF16) for x in t_inv]
    a_rb, av, y_v = fill["a_rb"], fill["av"], fill["y_v"]

    x = [(_dot(at[p], st_b[p], _NT) + jnp.where(m0, av[2 * p], av[2 * p + 1])).astype(BF16) for p in pairs]
    u = [jnp.where(m0, _dot(t_inv[2 * p], x[p]), _dot(t_inv[2 * p + 1], x[p])) for p in pairs]
    u_b = [t.astype(BF16) for t in u]
    y = [_dot(rt[p], st_b[p], _NT)
         + jnp.where(m0, _dot(a_rb[2 * p], u_b[p]) + y_v[2 * p], _dot(a_rb[2 * p + 1], u_b[p]) + y_v[2 * p + 1])
         for p in pairs]

    row = lax.broadcasted_iota(jnp.int32, (LANES, LANES), 0)
    colm = lax.broadcasted_iota(jnp.int32, (LANES, LANES), 1)
    bdiag = (row < C_HEAD) == (colm < C_HEAD)
    for p in pairs:
        uv = jnp.concatenate([u_b[p], v_b[p]], axis=0)
        bk = jnp.concatenate([(b[p] * eend[p]).astype(BF16), (kmod[p] * eend[p]).astype(BF16)], axis=0)
        st_ref[p] = st[p] * jnp.exp(last[p]) + jnp.where(bdiag, _dot(uv, bk, _TN), 0.0)

    inv_n = 1.0 / C_HEAD
    for p in pairs:
        yp = y[p]
        mu = _head_sum(yp, m0) * inv_n
        dy = yp - mu
        var = _head_sum(dy * dy, m0) * inv_n
        yn = dy * lax.rsqrt(var + C_GN_EPS) * gg_ref[:, cols[p]] + gb_ref[:, cols[p]]
        bonus = _head_sum(r[p] * kmod[p] * rk_ref[:, cols[p]], m0) * v[p]
        o_ref[:, cols[p]] = ((yn + bonus) * g_ref[:, cols[p]]).astype(o_ref.dtype)


def _rwkv_scan(rkv, v, lw, a, g, kk, ka, rk, gg, gb, layer, bsz, seq):
    t, d = lw.shape
    nchunk = seq // RW_CHUNK
    width = RW_PAIRS * LANES
    nblk = d // width

    def third(which):
        return pl.BlockSpec((RW_CHUNK, width), lambda b, p, c: (b * nchunk + c, which * nblk + p))

    blk = third(0)
    vec = pl.BlockSpec((None, 1, width), lambda b, p, c: (layer, 0, p))
    r, k = rkv, rkv
    v_spec = blk
    if v is None:
        v, v_spec = rkv, third(2)
    return pl.pallas_call(
        _rwkv_scan_body,
        grid=(bsz, nblk, nchunk),
        in_specs=[blk, third(1), v_spec] + [blk] * 3 + [vec] * 5,
        out_specs=blk,
        out_shape=jax.ShapeDtypeStruct((t, d), BF16),
        scratch_shapes=[pltpu.VMEM((RW_PAIRS, LANES, LANES), F32)],
        compiler_params=_cparams(("parallel", "parallel", "arbitrary")),
        name="rwkv_scan",
    )(r, k, v, lw, a, g, kk, ka, rk, gg, gb)


def _pad_cols(w, n):
    return jnp.pad(w, ((0, 0), (0, 0), (0, n - w.shape[-1])))


def _pad_rows(w, n):
    return jnp.pad(w, ((0, 0), (0, n - w.shape[1]), (0, 0)))


def kernel(x, p, norms, final_norm, ffn_wg, ffn_wu, ffn_wd, ple_wp, ple_wg, e_w_in, e_w_out, a_vnorm, a_ws, a_bs, b_onorm, b_lb_logits, c_mix, c_wr, c_wk, c_wv, c_wo, c_w0, c_w1, c_w2, c_a0, c_a1, c_a2, c_g1, c_g2, c_kk, c_ka, c_rk, c_gn_g, c_gn_b, c_v0, c_v1, c_v2):
    bsz, seq, d = x.shape
    depth = p.shape[0]
    t = bsz * seq
    a_width = a_vnorm.shape[-1]
    b_width = b_onorm.shape[-1]

    bf = lambda w: w.astype(BF16)
    wg_b, wu_b, wd_b = ffn_wg, ffn_wu, ffn_wd
    ple_wp_b, ple_wg_b = bf(ple_wp), bf(ple_wg)
    e_in_b = e_w_in
    e_out_b = e_w_out
    wo_b = c_wo
    rkv_b = bf(jnp.concatenate([c_wr, c_wk, c_wv], axis=-1))
    w1_b = bf(_pad_cols(c_w1, LORA_PAD))
    w2_b = bf(_pad_rows(c_w2, LORA_PAD))
    a1_b = bf(_pad_cols(c_a1, LORA_PAD))
    a2_b = bf(_pad_rows(c_a2, LORA_PAD))
    g1_b, g2_b = bf(c_g1), bf(c_g2)
    v1_b = bf(_pad_cols(c_v1, LORA_PAD))
    v2_b = bf(_pad_rows(c_v2, LORA_PAD))

    vec3 = lambda w: w.reshape(w.shape[0], 1, -1)
    a_vnorm3, b_onorm3 = vec3(a_vnorm), vec3(b_onorm)
    a_bs4 = a_bs.reshape(a_bs.shape + (1,))
    w0_3, a0_3, v0_3 = vec3(c_w0), vec3(c_a0), vec3(c_v0)
    kk3, ka3, rk3, gg3, gb3 = vec3(c_kk), vec3(c_ka), vec3(c_rk), vec3(c_gn_g), vec3(c_gn_b)
    fg = final_norm.reshape(1, d)

    h = x.reshape(t, d)
    p2 = p.reshape(depth, t, p.shape[-1])
    v_first = None
    for i in range(depth):
        j = i // 2
        h = _ffn(h, norms[i, 0].reshape(1, d), wg_b, wu_b, wd_b, i, 0)
        g1n = norms[i, 1].reshape(1, d)
        if i % 2 == 0:
            proj = _nmm(h, g1n, e_in_b, j)
            a_out = _gmlp(proj, a_vnorm3, a_ws, a_bs4, j, a_width)
            b_out = _hgrn2(proj, b_lb_logits, b_onorm3, i, j, bsz, seq, a_width, b_width)
            h = _mm2_res(h, a_out, b_out, e_out_b, j)
        else:
            if j == 0:
                rkv, hw, ha, hg = _rwkv_proj(h, g1n, c_mix, rkv_b, w1_b, a1_b, g1_b, None, j, None, seq)
                lw, a, g = _rwkv_prep(hw, ha, hg, w2_b, a2_b, g2_b, w0_3, a0_3, j)
                v = None
                rkv_first = rkv
            else:
                rkv, hw, ha, hg, hv = _rwkv_proj(h, g1n, c_mix, rkv_b, w1_b, a1_b, g1_b, v1_b, j, j - 1, seq)
                lw, a, g, v = _rwkv_prep(hw, ha, hg, w2_b, a2_b, g2_b, w0_3, a0_3, j,
                                         vres=(hv, rkv, rkv_first, v2_b, v0_3, j - 1))
            y = _rwkv_scan(rkv, v, lw, a, g, kk3, ka3, rk3, gg3, gb3, j, bsz, seq)
            h = _mm_res(h, y, wo_b, j)
        h = _ffn(h, norms[i, 2].reshape(1, d), wg_b, wu_b, wd_b, i, 1)
        h = _ple(h, norms[i, 3].reshape(1, d), p2, ple_wg_b, ple_wp_b, fg, i, final=(i == depth - 1))
    return h.reshape(bsz, seq, d)
```

```python
import functools

import jax
import jax.numpy as jnp
from jax import lax
from jax.experimental import pallas as pl
from jax.experimental.pallas import tpu as pltpu

F32 = jnp.float32
BF16 = jnp.bfloat16

LANES = 128
RMS_EPS = 1e-6
A_CHUNK = 128
B_HEAD = 128
B_MIN_F = 1e-30
C_HEAD = 64
C_GN_EPS = 64e-5
LORA_PAD = 128

VMEM_LIMIT = 56 * 1024 * 1024

_NN = ((1,), (0,))
_NT = ((1,), (1,))
_TN = ((0,), (0,))


def _cparams(sem):
    return pltpu.CompilerParams(dimension_semantics=sem, vmem_limit_bytes=VMEM_LIMIT)


def _rms(x, g, eps=RMS_EPS):
    return x * lax.rsqrt(jnp.mean(x * x, axis=-1, keepdims=True) + eps) * g


def _sigmoid(x):
    return 1.0 / (1.0 + jnp.exp(-x))


def _silu(x):
    return x * _sigmoid(x)


def _gelu_tanh(x):
    return 0.5 * x * (1.0 + jnp.tanh(0.7978845608028654 * (x + 0.044715 * (x * x * x))))


def _dot(a, b, dims=_NN):
    return lax.dot_general(a, b, (dims, ((), ())), preferred_element_type=F32)


def _sp(x, lo):
    hi = x.astype(BF16)
    if not lo:
        return (hi,)
    return (hi, (x - hi.astype(F32)).astype(BF16))


def _pd(a, b, dims=_NN):
    out = _dot(a[0], b[0], dims)
    extra = None
    if len(b) > 1:
        extra = _dot(a[0], b[1], dims)
    if len(a) > 1:
        t = _dot(a[1], b[0], dims)
        extra = t if extra is None else extra + t
    return out if extra is None else out + extra


def _cumsum_rows(tri_bf16, x):
    hi = x.astype(BF16)
    r1 = x - hi.astype(F32)
    mid = r1.astype(BF16)
    lo = (r1 - mid.astype(F32)).astype(BF16)
    return _dot(tri_bf16, hi) + (_dot(tri_bf16, mid) + _dot(tri_bf16, lo))


def _tri_masks(n):
    row = lax.broadcasted_iota(jnp.int32, (n, n), 0)
    col = lax.broadcasted_iota(jnp.int32, (n, n), 1)
    return col <= row, col < row


FFN_DOWN_COLS = 512


def _ffn_body(x_ref, g_ref, wg_ref, wu_ref, wd_ref, o_ref, xn_ref):
    j = pl.program_id(1)

    @pl.when(j == 0)
    def _():
        x = x_ref[...]
        xn_ref[...] = _rms(x, g_ref[...]).astype(BF16)
        o_ref[...] = x

    xn = xn_ref[...]
    gate = _dot(xn, wg_ref[...].astype(BF16))
    up = _dot(xn, wu_ref[...].astype(BF16))
    hid = (0.5 * _silu(gate) * up).astype(BF16)
    for c in range(0, o_ref.shape[1], FFN_DOWN_COLS):
        cols = slice(c, c + FFN_DOWN_COLS)
        o_ref[:, cols] += _dot(hid, wd_ref[:, cols].astype(BF16))


def _ffn(h, g, wg, wu, wd, layer, half, tm=1024, tf=256):
    t, d = h.shape
    f = wg.shape[-1]
    tm = min(tm, t)
    return pl.pallas_call(
        _ffn_body,
        grid=(t // tm, f // tf),
        in_specs=[
            pl.BlockSpec((tm, d), lambda i, j: (i, 0)),
            pl.BlockSpec((1, d), lambda i, j: (0, 0)),
            pl.BlockSpec((None, None, d, tf), lambda i, j: (layer, half, 0, j)),
            pl.BlockSpec((None, None, d, tf), lambda i, j: (layer, half, 0, j)),
            pl.BlockSpec((None, None, tf, d), lambda i, j: (layer, half, j, 0)),
        ],
        out_specs=pl.BlockSpec((tm, d), lambda i, j: (i, 0)),
        out_shape=jax.ShapeDtypeStruct((t, d), F32),
        scratch_shapes=[pltpu.VMEM((tm, d), BF16)],
        compiler_params=_cparams(("parallel", "arbitrary")),
        name="ffn",
    )(h, g, wg, wu, wd)


def _mm_res_body(h_ref, x_ref, w_ref, o_ref):
    o_ref[...] = h_ref[...] + _dot(x_ref[...], w_ref[...].astype(BF16))


def _mm_res(h, x, w, layer, tm=1024, tn=1024):
    t, k = x.shape
    n = w.shape[-1]
    tm = min(tm, t)
    return pl.pallas_call(
        _mm_res_body,
        grid=(t // tm, n // tn),
        in_specs=[
            pl.BlockSpec((tm, tn), lambda i, j: (i, j)),
            pl.BlockSpec((tm, k), lambda i, j: (i, 0)),
            pl.BlockSpec((None, k, tn), lambda i, j: (layer, 0, j)),
        ],
        out_specs=pl.BlockSpec((tm, tn), lambda i, j: (i, j)),
        out_shape=jax.ShapeDtypeStruct((t, n), F32),
        compiler_params=_cparams(("parallel", "parallel")),
        name="mm_res",
    )(h, x, w)


def _mm2_res_body(h_ref, xa_ref, xb_ref, wa_ref, wb_ref, o_ref):
    o_ref[...] = h_ref[...] + (_dot(xa_ref[...], wa_ref[...].astype(BF16))
                               + _dot(xb_ref[...], wb_ref[...].astype(BF16)))


def _mm2_res(h, xa, xb, w, layer, tm=1024, tn=1024):
    t, ka = xa.shape
    kb = xb.shape[1]
    n = w.shape[-1]
    assert ka == kb
    tm = min(tm, t)
    return pl.pallas_call(
        _mm2_res_body,
        grid=(t // tm, n // tn),
        in_specs=[
            pl.BlockSpec((tm, tn), lambda i, j: (i, j)),
            pl.BlockSpec((tm, ka), lambda i, j: (i, 0)),
            pl.BlockSpec((tm, kb), lambda i, j: (i, 0)),
            pl.BlockSpec((None, ka, tn), lambda i, j: (layer, 0, j)),
            pl.BlockSpec((None, kb, tn), lambda i, j: (layer, 1, j)),
        ],
        out_specs=pl.BlockSpec((tm, tn), lambda i, j: (i, j)),
        out_shape=jax.ShapeDtypeStruct((t, n), F32),
        compiler_params=_cparams(("parallel", "parallel")),
        name="mm2_res",
    )(h, xa, xb, w, w)


def _nmm_body(x_ref, g_ref, w_ref, o_ref, xn_ref):
    @pl.when(pl.program_id(1) == 0)
    def _():
        xn_ref[...] = _rms(x_ref[...], g_ref[...]).astype(BF16)

    o_ref[...] = _dot(xn_ref[...], w_ref[...].astype(BF16))


def _nmm(h, g, w, layer, tm=1024, tn=1024):
    t, d = h.shape
    n = w.shape[-1]
    tm = min(tm, t)
    return pl.pallas_call(
        _nmm_body,
        grid=(t // tm, n // tn),
        in_specs=[
            pl.BlockSpec((tm, d), lambda i, j: (i, 0)),
            pl.BlockSpec((1, d), lambda i, j: (0, 0)),
            pl.BlockSpec((None, d, tn), lambda i, j: (layer, 0, j)),
        ],
        out_specs=pl.BlockSpec((tm, tn), lambda i, j: (i, j)),
        out_shape=jax.ShapeDtypeStruct((t, n), F32),
        scratch_shapes=[pltpu.VMEM((tm, d), BF16)],
        compiler_params=_cparams(("parallel", "arbitrary")),
        name="norm_mm",
    )(h, g, w)


def _ple_body(h_ref, g_ref, p_ref, wg_ref, wp_ref, fg_ref, o_ref, *, final):
    h = h_ref[...]
    xn = _rms(h, g_ref[...]).astype(BF16)
    gate = _sigmoid(_dot(xn, wg_ref[...]))
    pe = _dot(p_ref[...].astype(BF16), wp_ref[...])
    out = h + gate * pe
    if final:
        out = _rms(out, fg_ref[...])
    o_ref[...] = out


def _ple(h, g, p, wg, wp, fg, layer, final, tm=512):
    t, d = h.shape
    pd = p.shape[-1]
    return pl.pallas_call(
        functools.partial(_ple_body, final=final),
        grid=(t // tm,),
        in_specs=[
            pl.BlockSpec((tm, d), lambda i: (i, 0)),
            pl.BlockSpec((1, d), lambda i: (0, 0)),
            pl.BlockSpec((None, tm, pd), lambda i: (layer, i, 0)),
            pl.BlockSpec((None, d, d), lambda i: (layer, 0, 0)),
            pl.BlockSpec((None, pd, d), lambda i: (layer, 0, 0)),
            pl.BlockSpec((1, d), lambda i: (0, 0)),
        ],
        out_specs=pl.BlockSpec((tm, d), lambda i: (i, 0)),
        out_shape=jax.ShapeDtypeStruct((t, d), F32),
        compiler_params=_cparams(("parallel",)),
        name="ple",
    )(h, g, p, wg, wp, fg)


GMLP_CHUNKS = 2


def _gmlp_body(u_ref, v_ref, gain_ref, ws_ref, bs_ref, o_ref):
    groups = ws_ref.shape[0]
    tril, _ = _tri_masks(A_CHUNK)
    for g in range(groups):
        w = jnp.where(tril, ws_ref[g], 0.0).astype(BF16)
        bias = bs_ref[g]
        cols = slice(g * A_CHUNK, (g + 1) * A_CHUNK)
        for c in range(GMLP_CHUNKS):
            rows = slice(c * A_CHUNK, (c + 1) * A_CHUNK)
            u = _gelu_tanh(u_ref[rows, cols])
            v = _gelu_tanh(v_ref[rows, cols])
            vg = _rms(v, gain_ref[:, cols])
            s = _dot(w, vg.astype(BF16)) + bias
            o_ref[rows, cols] = (u * s).astype(o_ref.dtype)


def _gmlp(proj, gain, ws, bs, layer, a_width):
    t = proj.shape[0]
    groups = a_width // A_CHUNK
    tm = GMLP_CHUNKS * A_CHUNK
    return pl.pallas_call(
        _gmlp_body,
        grid=(t // tm,),
        in_specs=[
            pl.BlockSpec((tm, a_width), lambda c: (c, 0)),
            pl.BlockSpec((tm, a_width), lambda c: (c, 1)),
            pl.BlockSpec((None, 1, a_width), lambda c: (layer, 0, 0)),
            pl.BlockSpec((None, groups, A_CHUNK, A_CHUNK), lambda c: (layer, 0, 0, 0)),
            pl.BlockSpec((None, groups, A_CHUNK, 1), lambda c: (layer, 0, 0, 0)),
        ],
        out_specs=pl.BlockSpec((tm, a_width), lambda c: (c, 0)),
        out_shape=jax.ShapeDtypeStruct((t, a_width), BF16),
        compiler_params=_cparams(("parallel",)),
        name="gmlp",
    )(proj, proj, gain, ws, bs)


HG_CHUNK = 64
HG_SUB = 16


HG_HEADS = 8
P_HG = False


def _hgrn2_body(q_ref, f_ref, i_ref, g_ref, lbl_ref, on_ref, o_ref, st_ref, *, layer):
    c = pl.program_id(2)

    @pl.when(c == 0)
    def _():
        st_ref[...] = jnp.zeros_like(st_ref)

    logits = lbl_ref[...]
    e = jnp.exp(logits - jnp.max(logits, axis=0, keepdims=True))
    probs = e / jnp.sum(e, axis=0, keepdims=True)
    lb = jnp.zeros((1, logits.shape[1]), F32)
    for r in range(1, layer + 1):
        lb = lb + probs[r:r + 1, :]

    n = HG_CHUNK
    hs = range(HG_HEADS)
    cols = [slice(hd * B_HEAD, (hd + 1) * B_HEAD) for hd in hs]
    tril, _ = _tri_masks(n)
    tri = jnp.where(tril, 1.0, 0.0).astype(BF16)

    st = [st_ref[hd] for hd in hs]
    v = [i_ref[:, c] for c in cols]
    f = [lb[:, c] + (1.0 - lb[:, c]) * _sigmoid(f_ref[:, c]) for c in cols]
    kf = [1.0 - x for x in f]
    qf = [_silu(q_ref[:, c]) for c in cols]
    cum = [_cumsum_rows(tri, jnp.log(jnp.maximum(x, B_MIN_F))) for x in f]
    last = [x[n - 1:n, :] for x in cum]

    st_s = [_sp(x, P_HG) for x in st]
    v_s = [_sp(x, P_HG) for x in v]
    o = [_pd(_sp(qf[h] * jnp.exp(cum[h]), P_HG), st_s[h], _NT) for h in hs]
    for h in hs:
        kend = kf[h] * jnp.exp(last[h] - cum[h])
        st_ref[h] = st[h] * jnp.exp(last[h]) + _pd(v_s[h], _sp(kend, P_HG), _TN)

    nsub = n // HG_SUB
    trow = lax.broadcasted_iota(jnp.int32, (HG_SUB, 1), 0)
    rows = [[] for _ in hs]
    for bi in range(nsub):
        lo, hi = bi * HG_SUB, (bi + 1) * HG_SUB
        acc = [jnp.zeros((HG_SUB, B_HEAD), F32) for _ in hs]
        if bi > 0:
            ref = [cum[h][lo - 1:lo, :] for h in hs]
            qh = [_sp(qf[h][lo:hi] * jnp.exp(cum[h][lo:hi] - ref[h]), P_HG) for h in hs]
            kh = [_sp(kf[h][:lo] * jnp.exp(ref[h] - cum[h][:lo]), P_HG) for h in hs]
            att = [_sp(_pd(qh[h], kh[h], _NT), P_HG) for h in hs]
            acc = [_pd(att[h], tuple(part[:lo] for part in v_s[h])) for h in hs]
        for s in range(HG_SUB):
            for h in hs:
                c_b = cum[h][lo:hi]
                dec = jnp.exp(jnp.minimum(c_b - c_b[s:s + 1, :], 0.0))
                col = jnp.sum(qf[h][lo:hi] * dec * kf[h][lo + s:lo + s + 1, :], axis=-1, keepdims=True)
                col = jnp.where(trow >= s, col, 0.0)
                acc[h] = acc[h] + col * v[h][lo + s:lo + s + 1, :]
        for h in hs:
            rows[h].append(acc[h])

    for h in hs:
        out = o[h] + jnp.concatenate(rows[h], axis=0)
        o_ref[:, cols[h]] = (_rms(out, on_ref[:, cols[h]]) * _silu(g_ref[:, cols[h]])).astype(o_ref.dtype)


def _hgrn2(proj, lb_logits, onorm, layer, elayer, bsz, seq, a_width, b_width):
    t = proj.shape[0]
    heads = b_width // B_HEAD
    hgroups = heads // HG_HEADS
    nchunk = seq // HG_CHUNK
    width = HG_HEADS * B_HEAD
    off = 2 * a_width // width

    def col(which):
        return lambda b, h, c: (b * nchunk + c, off + which * hgroups + h)

    blk = (HG_CHUNK, width)
    depth = lb_logits.shape[0]
    return pl.pallas_call(
        functools.partial(_hgrn2_body, layer=layer),
        grid=(bsz, hgroups, nchunk),
        in_specs=[
            pl.BlockSpec(blk, col(0)),
            pl.BlockSpec(blk, col(1)),
            pl.BlockSpec(blk, col(2)),
            pl.BlockSpec(blk, col(3)),
            pl.BlockSpec((depth, width), lambda b, h, c: (0, h)),
            pl.BlockSpec((None, 1, width), lambda b, h, c: (elayer, 0, h)),
        ],
        out_specs=pl.BlockSpec(blk, lambda b, h, c: (b * nchunk + c, h)),
        out_shape=jax.ShapeDtypeStruct((t, b_width), BF16),
        scratch_shapes=[pltpu.VMEM((HG_HEADS, B_HEAD, B_HEAD), F32)],
        compiler_params=_cparams(("parallel", "parallel", "arbitrary")),
        name="hgrn2",
    )(proj, proj, proj, proj, lb_logits, onorm)


MIX_R, MIX_W, MIX_K, MIX_V, MIX_A, MIX_G = range(6)
RW_DECAY_SCALE = 0.6065306597126334
SUBLANES = 8


def _rwkv_proj_body(*refs, tm, seq, ntile, vres):
    if vres:
        (x_ref, xp_ref, g_ref, mix_ref, w_ref, w1_ref, a1_ref, g1_ref, l_ref, b_ref,
         v1_ref, v2_ref, v0_ref, vf_ref, o_ref, lag_ref, xs_ref, hs_ref, hv_ref) = refs
    else:
        (x_ref, xp_ref, g_ref, mix_ref, w_ref, w1_ref, a1_ref, g1_ref, l_ref, b_ref,
         o_ref, lag_ref, xs_ref, hs_ref) = refs
    i = pl.program_id(0)
    j = pl.program_id(1)

    @pl.when(j == 0)
    def _():
        g = g_ref[...]
        hn = _rms(x_ref[...], g)
        hp = _rms(xp_ref[...], g)[SUBLANES - 1:SUBLANES, :]
        hp = jnp.where((i * tm) % seq == 0, 0.0, hp)
        row = lax.broadcasted_iota(jnp.int32, hn.shape, 0)
        xx = jnp.where(row == 0, hp, pltpu.roll(hn, 1, 0)) - hn

        def mixed(m):
            return (hn + xx * mix_ref[m:m + 1, :]).astype(BF16)

        xs_ref[0] = mixed(MIX_R)
        xs_ref[1] = mixed(MIX_K)
        xv = mixed(MIX_V)
        xs_ref[2] = xv
        hs_ref[0] = jnp.tanh(_dot(mixed(MIX_W), w1_ref[...])).astype(BF16)
        hs_ref[1] = _dot(mixed(MIX_A), a1_ref[...]).astype(BF16)
        hs_ref[2] = _sigmoid(_dot(mixed(MIX_G), g1_ref[...])).astype(BF16)
        if vres:
            hv_ref[...] = _dot(xv, v1_ref[...]).astype(BF16)

    third = j // ntile
    y = _dot(xs_ref[third], w_ref[...])
    z = b_ref[...] + _dot(hs_ref[third], l_ref[...])

    s = _sigmoid(z)
    lag_ref[...] = jnp.where(third == 2, z, s * jnp.where(third == 0, -RW_DECAY_SCALE, 1.0))
    if vres:
        @pl.when(third == 2)
        def _():
            mv = _sigmoid(v0_ref[...] + _dot(hv_ref[...], v2_ref[...]))
            o_ref[...] = y + (vf_ref[...] - y) * mv

        @pl.when(third != 2)
        def _():
            o_ref[...] = y
    else:
        o_ref[...] = y


def _rwkv_proj(h, g, mix, wcat, w1, a1, g1, lcat, bcat, layer, seq, vres=None, tm=512, tn=1024):
    t, d = h.shape
    tm = min(tm, t)
    ntile = d // tn
    rank = w1.shape[-1]

    def lora(lyr):
        return pl.BlockSpec((None, d, rank), lambda i, j: (lyr, 0, 0))

    def vtile(i, j):
        return jnp.maximum(j - 2 * ntile, 0)

    args = [h, h, g, mix, wcat, w1, a1, g1, lcat, bcat]
    in_specs = [
        pl.BlockSpec((tm, d), lambda i, j: (i, 0)),
        pl.BlockSpec((SUBLANES, d), lambda i, j: (jnp.maximum(i * (tm // SUBLANES) - 1, 0), 0)),
        pl.BlockSpec((1, d), lambda i, j: (0, 0)),
        pl.BlockSpec((None, mix.shape[1], d), lambda i, j: (layer, 0, 0)),
        pl.BlockSpec((None, d, tn), lambda i, j: (layer, 0, j)),
        lora(layer), lora(layer), lora(layer),
        pl.BlockSpec((None, rank, tn), lambda i, j: (layer, 0, j)),
        pl.BlockSpec((None, 1, tn), lambda i, j: (layer, 0, j)),
    ]
    scratch = [pltpu.VMEM((3, tm, d), BF16), pltpu.VMEM((3, tm, rank), BF16)]
    if vres is not None:
        v1, v2, v0, rkv_first, vl = vres
        vrank = v1.shape[-1]
        args += [v1, v2, v0, rkv_first]
        in_specs += [
            pl.BlockSpec((None, d, vrank), lambda i, j: (vl, 0, 0)),
            pl.BlockSpec((None, vrank, tn), lambda i, j: (vl, 0, vtile(i, j))),
            pl.BlockSpec((None, 1, tn), lambda i, j: (vl, 0, vtile(i, j))),
            pl.BlockSpec((tm, tn), lambda i, j: (i, 2 * ntile + vtile(i, j))),
        ]
        scratch.append(pltpu.VMEM((tm, vrank), BF16))
    tile = pl.BlockSpec((tm, tn), lambda i, j: (i, j))
    return pl.pallas_call(
        functools.partial(_rwkv_proj_body, tm=tm, seq=seq, ntile=ntile, vres=vres is not None),
        grid=(t // tm, 3 * ntile),
        in_specs=in_specs,
        out_specs=[tile, tile],
        out_shape=[jax.ShapeDtypeStruct((t, 3 * d), F32), jax.ShapeDtypeStruct((t, 3 * d), F32)],
        scratch_shapes=scratch,
        compiler_params=_cparams(("parallel", "arbitrary")),
        name="rwkv_proj",
    )(*args)


RW_CHUNK = 64


RW_PAIRS = 16


def _head_sum(x, m0):
    s0 = jnp.sum(jnp.where(m0, x, 0.0), axis=-1, keepdims=True)
    s1 = jnp.sum(jnp.where(m0, 0.0, x), axis=-1, keepdims=True)
    return jnp.where(m0, s0, s1)


def _rwkv_scan_body(r_ref, k_ref, v_ref, lw_ref, a_ref, g_ref, kk_ref, ka_ref, rk_ref, gg_ref, gb_ref,
                    o_ref, st_ref):
    c = pl.program_id(2)

    @pl.when(c == 0)
    def _():
        st_ref[...] = jnp.zeros_like(st_ref)

    n = RW_CHUNK
    pairs = range(RW_PAIRS)
    heads = [(p, hd) for p in pairs for hd in range(2)]
    cols = [slice(p * LANES, (p + 1) * LANES) for p in pairs]
    lane = lax.broadcasted_iota(jnp.int32, (1, LANES), 1)
    m0 = lane < C_HEAD
    m1 = jnp.logical_not(m0)
    tril, stril = _tri_masks(n)
    tri = jnp.where(tril, 1.0, 0.0).astype(BF16)
    eye = jnp.where(tril & jnp.logical_not(stril), 1.0, 0.0)

    def hsel(x, hd):
        return jnp.where(m0 if hd == 0 else m1, x, jnp.zeros_like(x))

    st = [st_ref[p] for p in pairs]
    r = [r_ref[:, c] for c in cols]
    k = [k_ref[:, c] for c in cols]
    v = [v_ref[:, c] for c in cols]
    asig = [a_ref[:, c] for c in cols]

    cum = [_cumsum_rows(tri, lw_ref[:, c]) for c in cols]
    last = [x[n - 1:n, :] for x in cum]
    kkr = [k[p] * kk_ref[:, cols[p]] for p in pairs]
    kk = [x / jnp.maximum(jnp.sqrt(_head_sum(x * x, m0)), 1e-12) for x in kkr]
    kmod = [k[p] * (1.0 + (asig[p] - 1.0) * ka_ref[:, cols[p]]) for p in pairs]
    b = [kk[p] * asig[p] for p in pairs]
    rt = [(r[p] * jnp.exp(cum[p])).astype(BF16) for p in pairs]
    at = [(-kk[p] * jnp.exp(cum[p] - lw_ref[:, cols[p]])).astype(BF16) for p in pairs]
    einv = [jnp.exp(-x) for x in cum]
    bt = [(b[p] * einv[p]).astype(BF16) for p in pairs]
    kt = [(kmod[p] * einv[p]).astype(BF16) for p in pairs]
    eend = [jnp.exp(last[p] - cum[p]) for p in pairs]
    v_b = [x.astype(BF16) for x in v]
    st_b = [x.astype(BF16) for x in st]

    at_h = [hsel(at[p], hd) for p, hd in heads]
    rt_h = [hsel(rt[p], hd) for p, hd in heads]
    pw = [jnp.where(stril, _dot(at_h[i], bt[p], _NT), 0.0) for i, (p, hd) in enumerate(heads)]
    t_inv = [eye + x for x in pw]
    fill = {}
    fillers = [
        lambda: fill.update(a_ak=[jnp.where(stril, _dot(at_h[i], kt[p], _NT), 0.0).astype(BF16)
                                  for i, (p, hd) in enumerate(heads)]),
        lambda: fill.update(a_rb=[jnp.where(tril, _dot(rt_h[i], bt[p], _NT), 0.0).astype(BF16)
                                  for i, (p, hd) in enumerate(heads)]),
        lambda: fill.update(a_rk=[jnp.where(tril, _dot(rt_h[i], kt[p], _NT), 0.0).astype(BF16)
                                  for i, (p, hd) in enumerate(heads)]),
        lambda: fill.update(av=[_dot(fill["a_ak"][i], v_b[p]) for i, (p, hd) in enumerate(heads)]),
        lambda: fill.update(y_v=[_dot(fill["a_rk"][i], v_b[p]) for i, (p, hd) in enumerate(heads)]),
    ]
    step = 2
    while step < n:
        pw_b = [x.astype(BF16) for x in pw]
        pw = [_dot(x, x) for x in pw_b]
        if fillers:
            fillers.pop(0)()
        t_inv = [t + _dot(t.astype(BF16), x.astype(BF16)) for t, x in zip(t_inv, pw)]
        step *= 2
    for filler in fillers:
        filler()
    t_inv = [x.astype(BF16) for x in t_inv]
    a_rb, av, y_v = fill["a_rb"], fill["av"], fill["y_v"]

    x = [(_dot(at[p], st_b[p], _NT) + jnp.where(m0, av[2 * p], av[2 * p + 1])).astype(BF16) for p in pairs]
    u = [jnp.where(m0, _dot(t_inv[2 * p], x[p]), _dot(t_inv[2 * p + 1], x[p])) for p in pairs]
    u_b = [t.astype(BF16) for t in u]
    y = [_dot(rt[p], st_b[p], _NT)
         + jnp.where(m0, _dot(a_rb[2 * p], u_b[p]) + y_v[2 * p], _dot(a_rb[2 * p + 1], u_b[p]) + y_v[2 * p + 1])
         for p in pairs]

    row = lax.broadcasted_iota(jnp.int32, (LANES, LANES), 0)
    colm = lax.broadcasted_iota(jnp.int32, (LANES, LANES), 1)
    bdiag = (row < C_HEAD) == (colm < C_HEAD)
    for p in pairs:
        uv = jnp.concatenate([u_b[p], v_b[p]], axis=0)
        bk = jnp.concatenate([(b[p] * eend[p]).astype(BF16), (kmod[p] * eend[p]).astype(BF16)], axis=0)
        st_ref[p] = st[p] * jnp.exp(last[p]) + jnp.where(bdiag, _dot(uv, bk, _TN), 0.0)

    inv_n = 1.0 / C_HEAD
    for p in pairs:
        yp = y[p]
        mu = _head_sum(yp, m0) * inv_n
        dy = yp - mu
        var = _head_sum(dy * dy, m0) * inv_n
        yn = dy * lax.rsqrt(var + C_GN_EPS) * gg_ref[:, cols[p]] + gb_ref[:, cols[p]]
        bonus = _head_sum(r[p] * kmod[p] * rk_ref[:, cols[p]], m0) * v[p]
        o_ref[:, cols[p]] = ((yn + bonus) * g_ref[:, cols[p]]).astype(o_ref.dtype)


def _rwkv_scan(rkv, lag, kk, ka, rk, gg, gb, layer, bsz, seq):
    t = rkv.shape[0]
    d = rkv.shape[1] // 3
    nchunk = seq // RW_CHUNK
    width = RW_PAIRS * LANES
    nblk = d // width

    def third(which):
        return pl.BlockSpec((RW_CHUNK, width), lambda b, p, c: (b * nchunk + c, which * nblk + p))

    blk = third(0)
    vec = pl.BlockSpec((None, 1, width), lambda b, p, c: (layer, 0, p))
    r, k, v, lw, a, g = rkv, rkv, rkv, lag, lag, lag
    return pl.pallas_call(
        _rwkv_scan_body,
        grid=(bsz, nblk, nchunk),
        in_specs=[third(0), third(1), third(2)] * 2 + [vec] * 5,
        out_specs=blk,
        out_shape=jax.ShapeDtypeStruct((t, d), BF16),
        scratch_shapes=[pltpu.VMEM((RW_PAIRS, LANES, LANES), F32)],
        compiler_params=_cparams(("parallel", "parallel", "arbitrary")),
        name="rwkv_scan",
    )(r, k, v, lw, a, g, kk, ka, rk, gg, gb)


def _pad_cols(w, n):
    return jnp.pad(w, ((0, 0), (0, 0), (0, n - w.shape[-1])))


def _pad_rows(w, n):
    return jnp.pad(w, ((0, 0), (0, n - w.shape[1]), (0, 0)))


def kernel(x, p, norms, final_norm, ffn_wg, ffn_wu, ffn_wd, ple_wp, ple_wg, e_w_in, e_w_out, a_vnorm, a_ws, a_bs, b_onorm, b_lb_logits, c_mix, c_wr, c_wk, c_wv, c_wo, c_w0, c_w1, c_w2, c_a0, c_a1, c_a2, c_g1, c_g2, c_kk, c_ka, c_rk, c_gn_g, c_gn_b, c_v0, c_v1, c_v2):
    bsz, seq, d = x.shape
    depth = p.shape[0]
    t = bsz * seq
    a_width = a_vnorm.shape[-1]
    b_width = b_onorm.shape[-1]

    bf = lambda w: w.astype(BF16)
    wg_b, wu_b, wd_b = ffn_wg, ffn_wu, ffn_wd
    ple_wp_b, ple_wg_b = bf(ple_wp), bf(ple_wg)
    e_in_b = e_w_in
    e_out_b = e_w_out
    wo_b = c_wo
    rkv_b = bf(jnp.concatenate([c_wr, c_wk, c_wv], axis=-1))
    rank = c_g1.shape[-1]
    w1_b = bf(_pad_cols(c_w1, rank))
    a1_b = bf(_pad_cols(c_a1, rank))
    g1_b = bf(c_g1)
    lag_b = bf(jnp.concatenate([_pad_rows(c_w2, rank), _pad_rows(c_a2, rank), c_g2], axis=-1))
    bias_b = jnp.concatenate([c_w0, c_a0, jnp.zeros_like(c_w0)], axis=-1).reshape(c_w0.shape[0], 1, 3 * d)
    v1_b = bf(_pad_cols(c_v1, LORA_PAD))
    v2_b = bf(_pad_rows(c_v2, LORA_PAD))

    vec3 = lambda w: w.reshape(w.shape[0], 1, -1)
    a_vnorm3, b_onorm3 = vec3(a_vnorm), vec3(b_onorm)
    a_bs4 = a_bs.reshape(a_bs.shape + (1,))
    v0_3 = vec3(c_v0)
    kk3, ka3, rk3, gg3, gb3 = vec3(c_kk), vec3(c_ka), vec3(c_rk), vec3(c_gn_g), vec3(c_gn_b)
    fg = final_norm.reshape(1, d)

    h = x.reshape(t, d)
    p2 = p.reshape(depth, t, p.shape[-1])
    rkv_first = None
    for i in range(depth):
        j = i // 2
        h = _ffn(h, norms[i, 0].reshape(1, d), wg_b, wu_b, wd_b, i, 0)
        g1n = norms[i, 1].reshape(1, d)
        if i % 2 == 0:
            proj = _nmm(h, g1n, e_in_b, j)
            a_out = _gmlp(proj, a_vnorm3, a_ws, a_bs4, j, a_width)
            b_out = _hgrn2(proj, b_lb_logits, b_onorm3, i, j, bsz, seq, a_width, b_width)
            h = _mm2_res(h, a_out, b_out, e_out_b, j)
        else:
            vres = None if j == 0 else (v1_b, v2_b, v0_3, rkv_first, j - 1)
            rkv, lag = _rwkv_proj(h, g1n, c_mix, rkv_b, w1_b, a1_b, g1_b, lag_b, bias_b, j, seq, vres=vres)
            if j == 0:
                rkv_first = rkv
            y = _rwkv_scan(rkv, lag, kk3, ka3, rk3, gg3, gb3, j, bsz, seq)
            h = _mm_res(h, y, wo_b, j)
        h = _ffn(h, norms[i, 2].reshape(1, d), wg_b, wu_b, wd_b, i, 1)
        h = _ple(h, norms[i, 3].reshape(1, d), p2, ple_wg_b, ple_wp_b, fg, i, final=(i == depth - 1))
    return h.reshape(bsz, seq, d)
```

```python
import functools

import jax
import jax.numpy as jnp
from jax import lax
from jax.experimental import pallas as pl
from jax.experimental.pallas import tpu as pltpu

F32 = jnp.float32
BF16 = jnp.bfloat16

LANES = 128
RMS_EPS = 1e-6
A_CHUNK = 128
B_HEAD = 128
B_MIN_F = 1e-30
C_HEAD = 64
C_GN_EPS = 64e-5
LORA_PAD = 128

VMEM_LIMIT = 56 * 1024 * 1024

_NN = ((1,), (0,))
_NT = ((1,), (1,))
_TN = ((0,), (0,))


def _cparams(sem):
    return pltpu.CompilerParams(dimension_semantics=sem, vmem_limit_bytes=VMEM_LIMIT)


def _rms(x, g, eps=RMS_EPS):
    return x * lax.rsqrt(jnp.mean(x * x, axis=-1, keepdims=True) + eps) * g


def _sigmoid(x):
    return 1.0 / (1.0 + jnp.exp(-x))


def _silu(x):
    return x * _sigmoid(x)


def _gelu_tanh(x):
    return 0.5 * x * (1.0 + jnp.tanh(0.7978845608028654 * (x + 0.044715 * (x * x * x))))


def _dot(a, b, dims=_NN):
    return lax.dot_general(a, b, (dims, ((), ())), preferred_element_type=F32)


def _sp(x, lo):
    hi = x.astype(BF16)
    if not lo:
        return (hi,)
    return (hi, (x - hi.astype(F32)).astype(BF16))


def _pd(a, b, dims=_NN):
    out = _dot(a[0], b[0], dims)
    extra = None
    if len(b) > 1:
        extra = _dot(a[0], b[1], dims)
    if len(a) > 1:
        t = _dot(a[1], b[0], dims)
        extra = t if extra is None else extra + t
    return out if extra is None else out + extra


def _cumsum_rows(tri_bf16, x):
    hi = x.astype(BF16)
    r1 = x - hi.astype(F32)
    mid = r1.astype(BF16)
    lo = (r1 - mid.astype(F32)).astype(BF16)
    return _dot(tri_bf16, hi) + (_dot(tri_bf16, mid) + _dot(tri_bf16, lo))


def _tri_masks(n):
    row = lax.broadcasted_iota(jnp.int32, (n, n), 0)
    col = lax.broadcasted_iota(jnp.int32, (n, n), 1)
    return col <= row, col < row


FFN_DOWN_COLS = 512


def _ffn_body(x_ref, g_ref, wg_ref, wu_ref, wd_ref, o_ref, xn_ref):
    j = pl.program_id(1)

    @pl.when(j == 0)
    def _():
        x = x_ref[...]
        xn_ref[...] = _rms(x, g_ref[...]).astype(BF16)
        o_ref[...] = x

    xn = xn_ref[...]
    gate = _dot(xn, wg_ref[...].astype(BF16))
    up = _dot(xn, wu_ref[...].astype(BF16))
    hid = (0.5 * _silu(gate) * up).astype(BF16)
    for c in range(0, o_ref.shape[1], FFN_DOWN_COLS):
        cols = slice(c, c + FFN_DOWN_COLS)
        o_ref[:, cols] += _dot(hid, wd_ref[:, cols].astype(BF16))


def _ffn(h, g, wg, wu, wd, layer, half, tm=1024, tf=256):
    t, d = h.shape
    f = wg.shape[-1]
    tm = min(tm, t)
    return pl.pallas_call(
        _ffn_body,
        grid=(t // tm, f // tf),
        in_specs=[
            pl.BlockSpec((tm, d), lambda i, j: (i, 0)),
            pl.BlockSpec((1, d), lambda i, j: (0, 0)),
            pl.BlockSpec((None, None, d, tf), lambda i, j: (layer, half, 0, j)),
            pl.BlockSpec((None, None, d, tf), lambda i, j: (layer, half, 0, j)),
            pl.BlockSpec((None, None, tf, d), lambda i, j: (layer, half, j, 0)),
        ],
        out_specs=pl.BlockSpec((tm, d), lambda i, j: (i, 0)),
        out_shape=jax.ShapeDtypeStruct((t, d), F32),
        scratch_shapes=[pltpu.VMEM((tm, d), BF16)],
        compiler_params=_cparams(("parallel", "arbitrary")),
        name="ffn",
    )(h, g, wg, wu, wd)


def _mm_res_body(h_ref, x_ref, w_ref, o_ref):
    o_ref[...] = h_ref[...] + _dot(x_ref[...], w_ref[...].astype(BF16))


def _mm_res(h, x, w, layer, tm=1024, tn=1024):
    t, k = x.shape
    n = w.shape[-1]
    tm = min(tm, t)
    return pl.pallas_call(
        _mm_res_body,
        grid=(t // tm, n // tn),
        in_specs=[
            pl.BlockSpec((tm, tn), lambda i, j: (i, j)),
            pl.BlockSpec((tm, k), lambda i, j: (i, 0)),
            pl.BlockSpec((None, k, tn), lambda i, j: (layer, 0, j)),
        ],
        out_specs=pl.BlockSpec((tm, tn), lambda i, j: (i, j)),
        out_shape=jax.ShapeDtypeStruct((t, n), F32),
        compiler_params=_cparams(("parallel", "parallel")),
        name="mm_res",
    )(h, x, w)


def _mm2_res_body(h_ref, xa_ref, xb_ref, wa_ref, wb_ref, o_ref):
    o_ref[...] = h_ref[...] + (_dot(xa_ref[...], wa_ref[...].astype(BF16))
                               + _dot(xb_ref[...], wb_ref[...].astype(BF16)))


def _mm2_res(h, xa, xb, w, layer, tm=1024, tn=1024):
    t, ka = xa.shape
    kb = xb.shape[1]
    n = w.shape[-1]
    assert ka == kb
    tm = min(tm, t)
    return pl.pallas_call(
        _mm2_res_body,
        grid=(t // tm, n // tn),
        in_specs=[
            pl.BlockSpec((tm, tn), lambda i, j: (i, j)),
            pl.BlockSpec((tm, ka), lambda i, j: (i, 0)),
            pl.BlockSpec((tm, kb), lambda i, j: (i, 0)),
            pl.BlockSpec((None, ka, tn), lambda i, j: (layer, 0, j)),
            pl.BlockSpec((None, kb, tn), lambda i, j: (layer, 1, j)),
        ],
        out_specs=pl.BlockSpec((tm, tn), lambda i, j: (i, j)),
        out_shape=jax.ShapeDtypeStruct((t, n), F32),
        compiler_params=_cparams(("parallel", "parallel")),
        name="mm2_res",
    )(h, xa, xb, w, w)


def _nmm_body(x_ref, g_ref, w_ref, o_ref, xn_ref):
    @pl.when(pl.program_id(1) == 0)
    def _():
        xn_ref[...] = _rms(x_ref[...], g_ref[...]).astype(BF16)

    o_ref[...] = _dot(xn_ref[...], w_ref[...].astype(BF16))


def _nmm(h, g, w, layer, tm=1024, tn=1024):
    t, d = h.shape
    n = w.shape[-1]
    tm = min(tm, t)
    return pl.pallas_call(
        _nmm_body,
        grid=(t // tm, n // tn),
        in_specs=[
            pl.BlockSpec((tm, d), lambda i, j: (i, 0)),
            pl.BlockSpec((1, d), lambda i, j: (0, 0)),
            pl.BlockSpec((None, d, tn), lambda i, j: (layer, 0, j)),
        ],
        out_specs=pl.BlockSpec((tm, tn), lambda i, j: (i, j)),
        out_shape=jax.ShapeDtypeStruct((t, n), F32),
        scratch_shapes=[pltpu.VMEM((tm, d), BF16)],
        compiler_params=_cparams(("parallel", "arbitrary")),
        name="norm_mm",
    )(h, g, w)


def _ple_body(h_ref, g_ref, p_ref, wg_ref, wp_ref, fg_ref, o_ref, *, final):
    h = h_ref[...]
    xn = _rms(h, g_ref[...]).astype(BF16)
    gate = _sigmoid(_dot(xn, wg_ref[...]))
    pe = _dot(p_ref[...].astype(BF16), wp_ref[...])
    out = h + gate * pe
    if final:
        out = _rms(out, fg_ref[...])
    o_ref[...] = out


def _ple(h, g, p, wg, wp, fg, layer, final, tm=512):
    t, d = h.shape
    pd = p.shape[-1]
    return pl.pallas_call(
        functools.partial(_ple_body, final=final),
        grid=(t // tm,),
        in_specs=[
            pl.BlockSpec((tm, d), lambda i: (i, 0)),
            pl.BlockSpec((1, d), lambda i: (0, 0)),
            pl.BlockSpec((None, tm, pd), lambda i: (layer, i, 0)),
            pl.BlockSpec((None, d, d), lambda i: (layer, 0, 0)),
            pl.BlockSpec((None, pd, d), lambda i: (layer, 0, 0)),
            pl.BlockSpec((1, d), lambda i: (0, 0)),
        ],
        out_specs=pl.BlockSpec((tm, d), lambda i: (i, 0)),
        out_shape=jax.ShapeDtypeStruct((t, d), F32),
        compiler_params=_cparams(("parallel",)),
        name="ple",
    )(h, g, p, wg, wp, fg)


GMLP_CHUNKS = 2


def _gmlp_body(u_ref, v_ref, gain_ref, ws_ref, bs_ref, o_ref):
    groups = ws_ref.shape[0]
    tril, _ = _tri_masks(A_CHUNK)
    for g in range(groups):
        w = jnp.where(tril, ws_ref[g], 0.0).astype(BF16)
        bias = bs_ref[g]
        cols = slice(g * A_CHUNK, (g + 1) * A_CHUNK)
        for c in range(GMLP_CHUNKS):
            rows = slice(c * A_CHUNK, (c + 1) * A_CHUNK)
            u = _gelu_tanh(u_ref[rows, cols])
            v = _gelu_tanh(v_ref[rows, cols])
            vg = _rms(v, gain_ref[:, cols])
            s = _dot(w, vg.astype(BF16)) + bias
            o_ref[rows, cols] = (u * s).astype(o_ref.dtype)


def _gmlp(proj, gain, ws, bs, layer, a_width):
    t = proj.shape[0]
    groups = a_width // A_CHUNK
    tm = GMLP_CHUNKS * A_CHUNK
    return pl.pallas_call(
        _gmlp_body,
        grid=(t // tm,),
        in_specs=[
            pl.BlockSpec((tm, a_width), lambda c: (c, 0)),
            pl.BlockSpec((tm, a_width), lambda c: (c, 1)),
            pl.BlockSpec((None, 1, a_width), lambda c: (layer, 0, 0)),
            pl.BlockSpec((None, groups, A_CHUNK, A_CHUNK), lambda c: (layer, 0, 0, 0)),
            pl.BlockSpec((None, groups, A_CHUNK, 1), lambda c: (layer, 0, 0, 0)),
        ],
        out_specs=pl.BlockSpec((tm, a_width), lambda c: (c, 0)),
        out_shape=jax.ShapeDtypeStruct((t, a_width), BF16),
        compiler_params=_cparams(("parallel",)),
        name="gmlp",
    )(proj, proj, gain, ws, bs)


HG_CHUNK = 64
HG_SUB = 16


HG_HEADS = 8
P_HG = False


def _hgrn2_body(q_ref, f_ref, i_ref, g_ref, lbl_ref, on_ref, o_ref, st_ref, *, layer):
    c = pl.program_id(2)

    @pl.when(c == 0)
    def _():
        st_ref[...] = jnp.zeros_like(st_ref)

    logits = lbl_ref[...]
    e = jnp.exp(logits - jnp.max(logits, axis=0, keepdims=True))
    probs = e / jnp.sum(e, axis=0, keepdims=True)
    lb = jnp.zeros((1, logits.shape[1]), F32)
    for r in range(1, layer + 1):
        lb = lb + probs[r:r + 1, :]

    n = HG_CHUNK
    hs = range(HG_HEADS)
    cols = [slice(hd * B_HEAD, (hd + 1) * B_HEAD) for hd in hs]
    tril, _ = _tri_masks(n)
    tri = jnp.where(tril, 1.0, 0.0).astype(BF16)

    st = [st_ref[hd] for hd in hs]
    v = [i_ref[:, c] for c in cols]
    f = [lb[:, c] + (1.0 - lb[:, c]) * _sigmoid(f_ref[:, c]) for c in cols]
    kf = [1.0 - x for x in f]
    qf = [_silu(q_ref[:, c]) for c in cols]
    cum = [_cumsum_rows(tri, jnp.log(jnp.maximum(x, B_MIN_F))) for x in f]
    last = [x[n - 1:n, :] for x in cum]

    st_s = [_sp(x, P_HG) for x in st]
    v_s = [_sp(x, P_HG) for x in v]
    o = [_pd(_sp(qf[h] * jnp.exp(cum[h]), P_HG), st_s[h], _NT) for h in hs]
    for h in hs:
        kend = kf[h] * jnp.exp(last[h] - cum[h])
        st_ref[h] = st[h] * jnp.exp(last[h]) + _pd(v_s[h], _sp(kend, P_HG), _TN)

    nsub = n // HG_SUB
    trow = lax.broadcasted_iota(jnp.int32, (HG_SUB, 1), 0)
    rows = [[] for _ in hs]
    for bi in range(nsub):
        lo, hi = bi * HG_SUB, (bi + 1) * HG_SUB
        acc = [jnp.zeros((HG_SUB, B_HEAD), F32) for _ in hs]
        if bi > 0:
            ref = [cum[h][lo - 1:lo, :] for h in hs]
            qh = [_sp(qf[h][lo:hi] * jnp.exp(cum[h][lo:hi] - ref[h]), P_HG) for h in hs]
            kh = [_sp(kf[h][:lo] * jnp.exp(ref[h] - cum[h][:lo]), P_HG) for h in hs]
            att = [_sp(_pd(qh[h], kh[h], _NT), P_HG) for h in hs]
            acc = [_pd(att[h], tuple(part[:lo] for part in v_s[h])) for h in hs]
        for s in range(HG_SUB):
            for h in hs:
                c_b = cum[h][lo:hi]
                dec = jnp.exp(jnp.minimum(c_b - c_b[s:s + 1, :], 0.0))
                col = jnp.sum(qf[h][lo:hi] * dec * kf[h][lo + s:lo + s + 1, :], axis=-1, keepdims=True)
                col = jnp.where(trow >= s, col, 0.0)
                acc[h] = acc[h] + col * v[h][lo + s:lo + s + 1, :]
        for h in hs:
            rows[h].append(acc[h])

    for h in hs:
        out = o[h] + jnp.concatenate(rows[h], axis=0)
        o_ref[:, cols[h]] = (_rms(out, on_ref[:, cols[h]]) * _silu(g_ref[:, cols[h]])).astype(o_ref.dtype)


def _hgrn2(proj, lb_logits, onorm, layer, elayer, bsz, seq, a_width, b_width):
    t = proj.shape[0]
    heads = b_width // B_HEAD
    hgroups = heads // HG_HEADS
    nchunk = seq // HG_CHUNK
    width = HG_HEADS * B_HEAD
    off = 2 * a_width // width

    def col(which):
        return lambda b, h, c: (b * nchunk + c, off + which * hgroups + h)

    blk = (HG_CHUNK, width)
    depth = lb_logits.shape[0]
    return pl.pallas_call(
        functools.partial(_hgrn2_body, layer=layer),
        grid=(bsz, hgroups, nchunk),
        in_specs=[
            pl.BlockSpec(blk, col(0)),
            pl.BlockSpec(blk, col(1)),
            pl.BlockSpec(blk, col(2)),
            pl.BlockSpec(blk, col(3)),
            pl.BlockSpec((depth, width), lambda b, h, c: (0, h)),
            pl.BlockSpec((None, 1, width), lambda b, h, c: (elayer, 0, h)),
        ],
        out_specs=pl.BlockSpec(blk, lambda b, h, c: (b * nchunk + c, h)),
        out_shape=jax.ShapeDtypeStruct((t, b_width), BF16),
        scratch_shapes=[pltpu.VMEM((HG_HEADS, B_HEAD, B_HEAD), F32)],
        compiler_params=_cparams(("parallel", "parallel", "arbitrary")),
        name="hgrn2",
    )(proj, proj, proj, proj, lb_logits, onorm)


MIX_R, MIX_W, MIX_K, MIX_V, MIX_A, MIX_G = range(6)
RW_DECAY_SCALE = 0.6065306597126334
PROJ_SUB = 128
SUBLANES = 8


def _rwkv_proj_body(*refs, tm, seq, ntile, vres):
    if vres:
        (x_ref, xp_ref, g_ref, mix_ref, w_ref, w1_ref, a1_ref, g1_ref, l_ref, b_ref,
         v1_ref, v2_ref, v0_ref, vf_ref, o_ref, lag_ref, xs_ref, hs_ref, hv_ref) = refs
    else:
        (x_ref, xp_ref, g_ref, mix_ref, w_ref, w1_ref, a1_ref, g1_ref, l_ref, b_ref,
         o_ref, lag_ref, xs_ref, hs_ref) = refs
    i = pl.program_id(0)
    j = pl.program_id(1)

    @pl.when(j == 0)
    def _():
        g = g_ref[...]
        for r0 in range(0, tm, PROJ_SUB):
            rows = slice(r0, r0 + PROJ_SUB)
            hn = _rms(x_ref[rows, :], g)
            if r0 == 0:
                hp = _rms(xp_ref[...], g)[SUBLANES - 1:SUBLANES, :]
                hp = jnp.where((i * tm) % seq == 0, 0.0, hp)
            else:
                hp = _rms(x_ref[r0 - SUBLANES:r0, :], g)[SUBLANES - 1:SUBLANES, :]
            row = lax.broadcasted_iota(jnp.int32, hn.shape, 0)
            xx = jnp.where(row == 0, hp, pltpu.roll(hn, 1, 0)) - hn

            def mixed(m):
                return (hn + xx * mix_ref[m:m + 1, :]).astype(BF16)

            xs_ref[0, rows, :] = mixed(MIX_R)
            xs_ref[1, rows, :] = mixed(MIX_K)
            xv = mixed(MIX_V)
            xs_ref[2, rows, :] = xv
            hs_ref[0, rows, :] = jnp.tanh(_dot(mixed(MIX_W), w1_ref[...])).astype(BF16)
            hs_ref[1, rows, :] = _dot(mixed(MIX_A), a1_ref[...]).astype(BF16)
            hs_ref[2, rows, :] = _sigmoid(_dot(mixed(MIX_G), g1_ref[...])).astype(BF16)
            if vres:
                hv_ref[rows, :] = _dot(xv, v1_ref[...]).astype(BF16)

    third = j // ntile
    y = _dot(xs_ref[third], w_ref[...])
    z = b_ref[...] + _dot(hs_ref[third], l_ref[...])

    s = _sigmoid(z)
    lag_ref[...] = jnp.where(third == 2, z, s * jnp.where(third == 0, -RW_DECAY_SCALE, 1.0))
    if vres:
        @pl.when(third == 2)
        def _():
            mv = _sigmoid(v0_ref[...] + _dot(hv_ref[...], v2_ref[...]))
            o_ref[...] = y + (vf_ref[...] - y) * mv

        @pl.when(third != 2)
        def _():
            o_ref[...] = y
    else:
        o_ref[...] = y


def _rwkv_proj(h, g, mix, wcat, w1, a1, g1, lcat, bcat, layer, seq, vres=None, tm=1024, tn=512):
    t, d = h.shape
    tm = min(tm, t)
    ntile = d // tn
    rank = w1.shape[-1]

    def lora(lyr):
        return pl.BlockSpec((None, d, rank), lambda i, j: (lyr, 0, 0))

    def vtile(i, j):
        return jnp.maximum(j - 2 * ntile, 0)

    args = [h, h, g, mix, wcat, w1, a1, g1, lcat, bcat]
    in_specs = [
        pl.BlockSpec((tm, d), lambda i, j: (i, 0)),
        pl.BlockSpec((SUBLANES, d), lambda i, j: (jnp.maximum(i * (tm // SUBLANES) - 1, 0), 0)),
        pl.BlockSpec((1, d), lambda i, j: (0, 0)),
        pl.BlockSpec((None, mix.shape[1], d), lambda i, j: (layer, 0, 0)),
        pl.BlockSpec((None, d, tn), lambda i, j: (layer, 0, j)),
        lora(layer), lora(layer), lora(layer),
        pl.BlockSpec((None, rank, tn), lambda i, j: (layer, 0, j)),
        pl.BlockSpec((None, 1, tn), lambda i, j: (layer, 0, j)),
    ]
    scratch = [pltpu.VMEM((3, tm, d), BF16), pltpu.VMEM((3, tm, rank), BF16)]
    if vres is not None:
        v1, v2, v0, rkv_first, vl = vres
        vrank = v1.shape[-1]
        args += [v1, v2, v0, rkv_first]
        in_specs += [
            pl.BlockSpec((None, d, vrank), lambda i, j: (vl, 0, 0)),
            pl.BlockSpec((None, vrank, tn), lambda i, j: (vl, 0, vtile(i, j))),
            pl.BlockSpec((None, 1, tn), lambda i, j: (vl, 0, vtile(i, j))),
            pl.BlockSpec((tm, tn), lambda i, j: (i, 2 * ntile + vtile(i, j))),
        ]
        scratch.append(pltpu.VMEM((tm, vrank), BF16))
    tile = pl.BlockSpec((tm, tn), lambda i, j: (i, j))
    return pl.pallas_call(
        functools.partial(_rwkv_proj_body, tm=tm, seq=seq, ntile=ntile, vres=vres is not None),
        grid=(t // tm, 3 * ntile),
        in_specs=in_specs,
        out_specs=[tile, tile],
        out_shape=[jax.ShapeDtypeStruct((t, 3 * d), F32), jax.ShapeDtypeStruct((t, 3 * d), F32)],
        scratch_shapes=scratch,
        compiler_params=_cparams(("parallel", "arbitrary")),
        name="rwkv_proj",
    )(*args)


RW_CHUNK = 64


RW_PAIRS = 16


def _head_sum(x, m0):
    s0 = jnp.sum(jnp.where(m0, x, 0.0), axis=-1, keepdims=True)
    s1 = jnp.sum(jnp.where(m0, 0.0, x), axis=-1, keepdims=True)
    return jnp.where(m0, s0, s1)


def _rwkv_scan_body(r_ref, k_ref, v_ref, lw_ref, a_ref, g_ref, kk_ref, ka_ref, rk_ref, gg_ref, gb_ref,
                    o_ref, st_ref):
    c = pl.program_id(2)

    @pl.when(c == 0)
    def _():
        st_ref[...] = jnp.zeros_like(st_ref)

    n = RW_CHUNK
    pairs = range(RW_PAIRS)
    heads = [(p, hd) for p in pairs for hd in range(2)]
    cols = [slice(p * LANES, (p + 1) * LANES) for p in pairs]
    lane = lax.broadcasted_iota(jnp.int32, (1, LANES), 1)
    m0 = lane < C_HEAD
    m1 = jnp.logical_not(m0)
    tril, stril = _tri_masks(n)
    tri = jnp.where(tril, 1.0, 0.0).astype(BF16)
    eye = jnp.where(tril & jnp.logical_not(stril), 1.0, 0.0)

    def hsel(x, hd):
        return jnp.where(m0 if hd == 0 else m1, x, jnp.zeros_like(x))

    st = [st_ref[p] for p in pairs]
    r = [r_ref[:, c] for c in cols]
    k = [k_ref[:, c] for c in cols]
    v = [v_ref[:, c] for c in cols]
    asig = [a_ref[:, c] for c in cols]

    cum = [_cumsum_rows(tri, lw_ref[:, c]) for c in cols]
    last = [x[n - 1:n, :] for x in cum]
    kkr = [k[p] * kk_ref[:, cols[p]] for p in pairs]
    kk = [x / jnp.maximum(jnp.sqrt(_head_sum(x * x, m0)), 1e-12) for x in kkr]
    kmod = [k[p] * (1.0 + (asig[p] - 1.0) * ka_ref[:, cols[p]]) for p in pairs]
    b = [kk[p] * asig[p] for p in pairs]
    rt = [(r[p] * jnp.exp(cum[p])).astype(BF16) for p in pairs]
    at = [(-kk[p] * jnp.exp(cum[p] - lw_ref[:, cols[p]])).astype(BF16) for p in pairs]
    einv = [jnp.exp(-x) for x in cum]
    bt = [(b[p] * einv[p]).astype(BF16) for p in pairs]
    kt = [(kmod[p] * einv[p]).astype(BF16) for p in pairs]
    eend = [jnp.exp(last[p] - cum[p]) for p in pairs]
    v_b = [x.astype(BF16) for x in v]
    st_b = [x.astype(BF16) for x in st]

    at_h = [hsel(at[p], hd) for p, hd in heads]
    rt_h = [hsel(rt[p], hd) for p, hd in heads]
    pw = [jnp.where(stril, _dot(at_h[i], bt[p], _NT), 0.0) for i, (p, hd) in enumerate(heads)]
    t_inv = [eye + x for x in pw]
    fill = {}
    fillers = [
        lambda: fill.update(a_ak=[jnp.where(stril, _dot(at_h[i], kt[p], _NT), 0.0).astype(BF16)
                                  for i, (p, hd) in enumerate(heads)]),
        lambda: fill.update(a_rb=[jnp.where(tril, _dot(rt_h[i], bt[p], _NT), 0.0).astype(BF16)
                                  for i, (p, hd) in enumerate(heads)]),
        lambda: fill.update(a_rk=[jnp.where(tril, _dot(rt_h[i], kt[p], _NT), 0.0).astype(BF16)
                                  for i, (p, hd) in enumerate(heads)]),
        lambda: fill.update(av=[_dot(fill["a_ak"][i], v_b[p]) for i, (p, hd) in enumerate(heads)]),
        lambda: fill.update(y_v=[_dot(fill["a_rk"][i], v_b[p]) for i, (p, hd) in enumerate(heads)]),
    ]
    step = 2
    while step < n:
        pw_b = [x.astype(BF16) for x in pw]
        pw = [_dot(x, x) for x in pw_b]
        if fillers:
            fillers.pop(0)()
        t_inv = [t + _dot(t.astype(BF16), x.astype(BF16)) for t, x in zip(t_inv, pw)]
        step *= 2
    for filler in fillers:
        filler()
    t_inv = [x.astype(BF16) for x in t_inv]
    a_rb, av, y_v = fill["a_rb"], fill["av"], fill["y_v"]

    x = [(_dot(at[p], st_b[p], _NT) + jnp.where(m0, av[2 * p], av[2 * p + 1])).astype(BF16) for p in pairs]
    u = [jnp.where(m0, _dot(t_inv[2 * p], x[p]), _dot(t_inv[2 * p + 1], x[p])) for p in pairs]
    u_b = [t.astype(BF16) for t in u]
    y = [_dot(rt[p], st_b[p], _NT)
         + jnp.where(m0, _dot(a_rb[2 * p], u_b[p]) + y_v[2 * p], _dot(a_rb[2 * p + 1], u_b[p]) + y_v[2 * p + 1])
         for p in pairs]

    row = lax.broadcasted_iota(jnp.int32, (LANES, LANES), 0)
    colm = lax.broadcasted_iota(jnp.int32, (LANES, LANES), 1)
    bdiag = (row < C_HEAD) == (colm < C_HEAD)
    for p in pairs:
        uv = jnp.concatenate([u_b[p], v_b[p]], axis=0)
        bk = jnp.concatenate([(b[p] * eend[p]).astype(BF16), (kmod[p] * eend[p]).astype(BF16)], axis=0)
        st_ref[p] = st[p] * jnp.exp(last[p]) + jnp.where(bdiag, _dot(uv, bk, _TN), 0.0)

    inv_n = 1.0 / C_HEAD
    for p in pairs:
        yp = y[p]
        mu = _head_sum(yp, m0) * inv_n
        dy = yp - mu
        var = _head_sum(dy * dy, m0) * inv_n
        yn = dy * lax.rsqrt(var + C_GN_EPS) * gg_ref[:, cols[p]] + gb_ref[:, cols[p]]
        bonus = _head_sum(r[p] * kmod[p] * rk_ref[:, cols[p]], m0) * v[p]
        o_ref[:, cols[p]] = ((yn + bonus) * g_ref[:, cols[p]]).astype(o_ref.dtype)


def _rwkv_scan(rkv, lag, kk, ka, rk, gg, gb, layer, bsz, seq):
    t = rkv.shape[0]
    d = rkv.shape[1] // 3
    nchunk = seq // RW_CHUNK
    width = RW_PAIRS * LANES
    nblk = d // width

    def third(which):
        return pl.BlockSpec((RW_CHUNK, width), lambda b, p, c: (b * nchunk + c, which * nblk + p))

    blk = third(0)
    vec = pl.BlockSpec((None, 1, width), lambda b, p, c: (layer, 0, p))
    r, k, v, lw, a, g = rkv, rkv, rkv, lag, lag, lag
    return pl.pallas_call(
        _rwkv_scan_body,
        grid=(bsz, nblk, nchunk),
        in_specs=[third(0), third(1), third(2)] * 2 + [vec] * 5,
        out_specs=blk,
        out_shape=jax.ShapeDtypeStruct((t, d), BF16),
        scratch_shapes=[pltpu.VMEM((RW_PAIRS, LANES, LANES), F32)],
        compiler_params=_cparams(("parallel", "parallel", "arbitrary")),
        name="rwkv_scan",
    )(r, k, v, lw, a, g, kk, ka, rk, gg, gb)


def _pad_cols(w, n):
    return jnp.pad(w, ((0, 0), (0, 0), (0, n - w.shape[-1])))


def _pad_rows(w, n):
    return jnp.pad(w, ((0, 0), (0, n - w.shape[1]), (0, 0)))


def kernel(x, p, norms, final_norm, ffn_wg, ffn_wu, ffn_wd, ple_wp, ple_wg, e_w_in, e_w_out, a_vnorm, a_ws, a_bs, b_onorm, b_lb_logits, c_mix, c_wr, c_wk, c_wv, c_wo, c_w0, c_w1, c_w2, c_a0, c_a1, c_a2, c_g1, c_g2, c_kk, c_ka, c_rk, c_gn_g, c_gn_b, c_v0, c_v1, c_v2):
    bsz, seq, d = x.shape
    depth = p.shape[0]
    t = bsz * seq
    a_width = a_vnorm.shape[-1]
    b_width = b_onorm.shape[-1]

    bf = lambda w: w.astype(BF16)
    wg_b, wu_b, wd_b = ffn_wg, ffn_wu, ffn_wd
    ple_wp_b, ple_wg_b = bf(ple_wp), bf(ple_wg)
    e_in_b = e_w_in
    e_out_b = e_w_out
    wo_b = c_wo
    rkv_b = bf(jnp.concatenate([c_wr, c_wk, c_wv], axis=-1))
    rank = c_g1.shape[-1]
    w1_b = bf(_pad_cols(c_w1, rank))
    a1_b = bf(_pad_cols(c_a1, rank))
    g1_b = bf(c_g1)
    lag_b = bf(jnp.concatenate([_pad_rows(c_w2, rank), _pad_rows(c_a2, rank), c_g2], axis=-1))
    bias_b = jnp.concatenate([c_w0, c_a0, jnp.zeros_like(c_w0)], axis=-1).reshape(c_w0.shape[0], 1, 3 * d)
    v1_b = bf(_pad_cols(c_v1, LORA_PAD))
    v2_b = bf(_pad_rows(c_v2, LORA_PAD))

    vec3 = lambda w: w.reshape(w.shape[0], 1, -1)
    a_vnorm3, b_onorm3 = vec3(a_vnorm), vec3(b_onorm)
    a_bs4 = a_bs.reshape(a_bs.shape + (1,))
    v0_3 = vec3(c_v0)
    kk3, ka3, rk3, gg3, gb3 = vec3(c_kk), vec3(c_ka), vec3(c_rk), vec3(c_gn_g), vec3(c_gn_b)
    fg = final_norm.reshape(1, d)

    h = x.reshape(t, d)
    p2 = p.reshape(depth, t, p.shape[-1])
    rkv_first = None
    for i in range(depth):
        j = i // 2
        h = _ffn(h, norms[i, 0].reshape(1, d), wg_b, wu_b, wd_b, i, 0)
        g1n = norms[i, 1].reshape(1, d)
        if i % 2 == 0:
            proj = _nmm(h, g1n, e_in_b, j)
            a_out = _gmlp(proj, a_vnorm3, a_ws, a_bs4, j, a_width)
            b_out = _hgrn2(proj, b_lb_logits, b_onorm3, i, j, bsz, seq, a_width, b_width)
            h = _mm2_res(h, a_out, b_out, e_out_b, j)
        else:
            vres = None if j == 0 else (v1_b, v2_b, v0_3, rkv_first, j - 1)
            rkv, lag = _rwkv_proj(h, g1n, c_mix, rkv_b, w1_b, a1_b, g1_b, lag_b, bias_b, j, seq, vres=vres)
            if j == 0:
                rkv_first = rkv
            y = _rwkv_scan(rkv, lag, kk3, ka3, rk3, gg3, gb3, j, bsz, seq)
            h = _mm_res(h, y, wo_b, j)
        h = _ffn(h, norms[i, 2].reshape(1, d), wg_b, wu_b, wd_b, i, 1)
        h = _ple(h, norms[i, 3].reshape(1, d), p2, ple_wg_b, ple_wp_b, fg, i, final=(i == depth - 1))
    return h.reshape(bsz, seq, d)
```

```python
import functools

import jax
import jax.numpy as jnp
from jax import lax
from jax.experimental import pallas as pl
from jax.experimental.pallas import tpu as pltpu

F32 = jnp.float32
BF16 = jnp.bfloat16

LANES = 128
RMS_EPS = 1e-6
A_CHUNK = 128
B_HEAD = 128
B_MIN_F = 1e-30
C_HEAD = 64
C_GN_EPS = 64e-5
LORA_PAD = 128

VMEM_LIMIT = 60 * 1024 * 1024

_NN = ((1,), (0,))
_NT = ((1,), (1,))
_TN = ((0,), (0,))


def _cparams(sem):
    return pltpu.CompilerParams(dimension_semantics=sem, vmem_limit_bytes=VMEM_LIMIT)


def _rms(x, g, eps=RMS_EPS):
    return x * lax.rsqrt(jnp.mean(x * x, axis=-1, keepdims=True) + eps) * g


def _sigmoid(x):
    return 1.0 / (1.0 + jnp.exp(-x))


def _silu(x):
    return x * _sigmoid(x)


def _gelu_tanh(x):
    return 0.5 * x * (1.0 + jnp.tanh(0.7978845608028654 * (x + 0.044715 * (x * x * x))))


def _dot(a, b, dims=_NN):
    return lax.dot_general(a, b, (dims, ((), ())), preferred_element_type=F32)


def _sp(x, lo):
    hi = x.astype(BF16)
    if not lo:
        return (hi,)
    return (hi, (x - hi.astype(F32)).astype(BF16))


def _pd(a, b, dims=_NN):
    out = _dot(a[0], b[0], dims)
    extra = None
    if len(b) > 1:
        extra = _dot(a[0], b[1], dims)
    if len(a) > 1:
        t = _dot(a[1], b[0], dims)
        extra = t if extra is None else extra + t
    return out if extra is None else out + extra


def _cumsum_rows(tri_bf16, x):
    hi = x.astype(BF16)
    r1 = x - hi.astype(F32)
    mid = r1.astype(BF16)
    lo = (r1 - mid.astype(F32)).astype(BF16)
    return _dot(tri_bf16, hi) + (_dot(tri_bf16, mid) + _dot(tri_bf16, lo))


def _tri_masks(n):
    row = lax.broadcasted_iota(jnp.int32, (n, n), 0)
    col = lax.broadcasted_iota(jnp.int32, (n, n), 1)
    return col <= row, col < row


FFN_DOWN_COLS = 512
FFN_HID_COLS = 256


def _ffn_body(x_ref, g_ref, wg_ref, wu_ref, wd_ref, o_ref, xn_ref):
    j = pl.program_id(1)

    @pl.when(j == 0)
    def _():
        x = x_ref[...]
        xn_ref[...] = _rms(x, g_ref[...]).astype(BF16)
        o_ref[...] = x

    xn = xn_ref[...]
    for f0 in range(0, wg_ref.shape[1], FFN_HID_COLS):
        fs = slice(f0, f0 + FFN_HID_COLS)
        gate = _dot(xn, wg_ref[:, fs].astype(BF16))
        up = _dot(xn, wu_ref[:, fs].astype(BF16))
        hid = (0.5 * _silu(gate) * up).astype(BF16)
        for c in range(0, o_ref.shape[1], FFN_DOWN_COLS):
            cols = slice(c, c + FFN_DOWN_COLS)
            o_ref[:, cols] += _dot(hid, wd_ref[fs, cols].astype(BF16))


def _ffn(h, g, wg, wu, wd, layer, half, tm=1024, tf=512):
    t, d = h.shape
    f = wg.shape[-1]
    tm = min(tm, t)
    return pl.pallas_call(
        _ffn_body,
        grid=(t // tm, f // tf),
        in_specs=[
            pl.BlockSpec((tm, d), lambda i, j: (i, 0), pipeline_mode=pl.Buffered(1)),
            pl.BlockSpec((1, d), lambda i, j: (0, 0)),
            pl.BlockSpec((None, None, d, tf), lambda i, j: (layer, half, 0, j)),
            pl.BlockSpec((None, None, d, tf), lambda i, j: (layer, half, 0, j)),
            pl.BlockSpec((None, None, tf, d), lambda i, j: (layer, half, j, 0)),
        ],
        out_specs=pl.BlockSpec((tm, d), lambda i, j: (i, 0)),
        out_shape=jax.ShapeDtypeStruct((t, d), F32),
        scratch_shapes=[pltpu.VMEM((tm, d), BF16)],
        compiler_params=_cparams(("parallel", "arbitrary")),
        name="ffn",
    )(h, g, wg, wu, wd)


def _mm_res_body(h_ref, x_ref, w_ref, o_ref):
    o_ref[...] = h_ref[...] + _dot(x_ref[...], w_ref[...].astype(BF16))


def _mm_res(h, x, w, layer, tm=1024, tn=1024):
    t, k = x.shape
    n = w.shape[-1]
    tm = min(tm, t)
    return pl.pallas_call(
        _mm_res_body,
        grid=(t // tm, n // tn),
        in_specs=[
            pl.BlockSpec((tm, tn), lambda i, j: (i, j)),
            pl.BlockSpec((tm, k), lambda i, j: (i, 0)),
            pl.BlockSpec((None, k, tn), lambda i, j: (layer, 0, j)),
        ],
        out_specs=pl.BlockSpec((tm, tn), lambda i, j: (i, j)),
        out_shape=jax.ShapeDtypeStruct((t, n), F32),
        compiler_params=_cparams(("parallel", "parallel")),
        name="mm_res",
    )(h, x, w)


def _mm2_res_body(h_ref, xa_ref, xb_ref, wa_ref, wb_ref, o_ref):
    o_ref[...] = h_ref[...] + (_dot(xa_ref[...], wa_ref[...].astype(BF16))
                               + _dot(xb_ref[...], wb_ref[...].astype(BF16)))


def _mm2_res(h, xa, xb, w, layer, tm=1024, tn=1024):
    t, ka = xa.shape
    kb = xb.shape[1]
    n = w.shape[-1]
    assert ka == kb
    tm = min(tm, t)
    return pl.pallas_call(
        _mm2_res_body,
        grid=(t // tm, n // tn),
        in_specs=[
            pl.BlockSpec((tm, tn), lambda i, j: (i, j)),
            pl.BlockSpec((tm, ka), lambda i, j: (i, 0)),
            pl.BlockSpec((tm, kb), lambda i, j: (i, 0)),
            pl.BlockSpec((None, ka, tn), lambda i, j: (layer, 0, j)),
            pl.BlockSpec((None, kb, tn), lambda i, j: (layer, 1, j)),
        ],
        out_specs=pl.BlockSpec((tm, tn), lambda i, j: (i, j)),
        out_shape=jax.ShapeDtypeStruct((t, n), F32),
        compiler_params=_cparams(("parallel", "parallel")),
        name="mm2_res",
    )(h, xa, xb, w, w)


def _nmm_body(x_ref, g_ref, w_ref, o_ref, xn_ref):
    @pl.when(pl.program_id(1) == 0)
    def _():
        xn_ref[...] = _rms(x_ref[...], g_ref[...]).astype(BF16)

    o_ref[...] = _dot(xn_ref[...], w_ref[...].astype(BF16))


def _nmm(h, g, w, layer, tm=1024, tn=1024):
    t, d = h.shape
    n = w.shape[-1]
    tm = min(tm, t)
    return pl.pallas_call(
        _nmm_body,
        grid=(t // tm, n // tn),
        in_specs=[
            pl.BlockSpec((tm, d), lambda i, j: (i, 0)),
            pl.BlockSpec((1, d), lambda i, j: (0, 0)),
            pl.BlockSpec((None, d, tn), lambda i, j: (layer, 0, j)),
        ],
        out_specs=pl.BlockSpec((tm, tn), lambda i, j: (i, j)),
        out_shape=jax.ShapeDtypeStruct((t, n), F32),
        scratch_shapes=[pltpu.VMEM((tm, d), BF16)],
        compiler_params=_cparams(("parallel", "arbitrary")),
        name="norm_mm",
    )(h, g, w)


def _ple_body(h_ref, g_ref, p_ref, wg_ref, wp_ref, fg_ref, o_ref, *, final):
    h = h_ref[...]
    xn = _rms(h, g_ref[...]).astype(BF16)
    gate = _sigmoid(_dot(xn, wg_ref[...]))
    pe = _dot(p_ref[...].astype(BF16), wp_ref[...])
    out = h + gate * pe
    if final:
        out = _rms(out, fg_ref[...])
    o_ref[...] = out


def _ple(h, g, p, wg, wp, fg, layer, final, tm=512):
    t, d = h.shape
    pd = p.shape[-1]
    return pl.pallas_call(
        functools.partial(_ple_body, final=final),
        grid=(t // tm,),
        in_specs=[
            pl.BlockSpec((tm, d), lambda i: (i, 0)),
            pl.BlockSpec((1, d), lambda i: (0, 0)),
            pl.BlockSpec((None, tm, pd), lambda i: (layer, i, 0)),
            pl.BlockSpec((None, d, d), lambda i: (layer, 0, 0)),
            pl.BlockSpec((None, pd, d), lambda i: (layer, 0, 0)),
            pl.BlockSpec((1, d), lambda i: (0, 0)),
        ],
        out_specs=pl.BlockSpec((tm, d), lambda i: (i, 0)),
        out_shape=jax.ShapeDtypeStruct((t, d), F32),
        compiler_params=_cparams(("parallel",)),
        name="ple",
    )(h, g, p, wg, wp, fg)


GMLP_CHUNKS = 2


def _gmlp_body(u_ref, v_ref, gain_ref, ws_ref, bs_ref, o_ref):
    groups = ws_ref.shape[0]
    tril, _ = _tri_masks(A_CHUNK)
    for g in range(groups):
        w = jnp.where(tril, ws_ref[g], 0.0).astype(BF16)
        bias = bs_ref[g]
        cols = slice(g * A_CHUNK, (g + 1) * A_CHUNK)
        for c in range(GMLP_CHUNKS):
            rows = slice(c * A_CHUNK, (c + 1) * A_CHUNK)
            u = _gelu_tanh(u_ref[rows, cols])
            v = _gelu_tanh(v_ref[rows, cols])
            vg = _rms(v, gain_ref[:, cols])
            s = _dot(w, vg.astype(BF16)) + bias
            o_ref[rows, cols] = (u * s).astype(o_ref.dtype)


def _gmlp(proj, gain, ws, bs, layer, a_width):
    t = proj.shape[0]
    groups = a_width // A_CHUNK
    tm = GMLP_CHUNKS * A_CHUNK
    return pl.pallas_call(
        _gmlp_body,
        grid=(t // tm,),
        in_specs=[
            pl.BlockSpec((tm, a_width), lambda c: (c, 0)),
            pl.BlockSpec((tm, a_width), lambda c: (c, 1)),
            pl.BlockSpec((None, 1, a_width), lambda c: (layer, 0, 0)),
            pl.BlockSpec((None, groups, A_CHUNK, A_CHUNK), lambda c: (layer, 0, 0, 0)),
            pl.BlockSpec((None, groups, A_CHUNK, 1), lambda c: (layer, 0, 0, 0)),
        ],
        out_specs=pl.BlockSpec((tm, a_width), lambda c: (c, 0)),
        out_shape=jax.ShapeDtypeStruct((t, a_width), BF16),
        compiler_params=_cparams(("parallel",)),
        name="gmlp",
    )(proj, proj, gain, ws, bs)


HG_CHUNK = 64
HG_SUB = 16


HG_HEADS = 8
P_HG = False


def _hgrn2_body(q_ref, f_ref, i_ref, g_ref, lbl_ref, on_ref, o_ref, st_ref, *, layer):
    c = pl.program_id(2)

    @pl.when(c == 0)
    def _():
        st_ref[...] = jnp.zeros_like(st_ref)

    logits = lbl_ref[...]
    e = jnp.exp(logits - jnp.max(logits, axis=0, keepdims=True))
    probs = e / jnp.sum(e, axis=0, keepdims=True)
    lb = jnp.zeros((1, logits.shape[1]), F32)
    for r in range(1, layer + 1):
        lb = lb + probs[r:r + 1, :]

    n = HG_CHUNK
    hs = range(HG_HEADS)
    cols = [slice(hd * B_HEAD, (hd + 1) * B_HEAD) for hd in hs]
    tril, _ = _tri_masks(n)
    tri = jnp.where(tril, 1.0, 0.0).astype(BF16)

    st = [st_ref[hd] for hd in hs]
    v = [i_ref[:, c] for c in cols]
    f = [lb[:, c] + (1.0 - lb[:, c]) * _sigmoid(f_ref[:, c]) for c in cols]
    kf = [1.0 - x for x in f]
    qf = [_silu(q_ref[:, c]) for c in cols]
    cum = [_cumsum_rows(tri, jnp.log(jnp.maximum(x, B_MIN_F))) for x in f]
    last = [x[n - 1:n, :] for x in cum]

    st_s = [_sp(x, P_HG) for x in st]
    v_s = [_sp(x, P_HG) for x in v]
    o = [_pd(_sp(qf[h] * jnp.exp(cum[h]), P_HG), st_s[h], _NT) for h in hs]
    for h in hs:
        kend = kf[h] * jnp.exp(last[h] - cum[h])
        st_ref[h] = st[h] * jnp.exp(last[h]) + _pd(v_s[h], _sp(kend, P_HG), _TN)

    nsub = n // HG_SUB
    trow = lax.broadcasted_iota(jnp.int32, (HG_SUB, 1), 0)
    rows = [[] for _ in hs]
    for bi in range(nsub):
        lo, hi = bi * HG_SUB, (bi + 1) * HG_SUB
        acc = [jnp.zeros((HG_SUB, B_HEAD), F32) for _ in hs]
        if bi > 0:
            ref = [cum[h][lo - 1:lo, :] for h in hs]
            qh = [_sp(qf[h][lo:hi] * jnp.exp(cum[h][lo:hi] - ref[h]), P_HG) for h in hs]
            kh = [_sp(kf[h][:lo] * jnp.exp(ref[h] - cum[h][:lo]), P_HG) for h in hs]
            att = [_sp(_pd(qh[h], kh[h], _NT), P_HG) for h in hs]
            acc = [_pd(att[h], tuple(part[:lo] for part in v_s[h])) for h in hs]
        for s in range(HG_SUB):
            for h in hs:
                c_b = cum[h][lo:hi]
                dec = jnp.exp(jnp.minimum(c_b - c_b[s:s + 1, :], 0.0))
                col = jnp.sum(qf[h][lo:hi] * dec * kf[h][lo + s:lo + s + 1, :], axis=-1, keepdims=True)
                col = jnp.where(trow >= s, col, 0.0)
                acc[h] = acc[h] + col * v[h][lo + s:lo + s + 1, :]
        for h in hs:
            rows[h].append(acc[h])

    for h in hs:
        out = o[h] + jnp.concatenate(rows[h], axis=0)
        o_ref[:, cols[h]] = (_rms(out, on_ref[:, cols[h]]) * _silu(g_ref[:, cols[h]])).astype(o_ref.dtype)


def _hgrn2(proj, lb_logits, onorm, layer, elayer, bsz, seq, a_width, b_width):
    t = proj.shape[0]
    heads = b_width // B_HEAD
    hgroups = heads // HG_HEADS
    nchunk = seq // HG_CHUNK
    width = HG_HEADS * B_HEAD
    off = 2 * a_width // width

    def col(which):
        return lambda b, h, c: (b * nchunk + c, off + which * hgroups + h)

    blk = (HG_CHUNK, width)
    depth = lb_logits.shape[0]
    return pl.pallas_call(
        functools.partial(_hgrn2_body, layer=layer),
        grid=(bsz, hgroups, nchunk),
        in_specs=[
            pl.BlockSpec(blk, col(0)),
            pl.BlockSpec(blk, col(1)),
            pl.BlockSpec(blk, col(2)),
            pl.BlockSpec(blk, col(3)),
            pl.BlockSpec((depth, width), lambda b, h, c: (0, h)),
            pl.BlockSpec((None, 1, width), lambda b, h, c: (elayer, 0, h)),
        ],
        out_specs=pl.BlockSpec(blk, lambda b, h, c: (b * nchunk + c, h)),
        out_shape=jax.ShapeDtypeStruct((t, b_width), BF16),
        scratch_shapes=[pltpu.VMEM((HG_HEADS, B_HEAD, B_HEAD), F32)],
        compiler_params=_cparams(("parallel", "parallel", "arbitrary")),
        name="hgrn2",
    )(proj, proj, proj, proj, lb_logits, onorm)


MIX_R, MIX_W, MIX_K, MIX_V, MIX_A, MIX_G = range(6)
RW_DECAY_SCALE = 0.6065306597126334
PROJ_SUB = 128
SUBLANES = 8


def _rwkv_proj_body(*refs, tm, seq, ntile, vres):
    if vres:
        (x_ref, xp_ref, g_ref, mix_ref, w_ref, w1_ref, a1_ref, g1_ref, l_ref, b_ref,
         v1_ref, v2_ref, v0_ref, vf_ref, o_ref, lag_ref, xs_ref, hs_ref, hv_ref) = refs
    else:
        (x_ref, xp_ref, g_ref, mix_ref, w_ref, w1_ref, a1_ref, g1_ref, l_ref, b_ref,
         o_ref, lag_ref, xs_ref, hs_ref) = refs
    i = pl.program_id(0)
    j = pl.program_id(1)

    @pl.when(j == 0)
    def _():
        g = g_ref[...]
        for r0 in range(0, tm, PROJ_SUB):
            rows = slice(r0, r0 + PROJ_SUB)
            hn = _rms(x_ref[rows, :], g)
            if r0 == 0:
                hp = _rms(xp_ref[...], g)[SUBLANES - 1:SUBLANES, :]
                hp = jnp.where((i * tm) % seq == 0, 0.0, hp)
            else:
                hp = _rms(x_ref[r0 - SUBLANES:r0, :], g)[SUBLANES - 1:SUBLANES, :]
            row = lax.broadcasted_iota(jnp.int32, hn.shape, 0)
            xx = jnp.where(row == 0, hp, pltpu.roll(hn, 1, 0)) - hn

            def mixed(m):
                return (hn + xx * mix_ref[m:m + 1, :]).astype(BF16)

            xs_ref[0, rows, :] = mixed(MIX_R)
            xs_ref[1, rows, :] = mixed(MIX_K)
            xv = mixed(MIX_V)
            xs_ref[2, rows, :] = xv
            hs_ref[0, rows, :] = jnp.tanh(_dot(mixed(MIX_W), w1_ref[...])).astype(BF16)
            hs_ref[1, rows, :] = _dot(mixed(MIX_A), a1_ref[...]).astype(BF16)
            hs_ref[2, rows, :] = _sigmoid(_dot(mixed(MIX_G), g1_ref[...])).astype(BF16)
            if vres:
                hv_ref[rows, :] = _dot(xv, v1_ref[...]).astype(BF16)

    third = j // ntile
    y = _dot(xs_ref[third], w_ref[...])
    z = b_ref[...] + _dot(hs_ref[third], l_ref[...])

    s = _sigmoid(z)
    lag_ref[...] = jnp.where(third == 2, z, s * jnp.where(third == 0, -RW_DECAY_SCALE, 1.0))
    if vres:
        @pl.when(third == 2)
        def _():
            mv = _sigmoid(v0_ref[...] + _dot(hv_ref[...], v2_ref[...]))
            o_ref[...] = y + (vf_ref[...] - y) * mv

        @pl.when(third != 2)
        def _():
            o_ref[...] = y
    else:
        o_ref[...] = y


def _rwkv_proj(h, g, mix, wcat, w1, a1, g1, lcat, bcat, layer, seq, vres=None, tm=1024, tn=512):
    t, d = h.shape
    tm = min(tm, t)
    ntile = d // tn
    rank = w1.shape[-1]

    def lora(lyr):
        return pl.BlockSpec((None, d, rank), lambda i, j: (lyr, 0, 0))

    def vtile(i, j):
        return jnp.maximum(j - 2 * ntile, 0)

    args = [h, h, g, mix, wcat, w1, a1, g1, lcat, bcat]
    in_specs = [
        pl.BlockSpec((tm, d), lambda i, j: (i, 0)),
        pl.BlockSpec((SUBLANES, d), lambda i, j: (jnp.maximum(i * (tm // SUBLANES) - 1, 0), 0)),
        pl.BlockSpec((1, d), lambda i, j: (0, 0)),
        pl.BlockSpec((None, mix.shape[1], d), lambda i, j: (layer, 0, 0)),
        pl.BlockSpec((None, d, tn), lambda i, j: (layer, 0, j)),
        lora(layer), lora(layer), lora(layer),
        pl.BlockSpec((None, rank, tn), lambda i, j: (layer, 0, j)),
        pl.BlockSpec((None, 1, tn), lambda i, j: (layer, 0, j)),
    ]
    scratch = [pltpu.VMEM((3, tm, d), BF16), pltpu.VMEM((3, tm, rank), BF16)]
    if vres is not None:
        v1, v2, v0, rkv_first, vl = vres
        vrank = v1.shape[-1]
        args += [v1, v2, v0, rkv_first]
        in_specs += [
            pl.BlockSpec((None, d, vrank), lambda i, j: (vl, 0, 0)),
            pl.BlockSpec((None, vrank, tn), lambda i, j: (vl, 0, vtile(i, j))),
            pl.BlockSpec((None, 1, tn), lambda i, j: (vl, 0, vtile(i, j))),
            pl.BlockSpec((tm, tn), lambda i, j: (i, 2 * ntile + vtile(i, j))),
        ]
        scratch.append(pltpu.VMEM((tm, vrank), BF16))
    tile = pl.BlockSpec((tm, tn), lambda i, j: (i, j))
    return pl.pallas_call(
        functools.partial(_rwkv_proj_body, tm=tm, seq=seq, ntile=ntile, vres=vres is not None),
        grid=(t // tm, 3 * ntile),
        in_specs=in_specs,
        out_specs=[tile, tile],
        out_shape=[jax.ShapeDtypeStruct((t, 3 * d), F32), jax.ShapeDtypeStruct((t, 3 * d), F32)],
        scratch_shapes=scratch,
        compiler_params=_cparams(("parallel", "arbitrary")),
        name="rwkv_proj",
    )(*args)


RW_CHUNK = 64


RW_PAIRS = 16


def _head_sum(x, m0):
    s0 = jnp.sum(jnp.where(m0, x, 0.0), axis=-1, keepdims=True)
    s1 = jnp.sum(jnp.where(m0, 0.0, x), axis=-1, keepdims=True)
    return jnp.where(m0, s0, s1)


def _rwkv_scan_body(r_ref, k_ref, v_ref, lw_ref, a_ref, g_ref, kk_ref, ka_ref, rk_ref, gg_ref, gb_ref,
                    o_ref, st_ref):
    c = pl.program_id(2)

    @pl.when(c == 0)
    def _():
        st_ref[...] = jnp.zeros_like(st_ref)

    n = RW_CHUNK
    pairs = range(RW_PAIRS)
    cols = [slice(p * LANES, (p + 1) * LANES) for p in pairs]
    lane = lax.broadcasted_iota(jnp.int32, (1, LANES), 1)
    m0 = lane < C_HEAD
    m1 = jnp.logical_not(m0)
    tril, _ = _tri_masks(n)
    tri = jnp.where(tril, 1.0, 0.0).astype(BF16)

    def hsel(x, hd):
        return jnp.where(m0 if hd == 0 else m1, x, jnp.zeros_like(x))

    st = [st_ref[p] for p in pairs]
    r = [r_ref[:, c] for c in cols]
    k = [k_ref[:, c] for c in cols]
    v = [v_ref[:, c] for c in cols]
    asig = [a_ref[:, c] for c in cols]

    cum = [_cumsum_rows(tri, lw_ref[:, c]) for c in cols]
    last = [x[n - 1:n, :] for x in cum]
    kkr = [k[p] * kk_ref[:, cols[p]] for p in pairs]
    kk = [x / jnp.maximum(jnp.sqrt(_head_sum(x * x, m0)), 1e-12) for x in kkr]
    kmod = [k[p] * (1.0 + (asig[p] - 1.0) * ka_ref[:, cols[p]]) for p in pairs]
    b = [kk[p] * asig[p] for p in pairs]
    rt = [(r[p] * jnp.exp(cum[p])).astype(BF16) for p in pairs]
    at = [(-kk[p] * jnp.exp(cum[p] - lw_ref[:, cols[p]])).astype(BF16) for p in pairs]
    einv = [jnp.exp(-x) for x in cum]
    bt = [(b[p] * einv[p]).astype(BF16) for p in pairs]
    kt = [(kmod[p] * einv[p]).astype(BF16) for p in pairs]
    eend = [jnp.exp(last[p] - cum[p]) for p in pairs]
    v_b = [x.astype(BF16) for x in v]
    st_b = [x.astype(BF16) for x in st]

    def bd(x):
        return jnp.concatenate([hsel(x, 0), hsel(x, 1)], axis=0)

    srow = lax.broadcasted_iota(jnp.int32, (n, LANES), 0)
    scol = lax.broadcasted_iota(jnp.int32, (n, LANES), 1) % n
    tril2, stril2 = scol <= srow, scol < srow
    eye2 = jnp.where(scol == srow, 1.0, 0.0)
    bd_b = [bd(x) for x in bt]
    bd_k = [bd(x) for x in kt]
    bd_v = [bd(x) for x in v_b]

    pw = [jnp.where(stril2, _dot(at[p], bd_b[p], _NT), 0.0) for p in pairs]
    t_inv = [eye2 + x for x in pw]
    bd_p = [bd(x.astype(BF16)) for x in pw]
    fill = {}
    fillers = [
        lambda: fill.update(a_ak=[jnp.where(stril2, _dot(at[p], bd_k[p], _NT), 0.0).astype(BF16) for p in pairs]),
        lambda: fill.update(a_rb=[jnp.where(tril2, _dot(rt[p], bd_b[p], _NT), 0.0).astype(BF16) for p in pairs]),
        lambda: fill.update(a_rk=[jnp.where(tril2, _dot(rt[p], bd_k[p], _NT), 0.0).astype(BF16) for p in pairs]),
        lambda: fill.update(av=[_dot(fill["a_ak"][p], bd_v[p]) for p in pairs]),
        lambda: fill.update(y_v=[_dot(fill["a_rk"][p], bd_v[p]) for p in pairs]),
    ]
    step = 2
    while step < n:
        pw = [_dot(pw[p].astype(BF16), bd_p[p]) for p in pairs]
        bd_p = [bd(x.astype(BF16)) for x in pw]
        if fillers:
            fillers.pop(0)()
        t_inv = [t_inv[p] + _dot(t_inv[p].astype(BF16), bd_p[p]) for p in pairs]
        step *= 2
    for filler in fillers:
        filler()
    t_inv = [x.astype(BF16) for x in t_inv]
    a_rb, av, y_v = fill["a_rb"], fill["av"], fill["y_v"]

    x = [(_dot(at[p], st_b[p], _NT) + av[p]).astype(BF16) for p in pairs]
    u_b = [_dot(t_inv[p], bd(x[p])).astype(BF16) for p in pairs]
    y = [_dot(rt[p], st_b[p], _NT) + _dot(a_rb[p], bd(u_b[p])) + y_v[p] for p in pairs]

    row = lax.broadcasted_iota(jnp.int32, (LANES, LANES), 0)
    colm = lax.broadcasted_iota(jnp.int32, (LANES, LANES), 1)
    bdiag = (row < C_HEAD) == (colm < C_HEAD)
    for p in pairs:
        uv = jnp.concatenate([u_b[p], v_b[p]], axis=0)
        bk = jnp.concatenate([(b[p] * eend[p]).astype(BF16), (kmod[p] * eend[p]).astype(BF16)], axis=0)
        st_ref[p] = st[p] * jnp.exp(last[p]) + jnp.where(bdiag, _dot(uv, bk, _TN), 0.0)

    inv_n = 1.0 / C_HEAD
    for p in pairs:
        yp = y[p]
        mu = _head_sum(yp, m0) * inv_n
        dy = yp - mu
        var = _head_sum(dy * dy, m0) * inv_n
        yn = dy * lax.rsqrt(var + C_GN_EPS) * gg_ref[:, cols[p]] + gb_ref[:, cols[p]]
        bonus = _head_sum(r[p] * kmod[p] * rk_ref[:, cols[p]], m0) * v[p]
        o_ref[:, cols[p]] = ((yn + bonus) * g_ref[:, cols[p]]).astype(o_ref.dtype)


def _rwkv_scan(rkv, lag, kk, ka, rk, gg, gb, layer, bsz, seq):
    t = rkv.shape[0]
    d = rkv.shape[1] // 3
    nchunk = seq // RW_CHUNK
    width = RW_PAIRS * LANES
    nblk = d // width

    def third(which):
        return pl.BlockSpec((RW_CHUNK, width), lambda b, p, c: (b * nchunk + c, which * nblk + p))

    blk = third(0)
    vec = pl.BlockSpec((None, 1, width), lambda b, p, c: (layer, 0, p))
    r, k, v, lw, a, g = rkv, rkv, rkv, lag, lag, lag
    return pl.pallas_call(
        _rwkv_scan_body,
        grid=(bsz, nblk, nchunk),
        in_specs=[third(0), third(1), third(2)] * 2 + [vec] * 5,
        out_specs=blk,
        out_shape=jax.ShapeDtypeStruct((t, d), BF16),
        scratch_shapes=[pltpu.VMEM((RW_PAIRS, LANES, LANES), F32)],
        compiler_params=_cparams(("parallel", "parallel", "arbitrary")),
        name="rwkv_scan",
    )(r, k, v, lw, a, g, kk, ka, rk, gg, gb)


def _pad_cols(w, n):
    return jnp.pad(w, ((0, 0), (0, 0), (0, n - w.shape[-1])))


def _pad_rows(w, n):
    return jnp.pad(w, ((0, 0), (0, n - w.shape[1]), (0, 0)))


def kernel(x, p, norms, final_norm, ffn_wg, ffn_wu, ffn_wd, ple_wp, ple_wg, e_w_in, e_w_out, a_vnorm, a_ws, a_bs, b_onorm, b_lb_logits, c_mix, c_wr, c_wk, c_wv, c_wo, c_w0, c_w1, c_w2, c_a0, c_a1, c_a2, c_g1, c_g2, c_kk, c_ka, c_rk, c_gn_g, c_gn_b, c_v0, c_v1, c_v2):
    bsz, seq, d = x.shape
    depth = p.shape[0]
    t = bsz * seq
    a_width = a_vnorm.shape[-1]
    b_width = b_onorm.shape[-1]

    bf = lambda w: w.astype(BF16)
    wg_b, wu_b, wd_b = ffn_wg, ffn_wu, ffn_wd
    ple_wp_b, ple_wg_b = bf(ple_wp), bf(ple_wg)
    e_in_b = e_w_in
    e_out_b = e_w_out
    wo_b = c_wo
    rkv_b = bf(jnp.concatenate([c_wr, c_wk, c_wv], axis=-1))
    rank = c_g1.shape[-1]
    w1_b = bf(_pad_cols(c_w1, rank))
    a1_b = bf(_pad_cols(c_a1, rank))
    g1_b = bf(c_g1)
    lag_b = bf(jnp.concatenate([_pad_rows(c_w2, rank), _pad_rows(c_a2, rank), c_g2], axis=-1))
    bias_b = jnp.concatenate([c_w0, c_a0, jnp.zeros_like(c_w0)], axis=-1).reshape(c_w0.shape[0], 1, 3 * d)
    v1_b = bf(_pad_cols(c_v1, LORA_PAD))
    v2_b = bf(_pad_rows(c_v2, LORA_PAD))

    vec3 = lambda w: w.reshape(w.shape[0], 1, -1)
    a_vnorm3, b_onorm3 = vec3(a_vnorm), vec3(b_onorm)
    a_bs4 = a_bs.reshape(a_bs.shape + (1,))
    v0_3 = vec3(c_v0)
    kk3, ka3, rk3, gg3, gb3 = vec3(c_kk), vec3(c_ka), vec3(c_rk), vec3(c_gn_g), vec3(c_gn_b)
    fg = final_norm.reshape(1, d)

    h = x.reshape(t, d)
    p2 = p.reshape(depth, t, p.shape[-1])
    rkv_first = None
    for i in range(depth):
        j = i // 2
        h = _ffn(h, norms[i, 0].reshape(1, d), wg_b, wu_b, wd_b, i, 0)
        g1n = norms[i, 1].reshape(1, d)
        if i % 2 == 0:
            proj = _nmm(h, g1n, e_in_b, j)
            a_out = _gmlp(proj, a_vnorm3, a_ws, a_bs4, j, a_width)
            b_out = _hgrn2(proj, b_lb_logits, b_onorm3, i, j, bsz, seq, a_width, b_width)
            h = _mm2_res(h, a_out, b_out, e_out_b, j)
        else:
            vres = None if j == 0 else (v1_b, v2_b, v0_3, rkv_first, j - 1)
            rkv, lag = _rwkv_proj(h, g1n, c_mix, rkv_b, w1_b, a1_b, g1_b, lag_b, bias_b, j, seq, vres=vres)
            if j == 0:
                rkv_first = rkv
            y = _rwkv_scan(rkv, lag, kk3, ka3, rk3, gg3, gb3, j, bsz, seq)
            h = _mm_res(h, y, wo_b, j)
        h = _ffn(h, norms[i, 2].reshape(1, d), wg_b, wu_b, wd_b, i, 1)
        h = _ple(h, norms[i, 3].reshape(1, d), p2, ple_wg_b, ple_wp_b, fg, i, final=(i == depth - 1))
    return h.reshape(bsz, seq, d)
```

```python
import functools

import jax
import jax.numpy as jnp
from jax import lax
from jax.experimental import pallas as pl
from jax.experimental.pallas import tpu as pltpu

F32 = jnp.float32
BF16 = jnp.bfloat16

LANES = 128
SUBLANES = 8
RMS_EPS = 1e-6
A_CHUNK = 128
B_HEAD = 128
B_MIN_F = 1e-30
C_HEAD = 64
C_GN_EPS = 64e-5
LORA_PAD = 128

VMEM_LIMIT = 56 * 1024 * 1024

_NN = ((1,), (0,))
_NT = ((1,), (1,))
_TN = ((0,), (0,))


def _cparams(sem):
    return pltpu.CompilerParams(dimension_semantics=sem, vmem_limit_bytes=VMEM_LIMIT)


def _rms(x, g, eps=RMS_EPS):
    return x * lax.rsqrt(jnp.mean(x * x, axis=-1, keepdims=True) + eps) * g


def _sigmoid(x):
    return 1.0 / (1.0 + jnp.exp(-x))


def _silu(x):
    return x * _sigmoid(x)


def _gelu_tanh(x):
    return 0.5 * x * (1.0 + jnp.tanh(0.7978845608028654 * (x + 0.044715 * (x * x * x))))


def _dot(a, b, dims=_NN):
    return lax.dot_general(a, b, (dims, ((), ())), preferred_element_type=F32)


def _cumsum_rows(tri_bf16, x):
    hi = x.astype(BF16)
    r1 = x - hi.astype(F32)
    mid = r1.astype(BF16)
    lo = (r1 - mid.astype(F32)).astype(BF16)
    return _dot(tri_bf16, hi) + (_dot(tri_bf16, mid) + _dot(tri_bf16, lo))


def _tri_masks(n):
    row = lax.broadcasted_iota(jnp.int32, (n, n), 0)
    col = lax.broadcasted_iota(jnp.int32, (n, n), 1)
    return col <= row, col < row


FFN_DOWN_COLS = 512


def _ffn_body(x_ref, g_ref, wg_ref, wu_ref, wd_ref, o_ref, xn_ref):
    j = pl.program_id(1)

    @pl.when(j == 0)
    def _():
        x = x_ref[...]
        xn_ref[...] = _rms(x, g_ref[...]).astype(BF16)
        o_ref[...] = x

    xn = xn_ref[...]
    gate = _dot(xn, wg_ref[...].astype(BF16))
    up = _dot(xn, wu_ref[...].astype(BF16))
    hid = (0.5 * _silu(gate) * up).astype(BF16)
    for c in range(0, o_ref.shape[1], FFN_DOWN_COLS):
        cols = slice(c, c + FFN_DOWN_COLS)
        o_ref[:, cols] += _dot(hid, wd_ref[:, cols].astype(BF16))


def _ffn(h, g, wg, wu, wd, layer, half, tm=1024, tf=256):
    t, d = h.shape
    f = wg.shape[-1]
    tm = min(tm, t)
    return pl.pallas_call(
        _ffn_body,
        grid=(t // tm, f // tf),
        in_specs=[
            pl.BlockSpec((tm, d), lambda i, j: (i, 0)),
            pl.BlockSpec((1, d), lambda i, j: (0, 0)),
            pl.BlockSpec((None, None, d, tf), lambda i, j: (layer, half, 0, j)),
            pl.BlockSpec((None, None, d, tf), lambda i, j: (layer, half, 0, j)),
            pl.BlockSpec((None, None, tf, d), lambda i, j: (layer, half, j, 0)),
        ],
        out_specs=pl.BlockSpec((tm, d), lambda i, j: (i, 0)),
        out_shape=jax.ShapeDtypeStruct((t, d), F32),
        scratch_shapes=[pltpu.VMEM((tm, d), BF16)],
        compiler_params=_cparams(("parallel", "arbitrary")),
        name="ffn",
    )(h, g, wg, wu, wd)


def _mm_res_body(h_ref, x_ref, w_ref, o_ref):
    o_ref[...] = h_ref[...] + _dot(x_ref[...], w_ref[...].astype(BF16))


def _mm_res(h, x, w, layer, tm=1024, tn=1024):
    t, k = x.shape
    n = w.shape[-1]
    tm = min(tm, t)
    return pl.pallas_call(
        _mm_res_body,
        grid=(t // tm, n // tn),
        in_specs=[
            pl.BlockSpec((tm, tn), lambda i, j: (i, j)),
            pl.BlockSpec((tm, k), lambda i, j: (i, 0)),
            pl.BlockSpec((None, k, tn), lambda i, j: (layer, 0, j)),
        ],
        out_specs=pl.BlockSpec((tm, tn), lambda i, j: (i, j)),
        out_shape=jax.ShapeDtypeStruct((t, n), F32),
        compiler_params=_cparams(("parallel", "parallel")),
        name="mm_res",
    )(h, x, w)


def _mm2_res_body(h_ref, xa_ref, xb_ref, wa_ref, wb_ref, o_ref):
    o_ref[...] = h_ref[...] + (_dot(xa_ref[...], wa_ref[...].astype(BF16))
                               + _dot(xb_ref[...], wb_ref[...].astype(BF16)))


def _mm2_res(h, xa, xb, w, layer, tm=1024, tn=1024):
    t, ka = xa.shape
    kb = xb.shape[1]
    n = w.shape[-1]
    assert ka == kb
    tm = min(tm, t)
    return pl.pallas_call(
        _mm2_res_body,
        grid=(t // tm, n // tn),
        in_specs=[
            pl.BlockSpec((tm, tn), lambda i, j: (i, j)),
            pl.BlockSpec((tm, ka), lambda i, j: (i, 0)),
            pl.BlockSpec((tm, kb), lambda i, j: (i, 0)),
            pl.BlockSpec((None, ka, tn), lambda i, j: (layer, 0, j)),
            pl.BlockSpec((None, kb, tn), lambda i, j: (layer, 1, j)),
        ],
        out_specs=pl.BlockSpec((tm, tn), lambda i, j: (i, j)),
        out_shape=jax.ShapeDtypeStruct((t, n), F32),
        compiler_params=_cparams(("parallel", "parallel")),
        name="mm2_res",
    )(h, xa, xb, w, w)


def _nmm_body(x_ref, g_ref, w_ref, o_ref, xn_ref):
    @pl.when(pl.program_id(1) == 0)
    def _():
        xn_ref[...] = _rms(x_ref[...], g_ref[...]).astype(BF16)

    o_ref[...] = _dot(xn_ref[...], w_ref[...].astype(BF16))


def _nmm(h, g, w, layer, tm=1024, tn=1024):
    t, d = h.shape
    n = w.shape[-1]
    tm = min(tm, t)
    return pl.pallas_call(
        _nmm_body,
        grid=(t // tm, n // tn),
        in_specs=[
            pl.BlockSpec((tm, d), lambda i, j: (i, 0)),
            pl.BlockSpec((1, d), lambda i, j: (0, 0)),
            pl.BlockSpec((None, d, tn), lambda i, j: (layer, 0, j)),
        ],
        out_specs=pl.BlockSpec((tm, tn), lambda i, j: (i, j)),
        out_shape=jax.ShapeDtypeStruct((t, n), F32),
        scratch_shapes=[pltpu.VMEM((tm, d), BF16)],
        compiler_params=_cparams(("parallel", "arbitrary")),
        name="norm_mm",
    )(h, g, w)


def _ple_body(h_ref, g_ref, p_ref, wg_ref, wp_ref, fg_ref, o_ref, *, final):
    h = h_ref[...]
    xn = _rms(h, g_ref[...]).astype(BF16)
    gate = _sigmoid(_dot(xn, wg_ref[...]))
    pe = _dot(p_ref[...].astype(BF16), wp_ref[...])
    out = h + gate * pe
    if final:
        out = _rms(out, fg_ref[...])
    o_ref[...] = out


def _ple(h, g, p, wg, wp, fg, layer, final, tm=512):
    t, d = h.shape
    pd = p.shape[-1]
    return pl.pallas_call(
        functools.partial(_ple_body, final=final),
        grid=(t // tm,),
        in_specs=[
            pl.BlockSpec((tm, d), lambda i: (i, 0)),
            pl.BlockSpec((1, d), lambda i: (0, 0)),
            pl.BlockSpec((None, tm, pd), lambda i: (layer, i, 0)),
            pl.BlockSpec((None, d, d), lambda i: (layer, 0, 0)),
            pl.BlockSpec((None, pd, d), lambda i: (layer, 0, 0)),
            pl.BlockSpec((1, d), lambda i: (0, 0)),
        ],
        out_specs=pl.BlockSpec((tm, d), lambda i: (i, 0)),
        out_shape=jax.ShapeDtypeStruct((t, d), F32),
        compiler_params=_cparams(("parallel",)),
        name="ple",
    )(h, g, p, wg, wp, fg)


GMLP_CHUNKS = 2


def _gmlp_body(u_ref, v_ref, gain_ref, ws_ref, bs_ref, o_ref):
    groups = ws_ref.shape[0]
    tril, _ = _tri_masks(A_CHUNK)
    for g in range(groups):
        w = jnp.where(tril, ws_ref[g], 0.0).astype(BF16)
        bias = bs_ref[g]
        cols = slice(g * A_CHUNK, (g + 1) * A_CHUNK)
        for c in range(GMLP_CHUNKS):
            rows = slice(c * A_CHUNK, (c + 1) * A_CHUNK)
            u = _gelu_tanh(u_ref[rows, cols])
            v = _gelu_tanh(v_ref[rows, cols])
            vg = _rms(v, gain_ref[:, cols])
            s = _dot(w, vg.astype(BF16)) + bias
            o_ref[rows, cols] = (u * s).astype(o_ref.dtype)


def _gmlp(proj, gain, ws, bs, layer, a_width):
    t = proj.shape[0]
    groups = a_width // A_CHUNK
    tm = GMLP_CHUNKS * A_CHUNK
    return pl.pallas_call(
        _gmlp_body,
        grid=(t // tm,),
        in_specs=[
            pl.BlockSpec((tm, a_width), lambda c: (c, 0)),
            pl.BlockSpec((tm, a_width), lambda c: (c, 1)),
            pl.BlockSpec((None, 1, a_width), lambda c: (layer, 0, 0)),
            pl.BlockSpec((None, groups, A_CHUNK, A_CHUNK), lambda c: (layer, 0, 0, 0)),
            pl.BlockSpec((None, groups, A_CHUNK, 1), lambda c: (layer, 0, 0, 0)),
        ],
        out_specs=pl.BlockSpec((tm, a_width), lambda c: (c, 0)),
        out_shape=jax.ShapeDtypeStruct((t, a_width), BF16),
        compiler_params=_cparams(("parallel",)),
        name="gmlp",
    )(proj, proj, gain, ws, bs)


HG_CHUNK = 64
HG_SUB = 16


HG_HEADS = 8


def _hgrn2_body(q_ref, f_ref, i_ref, g_ref, lbl_ref, on_ref, o_ref, st_ref, *, layer):
    c = pl.program_id(2)

    @pl.when(c == 0)
    def _():
        st_ref[...] = jnp.zeros_like(st_ref)

    logits = lbl_ref[...]
    e = jnp.exp(logits - jnp.max(logits, axis=0, keepdims=True))
    probs = e / jnp.sum(e, axis=0, keepdims=True)
    lb = jnp.zeros((1, logits.shape[1]), F32)
    for r in range(1, layer + 1):
        lb = lb + probs[r:r + 1, :]

    n = HG_CHUNK
    hs = range(HG_HEADS)
    cols = [slice(hd * B_HEAD, (hd + 1) * B_HEAD) for hd in hs]
    tril, _ = _tri_masks(n)
    tri = jnp.where(tril, 1.0, 0.0).astype(BF16)

    st = [st_ref[hd] for hd in hs]
    v = [i_ref[:, c] for c in cols]
    f = [lb[:, c] + (1.0 - lb[:, c]) * _sigmoid(f_ref[:, c]) for c in cols]
    kf = [1.0 - x for x in f]
    qf = [_silu(q_ref[:, c]) for c in cols]
    cum = [_cumsum_rows(tri, jnp.log(jnp.maximum(x, B_MIN_F))) for x in f]
    last = [x[n - 1:n, :] for x in cum]

    v_b = [x.astype(BF16) for x in v]
    o = [_dot((qf[h] * jnp.exp(cum[h])).astype(BF16), st[h].astype(BF16), _NT) for h in hs]
    for h in hs:
        kend = (kf[h] * jnp.exp(last[h] - cum[h])).astype(BF16)
        st_ref[h] = st[h] * jnp.exp(last[h]) + _dot(v_b[h], kend, _TN)

    nsub = n // HG_SUB
    trow = lax.broadcasted_iota(jnp.int32, (HG_SUB, 1), 0)
    rows = [[] for _ in hs]
    for bi in range(nsub):
        lo, hi = bi * HG_SUB, (bi + 1) * HG_SUB
        acc = [jnp.zeros((HG_SUB, B_HEAD), F32) for _ in hs]
        if bi > 0:
            ref = [cum[h][lo - 1:lo, :] for h in hs]
            qh = [(qf[h][lo:hi] * jnp.exp(cum[h][lo:hi] - ref[h])).astype(BF16) for h in hs]
            kh = [(kf[h][:lo] * jnp.exp(ref[h] - cum[h][:lo])).astype(BF16) for h in hs]
            att = [_dot(qh[h], kh[h], _NT).astype(BF16) for h in hs]
            acc = [_dot(att[h], v_b[h][:lo]) for h in hs]
        for s in range(HG_SUB):
            for h in hs:
                c_b = cum[h][lo:hi]
                dec = jnp.exp(jnp.minimum(c_b - c_b[s:s + 1, :], 0.0))
                col = jnp.sum(qf[h][lo:hi] * dec * kf[h][lo + s:lo + s + 1, :], axis=-1, keepdims=True)
                col = jnp.where(trow >= s, col, 0.0)
                acc[h] = acc[h] + col * v[h][lo + s:lo + s + 1, :]
        for h in hs:
            rows[h].append(acc[h])

    for h in hs:
        out = o[h] + jnp.concatenate(rows[h], axis=0)
        o_ref[:, cols[h]] = (_rms(out, on_ref[:, cols[h]]) * _silu(g_ref[:, cols[h]])).astype(o_ref.dtype)


def _hgrn2(proj, lb_logits, onorm, layer, elayer, bsz, seq, a_width, b_width):
    t = proj.shape[0]
    heads = b_width // B_HEAD
    hgroups = heads // HG_HEADS
    nchunk = seq // HG_CHUNK
    width = HG_HEADS * B_HEAD
    off = 2 * a_width // width

    def col(which):
        return lambda b, h, c: (b * nchunk + c, off + which * hgroups + h)

    blk = (HG_CHUNK, width)
    depth = lb_logits.shape[0]
    return pl.pallas_call(
        functools.partial(_hgrn2_body, layer=layer),
        grid=(bsz, hgroups, nchunk),
        in_specs=[
            pl.BlockSpec(blk, col(0)),
            pl.BlockSpec(blk, col(1)),
            pl.BlockSpec(blk, col(2)),
            pl.BlockSpec(blk, col(3)),
            pl.BlockSpec((depth, width), lambda b, h, c: (0, h)),
            pl.BlockSpec((None, 1, width), lambda b, h, c: (elayer, 0, h)),
        ],
        out_specs=pl.BlockSpec(blk, lambda b, h, c: (b * nchunk + c, h)),
        out_shape=jax.ShapeDtypeStruct((t, b_width), BF16),
        scratch_shapes=[pltpu.VMEM((HG_HEADS, B_HEAD, B_HEAD), F32)],
        compiler_params=_cparams(("parallel", "parallel", "arbitrary")),
        name="hgrn2",
    )(proj, proj, proj, proj, lb_logits, onorm)


MIX_R, MIX_W, MIX_K, MIX_V, MIX_A, MIX_G = range(6)
RW_DECAY_SCALE = 0.6065306597126334
PROJ_SUB = 128


def _rwkv_proj_body(*refs, tm, seq, ntile, vres):
    if vres:
        (x_ref, xp_ref, g_ref, mix_ref, w_ref, w1_ref, a1_ref, g1_ref, l_ref, b_ref,
         v1_ref, v2_ref, v0_ref, vf_ref, o_ref, lag_ref, xs_ref, hs_ref, hv_ref) = refs
    else:
        (x_ref, xp_ref, g_ref, mix_ref, w_ref, w1_ref, a1_ref, g1_ref, l_ref, b_ref,
         o_ref, lag_ref, xs_ref, hs_ref) = refs
    i = pl.program_id(0)
    j = pl.program_id(1)

    @pl.when(j == 0)
    def _():
        g = g_ref[...]
        for r0 in range(0, tm, PROJ_SUB):
            rows = slice(r0, r0 + PROJ_SUB)
            hn = _rms(x_ref[rows, :], g)
            if r0 == 0:
                hp = _rms(xp_ref[...], g)[SUBLANES - 1:SUBLANES, :]
                hp = jnp.where((i * tm) % seq == 0, 0.0, hp)
            else:
                hp = _rms(x_ref[r0 - SUBLANES:r0, :], g)[SUBLANES - 1:SUBLANES, :]
            row = lax.broadcasted_iota(jnp.int32, hn.shape, 0)
            xx = jnp.where(row == 0, hp, pltpu.roll(hn, 1, 0)) - hn

            def mixed(m):
                return (hn + xx * mix_ref[m:m + 1, :]).astype(BF16)

            xs_ref[0, rows, :] = mixed(MIX_R)
            xs_ref[1, rows, :] = mixed(MIX_K)
            xv = mixed(MIX_V)
            xs_ref[2, rows, :] = xv
            hs_ref[0, rows, :] = jnp.tanh(_dot(mixed(MIX_W), w1_ref[...])).astype(BF16)
            hs_ref[1, rows, :] = _dot(mixed(MIX_A), a1_ref[...]).astype(BF16)
            hs_ref[2, rows, :] = _sigmoid(_dot(mixed(MIX_G), g1_ref[...])).astype(BF16)
            if vres:
                hv_ref[rows, :] = _dot(xv, v1_ref[...]).astype(BF16)

    third = j // ntile
    y = _dot(xs_ref[third], w_ref[...])
    z = b_ref[...] + _dot(hs_ref[third], l_ref[...])

    s = _sigmoid(z)
    lag_ref[...] = jnp.where(third == 2, z, s * jnp.where(third == 0, -RW_DECAY_SCALE, 1.0))
    if vres:
        @pl.when(third == 2)
        def _():
            mv = _sigmoid(v0_ref[...] + _dot(hv_ref[...], v2_ref[...]))
            o_ref[...] = y + (vf_ref[...] - y) * mv

        @pl.when(third != 2)
        def _():
            o_ref[...] = y
    else:
        o_ref[...] = y


def _rwkv_proj(h, g, mix, wcat, w1, a1, g1, lcat, bcat, layer, seq, vres=None, tm=1024, tn=512):
    t, d = h.shape
    tm = min(tm, t)
    ntile = d // tn
    rank = w1.shape[-1]

    def lora(lyr):
        return pl.BlockSpec((None, d, rank), lambda i, j: (lyr, 0, 0))

    def vtile(i, j):
        return jnp.maximum(j - 2 * ntile, 0)

    args = [h, h, g, mix, wcat, w1, a1, g1, lcat, bcat]
    in_specs = [
        pl.BlockSpec((tm, d), lambda i, j: (i, 0)),
        pl.BlockSpec((SUBLANES, d), lambda i, j: (jnp.maximum(i * (tm // SUBLANES) - 1, 0), 0)),
        pl.BlockSpec((1, d), lambda i, j: (0, 0)),
        pl.BlockSpec((None, mix.shape[1], d), lambda i, j: (layer, 0, 0)),
        pl.BlockSpec((None, d, tn), lambda i, j: (layer, 0, j)),
        lora(layer), lora(layer), lora(layer),
        pl.BlockSpec((None, rank, tn), lambda i, j: (layer, 0, j)),
        pl.BlockSpec((None, 1, tn), lambda i, j: (layer, 0, j)),
    ]
    scratch = [pltpu.VMEM((3, tm, d), BF16), pltpu.VMEM((3, tm, rank), BF16)]
    if vres is not None:
        v1, v2, v0, rkv_first, vl = vres
        vrank = v1.shape[-1]
        args += [v1, v2, v0, rkv_first]
        in_specs += [
            pl.BlockSpec((None, d, vrank), lambda i, j: (vl, 0, 0)),
            pl.BlockSpec((None, vrank, tn), lambda i, j: (vl, 0, vtile(i, j))),
            pl.BlockSpec((None, 1, tn), lambda i, j: (vl, 0, vtile(i, j))),
            pl.BlockSpec((tm, tn), lambda i, j: (i, 2 * ntile + vtile(i, j))),
        ]
        scratch.append(pltpu.VMEM((tm, vrank), BF16))
    tile = pl.BlockSpec((tm, tn), lambda i, j: (i, j))
    return pl.pallas_call(
        functools.partial(_rwkv_proj_body, tm=tm, seq=seq, ntile=ntile, vres=vres is not None),
        grid=(t // tm, 3 * ntile),
        in_specs=in_specs,
        out_specs=[tile, tile],
        out_shape=[jax.ShapeDtypeStruct((t, 3 * d), F32), jax.ShapeDtypeStruct((t, 3 * d), F32)],
        scratch_shapes=scratch,
        compiler_params=_cparams(("parallel", "arbitrary")),
        name="rwkv_proj",
    )(*args)


RW_CHUNK = 64


RW_PAIRS = 16


def _head_sum(x, m0):
    s0 = jnp.sum(jnp.where(m0, x, 0.0), axis=-1, keepdims=True)
    s1 = jnp.sum(jnp.where(m0, 0.0, x), axis=-1, keepdims=True)
    return jnp.where(m0, s0, s1)


def _rwkv_scan_body(r_ref, k_ref, v_ref, lw_ref, a_ref, g_ref, kk_ref, ka_ref, rk_ref, gg_ref, gb_ref,
                    o_ref, st_ref):
    c = pl.program_id(2)

    @pl.when(c == 0)
    def _():
        st_ref[...] = jnp.zeros_like(st_ref)

    n = RW_CHUNK
    pairs = range(RW_PAIRS)
    cols = [slice(p * LANES, (p + 1) * LANES) for p in pairs]
    lane = lax.broadcasted_iota(jnp.int32, (1, LANES), 1)
    m0 = lane < C_HEAD
    m1 = jnp.logical_not(m0)
    tril, _ = _tri_masks(n)
    tri = jnp.where(tril, 1.0, 0.0).astype(BF16)
    row = lax.broadcasted_iota(jnp.int32, (LANES, LANES), 0)
    colm = lax.broadcasted_iota(jnp.int32, (LANES, LANES), 1)
    bdiag = (row < C_HEAD) == (colm < C_HEAD)

    def hsel(x, hd):
        return jnp.where(m0 if hd == 0 else m1, x, jnp.zeros_like(x))

    st = [st_ref[p] for p in pairs]
    r = [r_ref[:, c] for c in cols]
    k = [k_ref[:, c] for c in cols]
    v = [v_ref[:, c] for c in cols]
    asig = [a_ref[:, c] for c in cols]

    cum = [_cumsum_rows(tri, lw_ref[:, c]) for c in cols]
    last = [x[n - 1:n, :] for x in cum]
    kkr = [k[p] * kk_ref[:, cols[p]] for p in pairs]
    kk = [x / jnp.maximum(jnp.sqrt(_head_sum(x * x, m0)), 1e-12) for x in kkr]
    kmod = [k[p] * (1.0 + (asig[p] - 1.0) * ka_ref[:, cols[p]]) for p in pairs]
    b = [kk[p] * asig[p] for p in pairs]
    bonus = [_head_sum(r[p] * kmod[p] * rk_ref[:, cols[p]], m0) * v[p] for p in pairs]
    rt = [(r[p] * jnp.exp(cum[p])).astype(BF16) for p in pairs]
    at = [(-kk[p] * jnp.exp(cum[p] - lw_ref[:, cols[p]])).astype(BF16) for p in pairs]
    einv = [jnp.exp(-x) for x in cum]
    bt = [(b[p] * einv[p]).astype(BF16) for p in pairs]
    kt = [(kmod[p] * einv[p]).astype(BF16) for p in pairs]
    eend = [jnp.exp(last[p] - cum[p]) for p in pairs]
    v_b = [x.astype(BF16) for x in v]
    st_b = [x.astype(BF16) for x in st]

    heads = [(p, hd) for p in pairs for hd in range(2)]
    _, stril = _tri_masks(n)
    eye = jnp.where(tril & jnp.logical_not(stril), 1.0, 0.0)
    at_h = [hsel(at[p], hd) for p, hd in heads]
    rt_h = [hsel(rt[p], hd) for p, hd in heads]
    pw = [jnp.where(stril, _dot(at_h[i], bt[p], _NT), 0.0) for i, (p, hd) in enumerate(heads)]
    t_inv = [eye + x for x in pw]
    fill = {}
    fillers = [
        lambda: fill.update(a_ak=[jnp.where(stril, _dot(at_h[i], kt[p], _NT), 0.0).astype(BF16)
                                  for i, (p, hd) in enumerate(heads)]),
        lambda: fill.update(a_rb=[jnp.where(tril, _dot(rt_h[i], bt[p], _NT), 0.0).astype(BF16)
                                  for i, (p, hd) in enumerate(heads)]),
        lambda: fill.update(a_rk=[jnp.where(tril, _dot(rt_h[i], kt[p], _NT), 0.0).astype(BF16)
                                  for i, (p, hd) in enumerate(heads)]),
        lambda: fill.update(av=[_dot(fill["a_ak"][i], v_b[p]) for i, (p, hd) in enumerate(heads)]),
        lambda: fill.update(y_v=[_dot(fill["a_rk"][i], v_b[p]) for i, (p, hd) in enumerate(heads)]),
    ]
    step = 2
    while step < n:
        pw_b = [x.astype(BF16) for x in pw]
        pw = [_dot(x, x) for x in pw_b]
        if fillers:
            fillers.pop(0)()
        t_inv = [t + _dot(t.astype(BF16), x.astype(BF16)) for t, x in zip(t_inv, pw)]
        step *= 2
    for filler in fillers:
        filler()
    t_inv = [x.astype(BF16) for x in t_inv]
    a_rb, av, y_v = fill["a_rb"], fill["av"], fill["y_v"]

    x = [(_dot(at[p], st_b[p], _NT) + jnp.where(m0, av[2 * p], av[2 * p + 1])).astype(BF16) for p in pairs]
    u = [jnp.where(m0, _dot(t_inv[2 * p], x[p]), _dot(t_inv[2 * p + 1], x[p])) for p in pairs]
    u_b = [t.astype(BF16) for t in u]
    y = [_dot(rt[p], st_b[p], _NT)
         + jnp.where(m0, _dot(a_rb[2 * p], u_b[p]) + y_v[2 * p], _dot(a_rb[2 * p + 1], u_b[p]) + y_v[2 * p + 1])
         for p in pairs]

    for p in pairs:
        uv = jnp.concatenate([u_b[p], v_b[p]], axis=0)
        bk = jnp.concatenate([(b[p] * eend[p]).astype(BF16), (kmod[p] * eend[p]).astype(BF16)], axis=0)
        st_ref[p] = st[p] * jnp.exp(last[p]) + jnp.where(bdiag, _dot(uv, bk, _TN), 0.0)

    inv_n = 1.0 / C_HEAD
    dy = [y[p] - _head_sum(y[p], m0) * inv_n for p in pairs]
    var = [_head_sum(t * t, m0) * inv_n for t in dy]
    for p in pairs:
        yn = dy[p] * lax.rsqrt(var[p] + C_GN_EPS) * gg_ref[:, cols[p]] + gb_ref[:, cols[p]]
        o_ref[:, cols[p]] = ((yn + bonus[p]) * g_ref[:, cols[p]]).astype(o_ref.dtype)


def _rwkv_scan(rkv, lag, kk, ka, rk, gg, gb, layer, bsz, seq):
    t = rkv.shape[0]
    d = rkv.shape[1] // 3
    nchunk = seq // RW_CHUNK
    width = RW_PAIRS * LANES
    nblk = d // width

    def third(which):
        return pl.BlockSpec((RW_CHUNK, width), lambda b, p, c: (b * nchunk + c, which * nblk + p))

    blk = third(0)
    vec = pl.BlockSpec((None, 1, width), lambda b, p, c: (layer, 0, p))
    r, k, v, lw, a, g = rkv, rkv, rkv, lag, lag, lag
    return pl.pallas_call(
        _rwkv_scan_body,
        grid=(bsz, nblk, nchunk),
        in_specs=[third(0), third(1), third(2)] * 2 + [vec] * 5,
        out_specs=blk,
        out_shape=jax.ShapeDtypeStruct((t, d), BF16),
        scratch_shapes=[pltpu.VMEM((RW_PAIRS, LANES, LANES), F32)],
        compiler_params=_cparams(("parallel", "parallel", "arbitrary")),
        name="rwkv_scan",
    )(r, k, v, lw, a, g, kk, ka, rk, gg, gb)


def _pad_cols(w, n):
    return jnp.pad(w, ((0, 0), (0, 0), (0, n - w.shape[-1])))


def _pad_rows(w, n):
    return jnp.pad(w, ((0, 0), (0, n - w.shape[1]), (0, 0)))


def kernel(x, p, norms, final_norm, ffn_wg, ffn_wu, ffn_wd, ple_wp, ple_wg, e_w_in, e_w_out, a_vnorm, a_ws, a_bs, b_onorm, b_lb_logits, c_mix, c_wr, c_wk, c_wv, c_wo, c_w0, c_w1, c_w2, c_a0, c_a1, c_a2, c_g1, c_g2, c_kk, c_ka, c_rk, c_gn_g, c_gn_b, c_v0, c_v1, c_v2):
    bsz, seq, d = x.shape
    depth = p.shape[0]
    t = bsz * seq
    a_width = a_vnorm.shape[-1]
    b_width = b_onorm.shape[-1]

    bf = lambda w: w.astype(BF16)
    wg_b, wu_b, wd_b = ffn_wg, ffn_wu, ffn_wd
    ple_wp_b, ple_wg_b = bf(ple_wp), bf(ple_wg)
    e_in_b = e_w_in
    e_out_b = bf(e_w_out)
    wo_b = bf(c_wo)
    rkv_b = bf(jnp.concatenate([c_wr, c_wk, c_wv], axis=-1))
    rank = c_g1.shape[-1]
    w1_b = bf(_pad_cols(c_w1, rank))
    a1_b = bf(_pad_cols(c_a1, rank))
    g1_b = bf(c_g1)
    lag_b = bf(jnp.concatenate([_pad_rows(c_w2, rank), _pad_rows(c_a2, rank), c_g2], axis=-1))
    bias_b = jnp.concatenate([c_w0, c_a0, jnp.zeros_like(c_w0)], axis=-1).reshape(c_w0.shape[0], 1, 3 * d)
    v1_b = bf(_pad_cols(c_v1, LORA_PAD))
    v2_b = bf(_pad_rows(c_v2, LORA_PAD))

    vec3 = lambda w: w.reshape(w.shape[0], 1, -1)
    a_vnorm3, b_onorm3 = vec3(a_vnorm), vec3(b_onorm)
    a_bs4 = a_bs.reshape(a_bs.shape + (1,))
    v0_3 = vec3(c_v0)
    kk3, ka3, rk3, gg3, gb3 = vec3(c_kk), vec3(c_ka), vec3(c_rk), vec3(c_gn_g), vec3(c_gn_b)
    fg = final_norm.reshape(1, d)

    h = x.reshape(t, d)
    p2 = p.reshape(depth, t, p.shape[-1])
    rkv_first = None
    for i in range(depth):
        j = i // 2
        h = _ffn(h, norms[i, 0].reshape(1, d), wg_b, wu_b, wd_b, i, 0)
        g1n = norms[i, 1].reshape(1, d)
        if i % 2 == 0:
            proj = _nmm(h, g1n, e_in_b, j)
            a_out = _gmlp(proj, a_vnorm3, a_ws, a_bs4, j, a_width)
            b_out = _hgrn2(proj, b_lb_logits, b_onorm3, i, j, bsz, seq, a_width, b_width)
            h = _mm2_res(h, a_out, b_out, e_out_b, j)
        else:
            vres = None if j == 0 else (v1_b, v2_b, v0_3, rkv_first, j - 1)
            rkv, lag = _rwkv_proj(h, g1n, c_mix, rkv_b, w1_b, a1_b, g1_b, lag_b, bias_b, j, seq, vres=vres)
            if j == 0:
                rkv_first = rkv
            y = _rwkv_scan(rkv, lag, kk3, ka3, rk3, gg3, gb3, j, bsz, seq)
            h = _mm_res(h, y, wo_b, j)
        h = _ffn(h, norms[i, 2].reshape(1, d), wg_b, wu_b, wd_b, i, 1)
        h = _ple(h, norms[i, 3].reshape(1, d), p2, ple_wg_b, ple_wp_b, fg, i, final=(i == depth - 1))
    return h.reshape(bsz, seq, d)
```

```python
import functools

import jax
import jax.numpy as jnp
from jax import lax
from jax.experimental import pallas as pl
from jax.experimental.pallas import tpu as pltpu

F32 = jnp.float32
BF16 = jnp.bfloat16

LANES = 128
SUBLANES = 8
RMS_EPS = 1e-6
A_CHUNK = 128
B_HEAD = 128
B_MIN_F = 1e-30
C_HEAD = 64
C_GN_EPS = 64e-5
LORA_PAD = 128

VMEM_LIMIT = 56 * 1024 * 1024

_NN = ((1,), (0,))
_NT = ((1,), (1,))
_TN = ((0,), (0,))


def _cparams(sem):
    return pltpu.CompilerParams(dimension_semantics=sem, vmem_limit_bytes=VMEM_LIMIT)


def _rms(x, g, eps=RMS_EPS):
    return x * lax.rsqrt(jnp.mean(x * x, axis=-1, keepdims=True) + eps) * g


def _sigmoid(x):
    return 1.0 / (1.0 + jnp.exp(-x))


def _silu(x):
    return x * _sigmoid(x)


def _gelu_tanh(x):
    return 0.5 * x * (1.0 + jnp.tanh(0.7978845608028654 * (x + 0.044715 * (x * x * x))))


def _dot(a, b, dims=_NN):
    return lax.dot_general(a, b, (dims, ((), ())), preferred_element_type=F32)


def _cumsum_rows(tri_bf16, x):
    hi = x.astype(BF16)
    r1 = x - hi.astype(F32)
    mid = r1.astype(BF16)
    lo = (r1 - mid.astype(F32)).astype(BF16)
    return _dot(tri_bf16, hi) + (_dot(tri_bf16, mid) + _dot(tri_bf16, lo))


def _tri_masks(n):
    row = lax.broadcasted_iota(jnp.int32, (n, n), 0)
    col = lax.broadcasted_iota(jnp.int32, (n, n), 1)
    return col <= row, col < row


FFN_DOWN_COLS = 512


def _ffn_body(x_ref, g_ref, wg_ref, wu_ref, wd_ref, o_ref, xn_ref):
    j = pl.program_id(1)

    @pl.when(j == 0)
    def _():
        x = x_ref[...]
        xn_ref[...] = _rms(x, g_ref[...]).astype(BF16)
        o_ref[...] = x

    xn = xn_ref[...]
    gate = _dot(xn, wg_ref[...].astype(BF16))
    up = _dot(xn, wu_ref[...].astype(BF16))
    hid = (0.5 * _silu(gate) * up).astype(BF16)
    for c in range(0, o_ref.shape[1], FFN_DOWN_COLS):
        cols = slice(c, c + FFN_DOWN_COLS)
        o_ref[:, cols] += _dot(hid, wd_ref[:, cols].astype(BF16))


def _ffn(h, g, wg, wu, wd, layer, half, tm=1024, tf=256):
    t, d = h.shape
    f = wg.shape[-1]
    tm = min(tm, t)
    return pl.pallas_call(
        _ffn_body,
        grid=(t // tm, f // tf),
        in_specs=[
            pl.BlockSpec((tm, d), lambda i, j: (i, 0)),
            pl.BlockSpec((1, d), lambda i, j: (0, 0)),
            pl.BlockSpec((None, None, d, tf), lambda i, j: (layer, half, 0, j)),
            pl.BlockSpec((None, None, d, tf), lambda i, j: (layer, half, 0, j)),
            pl.BlockSpec((None, None, tf, d), lambda i, j: (layer, half, j, 0)),
        ],
        out_specs=pl.BlockSpec((tm, d), lambda i, j: (i, 0)),
        out_shape=jax.ShapeDtypeStruct((t, d), F32),
        scratch_shapes=[pltpu.VMEM((tm, d), BF16)],
        compiler_params=_cparams(("parallel", "arbitrary")),
        name="ffn",
    )(h, g, wg, wu, wd)


def _mm_res_body(h_ref, x_ref, w_ref, o_ref):
    o_ref[...] = h_ref[...] + _dot(x_ref[...], w_ref[...].astype(BF16))


def _mm_res(h, x, w, layer, tm=512, tn=2048):
    t, k = x.shape
    n = w.shape[-1]
    tm = min(tm, t)
    return pl.pallas_call(
        _mm_res_body,
        grid=(t // tm, n // tn),
        in_specs=[
            pl.BlockSpec((tm, tn), lambda i, j: (i, j)),
            pl.BlockSpec((tm, k), lambda i, j: (i, 0)),
            pl.BlockSpec((None, k, tn), lambda i, j: (layer, 0, j)),
        ],
        out_specs=pl.BlockSpec((tm, tn), lambda i, j: (i, j)),
        out_shape=jax.ShapeDtypeStruct((t, n), F32),
        compiler_params=_cparams(("parallel", "parallel")),
        name="mm_res",
    )(h, x, w)


def _mm2_res_body(h_ref, xa_ref, xb_ref, wa_ref, wb_ref, o_ref):
    o_ref[...] = h_ref[...] + (_dot(xa_ref[...], wa_ref[...].astype(BF16))
                               + _dot(xb_ref[...], wb_ref[...].astype(BF16)))


def _mm2_res(h, xa, xb, w, layer, tm=512, tn=2048):
    t, ka = xa.shape
    kb = xb.shape[1]
    n = w.shape[-1]
    assert ka == kb
    tm = min(tm, t)
    return pl.pallas_call(
        _mm2_res_body,
        grid=(t // tm, n // tn),
        in_specs=[
            pl.BlockSpec((tm, tn), lambda i, j: (i, j)),
            pl.BlockSpec((tm, ka), lambda i, j: (i, 0)),
            pl.BlockSpec((tm, kb), lambda i, j: (i, 0)),
            pl.BlockSpec((None, ka, tn), lambda i, j: (layer, 0, j)),
            pl.BlockSpec((None, kb, tn), lambda i, j: (layer, 1, j)),
        ],
        out_specs=pl.BlockSpec((tm, tn), lambda i, j: (i, j)),
        out_shape=jax.ShapeDtypeStruct((t, n), F32),
        compiler_params=_cparams(("parallel", "parallel")),
        name="mm2_res",
    )(h, xa, xb, w, w)


def _nmm_body(x_ref, g_ref, w_ref, o_ref, xn_ref):
    @pl.when(pl.program_id(1) == 0)
    def _():
        xn_ref[...] = _rms(x_ref[...], g_ref[...]).astype(BF16)

    o_ref[...] = _dot(xn_ref[...], w_ref[...].astype(BF16))


def _nmm(h, g, w, layer, tm=1024, tn=1024):
    t, d = h.shape
    n = w.shape[-1]
    tm = min(tm, t)
    return pl.pallas_call(
        _nmm_body,
        grid=(t // tm, n // tn),
        in_specs=[
            pl.BlockSpec((tm, d), lambda i, j: (i, 0)),
            pl.BlockSpec((1, d), lambda i, j: (0, 0)),
            pl.BlockSpec((None, d, tn), lambda i, j: (layer, 0, j)),
        ],
        out_specs=pl.BlockSpec((tm, tn), lambda i, j: (i, j)),
        out_shape=jax.ShapeDtypeStruct((t, n), F32),
        scratch_shapes=[pltpu.VMEM((tm, d), BF16)],
        compiler_params=_cparams(("parallel", "arbitrary")),
        name="norm_mm",
    )(h, g, w)


def _ple_body(h_ref, g_ref, p_ref, wg_ref, wp_ref, fg_ref, o_ref, *, final):
    h = h_ref[...]
    xn = _rms(h, g_ref[...]).astype(BF16)
    gate = _sigmoid(_dot(xn, wg_ref[...].astype(BF16)))
    pe = _dot(p_ref[...].astype(BF16), wp_ref[...].astype(BF16))
    out = h + gate * pe
    if final:
        out = _rms(out, fg_ref[...])
    o_ref[...] = out


def _ple(h, g, p, wg, wp, fg, layer, final, tm=512):
    t, d = h.shape
    pd = p.shape[-1]
    return pl.pallas_call(
        functools.partial(_ple_body, final=final),
        grid=(t // tm,),
        in_specs=[
            pl.BlockSpec((tm, d), lambda i: (i, 0)),
            pl.BlockSpec((1, d), lambda i: (0, 0)),
            pl.BlockSpec((None, tm, pd), lambda i: (layer, i, 0)),
            pl.BlockSpec((None, d, d), lambda i: (layer, 0, 0)),
            pl.BlockSpec((None, pd, d), lambda i: (layer, 0, 0)),
            pl.BlockSpec((1, d), lambda i: (0, 0)),
        ],
        out_specs=pl.BlockSpec((tm, d), lambda i: (i, 0)),
        out_shape=jax.ShapeDtypeStruct((t, d), F32),
        compiler_params=_cparams(("parallel",)),
        name="ple",
    )(h, g, p, wg, wp, fg)


GMLP_CHUNKS = 4


def _gmlp_body(u_ref, v_ref, gain_ref, ws_ref, bs_ref, o_ref):
    groups = ws_ref.shape[0]
    tril, _ = _tri_masks(A_CHUNK)
    for g in range(groups):
        w = jnp.where(tril, ws_ref[g], 0.0).astype(BF16)
        bias = bs_ref[g]
        cols = slice(g * A_CHUNK, (g + 1) * A_CHUNK)
        for c in range(GMLP_CHUNKS):
            rows = slice(c * A_CHUNK, (c + 1) * A_CHUNK)
            u = _gelu_tanh(u_ref[rows, cols])
            v = _gelu_tanh(v_ref[rows, cols])
            vg = _rms(v, gain_ref[:, cols])
            s = _dot(w, vg.astype(BF16)) + bias
            o_ref[rows, cols] = (u * s).astype(o_ref.dtype)


def _gmlp(proj, gain, ws, bs, layer, a_width):
    t = proj.shape[0]
    groups = a_width // A_CHUNK
    tm = GMLP_CHUNKS * A_CHUNK
    return pl.pallas_call(
        _gmlp_body,
        grid=(t // tm,),
        in_specs=[
            pl.BlockSpec((tm, a_width), lambda c: (c, 0)),
            pl.BlockSpec((tm, a_width), lambda c: (c, 1)),
            pl.BlockSpec((None, 1, a_width), lambda c: (layer, 0, 0)),
            pl.BlockSpec((None, groups, A_CHUNK, A_CHUNK), lambda c: (layer, 0, 0, 0)),
            pl.BlockSpec((None, groups, A_CHUNK, 1), lambda c: (layer, 0, 0, 0)),
        ],
        out_specs=pl.BlockSpec((tm, a_width), lambda c: (c, 0)),
        out_shape=jax.ShapeDtypeStruct((t, a_width), BF16),
        compiler_params=_cparams(("parallel",)),
        name="gmlp",
    )(proj, proj, gain, ws, bs)


HG_CHUNK = 64
HG_SUB = 16


HG_HEADS = 8


def _hgrn2_body(q_ref, f_ref, i_ref, g_ref, lbl_ref, on_ref, o_ref, st_ref, *, layer):
    c = pl.program_id(2)

    @pl.when(c == 0)
    def _():
        st_ref[...] = jnp.zeros_like(st_ref)

    logits = lbl_ref[...]
    e = jnp.exp(logits - jnp.max(logits, axis=0, keepdims=True))
    probs = e / jnp.sum(e, axis=0, keepdims=True)
    lb = jnp.zeros((1, logits.shape[1]), F32)
    for r in range(1, layer + 1):
        lb = lb + probs[r:r + 1, :]

    n = HG_CHUNK
    hs = range(HG_HEADS)
    cols = [slice(hd * B_HEAD, (hd + 1) * B_HEAD) for hd in hs]
    tril, _ = _tri_masks(n)
    tri = jnp.where(tril, 1.0, 0.0).astype(BF16)

    st = [st_ref[hd] for hd in hs]
    v = [i_ref[:, c] for c in cols]
    f = [lb[:, c] + (1.0 - lb[:, c]) * _sigmoid(f_ref[:, c]) for c in cols]
    kf = [1.0 - x for x in f]
    qf = [_silu(q_ref[:, c]) for c in cols]
    cum = [_cumsum_rows(tri, jnp.log(jnp.maximum(x, B_MIN_F))) for x in f]
    last = [x[n - 1:n, :] for x in cum]

    v_b = [x.astype(BF16) for x in v]
    o = [_dot((qf[h] * jnp.exp(cum[h])).astype(BF16), st[h].astype(BF16), _NT) for h in hs]
    for h in hs:
        kend = (kf[h] * jnp.exp(last[h] - cum[h])).astype(BF16)
        st_ref[h] = st[h] * jnp.exp(last[h]) + _dot(v_b[h], kend, _TN)

    nsub = n // HG_SUB
    trow = lax.broadcasted_iota(jnp.int32, (HG_SUB, 1), 0)
    rows = [[] for _ in hs]
    for bi in range(nsub):
        lo, hi = bi * HG_SUB, (bi + 1) * HG_SUB
        acc = [jnp.zeros((HG_SUB, B_HEAD), F32) for _ in hs]
        if bi > 0:
            ref = [cum[h][lo - 1:lo, :] for h in hs]
            qh = [(qf[h][lo:hi] * jnp.exp(cum[h][lo:hi] - ref[h])).astype(BF16) for h in hs]
            kh = [(kf[h][:lo] * jnp.exp(ref[h] - cum[h][:lo])).astype(BF16) for h in hs]
            att = [_dot(qh[h], kh[h], _NT).astype(BF16) for h in hs]
            acc = [_dot(att[h], v_b[h][:lo]) for h in hs]
        for s in range(HG_SUB):
            for h in hs:
                c_b = cum[h][lo:hi]
                dec = jnp.exp(jnp.minimum(c_b - c_b[s:s + 1, :], 0.0))
                col = jnp.sum(qf[h][lo:hi] * dec * kf[h][lo + s:lo + s + 1, :], axis=-1, keepdims=True)
                col = jnp.where(trow >= s, col, 0.0)
                acc[h] = acc[h] + col * v[h][lo + s:lo + s + 1, :]
        for h in hs:
            rows[h].append(acc[h])

    for h in hs:
        out = o[h] + jnp.concatenate(rows[h], axis=0)
        o_ref[:, cols[h]] = (_rms(out, on_ref[:, cols[h]]) * _silu(g_ref[:, cols[h]])).astype(o_ref.dtype)


def _hgrn2(proj, lb_logits, onorm, layer, elayer, bsz, seq, a_width, b_width):
    t = proj.shape[0]
    heads = b_width // B_HEAD
    hgroups = heads // HG_HEADS
    nchunk = seq // HG_CHUNK
    width = HG_HEADS * B_HEAD
    off = 2 * a_width // width

    def col(which):
        return lambda b, h, c: (b * nchunk + c, off + which * hgroups + h)

    blk = (HG_CHUNK, width)
    depth = lb_logits.shape[0]
    return pl.pallas_call(
        functools.partial(_hgrn2_body, layer=layer),
        grid=(bsz, hgroups, nchunk),
        in_specs=[
            pl.BlockSpec(blk, col(0)),
            pl.BlockSpec(blk, col(1)),
            pl.BlockSpec(blk, col(2)),
            pl.BlockSpec(blk, col(3)),
            pl.BlockSpec((depth, width), lambda b, h, c: (0, h)),
            pl.BlockSpec((None, 1, width), lambda b, h, c: (elayer, 0, h)),
        ],
        out_specs=pl.BlockSpec(blk, lambda b, h, c: (b * nchunk + c, h)),
        out_shape=jax.ShapeDtypeStruct((t, b_width), BF16),
        scratch_shapes=[pltpu.VMEM((HG_HEADS, B_HEAD, B_HEAD), F32)],
        compiler_params=_cparams(("parallel", "parallel", "arbitrary")),
        name="hgrn2",
    )(proj, proj, proj, proj, lb_logits, onorm)


MIX_R, MIX_W, MIX_K, MIX_V, MIX_A, MIX_G = range(6)
RW_DECAY_SCALE = 0.6065306597126334
PROJ_SUB = 128


def _rwkv_proj_body(*refs, tm, seq, ntile, vres):
    if vres:
        (x_ref, xp_ref, g_ref, mix_ref, w_ref, w1_ref, a1_ref, g1_ref, l_ref, b_ref,
         v1_ref, v2_ref, v0_ref, vf_ref, o_ref, lag_ref, xs_ref, hs_ref, hv_ref) = refs
    else:
        (x_ref, xp_ref, g_ref, mix_ref, w_ref, w1_ref, a1_ref, g1_ref, l_ref, b_ref,
         o_ref, lag_ref, xs_ref, hs_ref) = refs
    i = pl.program_id(0)
    j = pl.program_id(1)

    @pl.when(j == 0)
    def _():
        g = g_ref[...]
        for r0 in range(0, tm, PROJ_SUB):
            rows = slice(r0, r0 + PROJ_SUB)
            hn = _rms(x_ref[rows, :], g)
            if r0 == 0:
                hp = _rms(xp_ref[...], g)[SUBLANES - 1:SUBLANES, :]
                hp = jnp.where((i * tm) % seq == 0, 0.0, hp)
            else:
                hp = _rms(x_ref[r0 - SUBLANES:r0, :], g)[SUBLANES - 1:SUBLANES, :]
            row = lax.broadcasted_iota(jnp.int32, hn.shape, 0)
            xx = jnp.where(row == 0, hp, pltpu.roll(hn, 1, 0)) - hn

            def mixed(m):
                return (hn + xx * mix_ref[m:m + 1, :]).astype(BF16)

            xs_ref[0, rows, :] = mixed(MIX_R)
            xs_ref[1, rows, :] = mixed(MIX_K)
            xv = mixed(MIX_V)
            xs_ref[2, rows, :] = xv
            hs_ref[0, rows, :] = jnp.tanh(_dot(mixed(MIX_W), w1_ref[...])).astype(BF16)
            hs_ref[1, rows, :] = _dot(mixed(MIX_A), a1_ref[...]).astype(BF16)
            hs_ref[2, rows, :] = _sigmoid(_dot(mixed(MIX_G), g1_ref[...])).astype(BF16)
            if vres:
                hv_ref[rows, :] = _dot(xv, v1_ref[...]).astype(BF16)

    third = j // ntile
    y = _dot(xs_ref[third], w_ref[...])
    z = b_ref[...] + _dot(hs_ref[third], l_ref[...])

    s = _sigmoid(z)
    lag_ref[...] = jnp.where(third == 2, z, s * jnp.where(third == 0, -RW_DECAY_SCALE, 1.0))
    if vres:
        @pl.when(third == 2)
        def _():
            mv = _sigmoid(v0_ref[...] + _dot(hv_ref[...], v2_ref[...]))
            o_ref[...] = y + (vf_ref[...] - y) * mv

        @pl.when(third != 2)
        def _():
            o_ref[...] = y
    else:
        o_ref[...] = y


def _rwkv_proj(h, g, mix, wcat, w1, a1, g1, lcat, bcat, layer, seq, vres=None, tm=1024, tn=512):
    t, d = h.shape
    tm = min(tm, t)
    ntile = d // tn
    rank = w1.shape[-1]

    def lora(lyr):
        return pl.BlockSpec((None, d, rank), lambda i, j: (lyr, 0, 0))

    def vtile(i, j):
        return jnp.maximum(j - 2 * ntile, 0)

    args = [h, h, g, mix, wcat, w1, a1, g1, lcat, bcat]
    in_specs = [
        pl.BlockSpec((tm, d), lambda i, j: (i, 0)),
        pl.BlockSpec((SUBLANES, d), lambda i, j: (jnp.maximum(i * (tm // SUBLANES) - 1, 0), 0)),
        pl.BlockSpec((1, d), lambda i, j: (0, 0)),
        pl.BlockSpec((None, mix.shape[1], d), lambda i, j: (layer, 0, 0)),
        pl.BlockSpec((None, d, tn), lambda i, j: (layer, 0, j)),
        lora(layer), lora(layer), lora(layer),
        pl.BlockSpec((None, rank, tn), lambda i, j: (layer, 0, j)),
        pl.BlockSpec((None, 1, tn), lambda i, j: (layer, 0, j)),
    ]
    scratch = [pltpu.VMEM((3, tm, d), BF16), pltpu.VMEM((3, tm, rank), BF16)]
    if vres is not None:
        v1, v2, v0, rkv_first, vl = vres
        vrank = v1.shape[-1]
        args += [v1, v2, v0, rkv_first]
        in_specs += [
            pl.BlockSpec((None, d, vrank), lambda i, j: (vl, 0, 0)),
            pl.BlockSpec((None, vrank, tn), lambda i, j: (vl, 0, vtile(i, j))),
            pl.BlockSpec((None, 1, tn), lambda i, j: (vl, 0, vtile(i, j))),
            pl.BlockSpec((tm, tn), lambda i, j: (i, 2 * ntile + vtile(i, j))),
        ]
        scratch.append(pltpu.VMEM((tm, vrank), BF16))
    tile = pl.BlockSpec((tm, tn), lambda i, j: (i, j))
    return pl.pallas_call(
        functools.partial(_rwkv_proj_body, tm=tm, seq=seq, ntile=ntile, vres=vres is not None),
        grid=(t // tm, 3 * ntile),
        in_specs=in_specs,
        out_specs=[tile, tile],
        out_shape=[jax.ShapeDtypeStruct((t, 3 * d), F32), jax.ShapeDtypeStruct((t, 3 * d), F32)],
        scratch_shapes=scratch,
        compiler_params=_cparams(("parallel", "arbitrary")),
        name="rwkv_proj",
    )(*args)


RW_CHUNK = 64


RW_PAIRS = 16


def _head_sum(x, m0):
    s0 = jnp.sum(jnp.where(m0, x, 0.0), axis=-1, keepdims=True)
    s1 = jnp.sum(jnp.where(m0, 0.0, x), axis=-1, keepdims=True)
    return jnp.where(m0, s0, s1)


def _rwkv_scan_body(r_ref, k_ref, v_ref, lw_ref, a_ref, g_ref, kk_ref, ka_ref, rk_ref, gg_ref, gb_ref,
                    o_ref, st_ref):
    c = pl.program_id(2)

    @pl.when(c == 0)
    def _():
        st_ref[...] = jnp.zeros_like(st_ref)

    n = RW_CHUNK
    pairs = range(RW_PAIRS)
    cols = [slice(p * LANES, (p + 1) * LANES) for p in pairs]
    lane = lax.broadcasted_iota(jnp.int32, (1, LANES), 1)
    m0 = lane < C_HEAD
    m1 = jnp.logical_not(m0)
    tril, _ = _tri_masks(n)
    tri = jnp.where(tril, 1.0, 0.0).astype(BF16)
    row = lax.broadcasted_iota(jnp.int32, (LANES, LANES), 0)
    colm = lax.broadcasted_iota(jnp.int32, (LANES, LANES), 1)
    bdiag = (row < C_HEAD) == (colm < C_HEAD)

    def hsel(x, hd):
        return jnp.where(m0 if hd == 0 else m1, x, jnp.zeros_like(x))

    st = [st_ref[p] for p in pairs]
    r = [r_ref[:, c] for c in cols]
    k = [k_ref[:, c] for c in cols]
    v = [v_ref[:, c] for c in cols]
    asig = [a_ref[:, c] for c in cols]

    cum = [_cumsum_rows(tri, lw_ref[:, c]) for c in cols]
    last = [x[n - 1:n, :] for x in cum]
    kkr = [k[p] * kk_ref[:, cols[p]] for p in pairs]
    kk = [x / jnp.maximum(jnp.sqrt(_head_sum(x * x, m0)), 1e-12) for x in kkr]
    kmod = [k[p] * (1.0 + (asig[p] - 1.0) * ka_ref[:, cols[p]]) for p in pairs]
    b = [kk[p] * asig[p] for p in pairs]
    bonus = [_head_sum(r[p] * kmod[p] * rk_ref[:, cols[p]], m0) * v[p] for p in pairs]
    rt = [(r[p] * jnp.exp(cum[p])).astype(BF16) for p in pairs]
    at = [(-kk[p] * jnp.exp(cum[p] - lw_ref[:, cols[p]])).astype(BF16) for p in pairs]
    einv = [jnp.exp(-x) for x in cum]
    bt = [(b[p] * einv[p]).astype(BF16) for p in pairs]
    kt = [(kmod[p] * einv[p]).astype(BF16) for p in pairs]
    eend = [jnp.exp(last[p] - cum[p]) for p in pairs]
    v_b = [x.astype(BF16) for x in v]
    st_b = [x.astype(BF16) for x in st]

    heads = [(p, hd) for p in pairs for hd in range(2)]
    _, stril = _tri_masks(n)
    eye = jnp.where(tril & jnp.logical_not(stril), 1.0, 0.0)
    at_h = [hsel(at[p], hd) for p, hd in heads]
    rt_h = [hsel(rt[p], hd) for p, hd in heads]
    pw = [jnp.where(stril, _dot(at_h[i], bt[p], _NT), 0.0) for i, (p, hd) in enumerate(heads)]
    t_inv = [eye + x for x in pw]
    fill = {}
    fillers = [
        lambda: fill.update(a_ak=[jnp.where(stril, _dot(at_h[i], kt[p], _NT), 0.0).astype(BF16)
                                  for i, (p, hd) in enumerate(heads)]),
        lambda: fill.update(a_rb=[jnp.where(tril, _dot(rt_h[i], bt[p], _NT), 0.0).astype(BF16)
                                  for i, (p, hd) in enumerate(heads)]),
        lambda: fill.update(a_rk=[jnp.where(tril, _dot(rt_h[i], kt[p], _NT), 0.0).astype(BF16)
                                  for i, (p, hd) in enumerate(heads)]),
        lambda: fill.update(av=[_dot(fill["a_ak"][i], v_b[p]) for i, (p, hd) in enumerate(heads)]),
        lambda: fill.update(y_v=[_dot(fill["a_rk"][i], v_b[p]) for i, (p, hd) in enumerate(heads)]),
    ]
    step = 2
    while step < n:
        pw_b = [x.astype(BF16) for x in pw]
        pw = [_dot(x, x) for x in pw_b]
        if fillers:
            fillers.pop(0)()
        t_inv = [t + _dot(t.astype(BF16), x.astype(BF16)) for t, x in zip(t_inv, pw)]
        step *= 2
    for filler in fillers:
        filler()
    t_inv = [x.astype(BF16) for x in t_inv]
    a_rb, av, y_v = fill["a_rb"], fill["av"], fill["y_v"]

    x = [(_dot(at[p], st_b[p], _NT) + jnp.where(m0, av[2 * p], av[2 * p + 1])).astype(BF16) for p in pairs]
    u = [jnp.where(m0, _dot(t_inv[2 * p], x[p]), _dot(t_inv[2 * p + 1], x[p])) for p in pairs]
    u_b = [t.astype(BF16) for t in u]
    y = [_dot(rt[p], st_b[p], _NT)
         + jnp.where(m0, _dot(a_rb[2 * p], u_b[p]) + y_v[2 * p], _dot(a_rb[2 * p + 1], u_b[p]) + y_v[2 * p + 1])
         for p in pairs]

    for p in pairs:
        uv = jnp.concatenate([u_b[p], v_b[p]], axis=0)
        bk = jnp.concatenate([(b[p] * eend[p]).astype(BF16), (kmod[p] * eend[p]).astype(BF16)], axis=0)
        st_ref[p] = st[p] * jnp.exp(last[p]) + jnp.where(bdiag, _dot(uv, bk, _TN), 0.0)

    inv_n = 1.0 / C_HEAD
    dy = [y[p] - _head_sum(y[p], m0) * inv_n for p in pairs]
    var = [_head_sum(t * t, m0) * inv_n for t in dy]
    for p in pairs:
        yn = dy[p] * lax.rsqrt(var[p] + C_GN_EPS) * gg_ref[:, cols[p]] + gb_ref[:, cols[p]]
        o_ref[:, cols[p]] = ((yn + bonus[p]) * g_ref[:, cols[p]]).astype(o_ref.dtype)


def _rwkv_scan(rkv, lag, kk, ka, rk, gg, gb, layer, bsz, seq):
    t = rkv.shape[0]
    d = rkv.shape[1] // 3
    nchunk = seq // RW_CHUNK
    width = RW_PAIRS * LANES
    nblk = d // width

    def third(which):
        return pl.BlockSpec((RW_CHUNK, width), lambda b, p, c: (b * nchunk + c, which * nblk + p))

    blk = third(0)
    vec = pl.BlockSpec((None, 1, width), lambda b, p, c: (layer, 0, p))
    r, k, v, lw, a, g = rkv, rkv, rkv, lag, lag, lag
    return pl.pallas_call(
        _rwkv_scan_body,
        grid=(bsz, nblk, nchunk),
        in_specs=[third(0), third(1), third(2)] * 2 + [vec] * 5,
        out_specs=blk,
        out_shape=jax.ShapeDtypeStruct((t, d), BF16),
        scratch_shapes=[pltpu.VMEM((RW_PAIRS, LANES, LANES), F32)],
        compiler_params=_cparams(("parallel", "parallel", "arbitrary")),
        name="rwkv_scan",
    )(r, k, v, lw, a, g, kk, ka, rk, gg, gb)


def _pad_cols(w, n):
    return jnp.pad(w, ((0, 0), (0, 0), (0, n - w.shape[-1])))


def _pad_rows(w, n):
    return jnp.pad(w, ((0, 0), (0, n - w.shape[1]), (0, 0)))


def kernel(x, p, norms, final_norm, ffn_wg, ffn_wu, ffn_wd, ple_wp, ple_wg, e_w_in, e_w_out, a_vnorm, a_ws, a_bs, b_onorm, b_lb_logits, c_mix, c_wr, c_wk, c_wv, c_wo, c_w0, c_w1, c_w2, c_a0, c_a1, c_a2, c_g1, c_g2, c_kk, c_ka, c_rk, c_gn_g, c_gn_b, c_v0, c_v1, c_v2):
    bsz, seq, d = x.shape
    depth = p.shape[0]
    t = bsz * seq
    a_width = a_vnorm.shape[-1]
    b_width = b_onorm.shape[-1]

    bf = lambda w: w.astype(BF16)
    wg_b, wu_b, wd_b = ffn_wg, ffn_wu, ffn_wd
    ple_wp_b, ple_wg_b = ple_wp, ple_wg
    e_in_b = e_w_in
    e_out_b = bf(e_w_out)
    wo_b = bf(c_wo)
    rkv_b = bf(jnp.concatenate([c_wr, c_wk, c_wv], axis=-1))
    rank = c_g1.shape[-1]
    w1_b = bf(_pad_cols(c_w1, rank))
    a1_b = bf(_pad_cols(c_a1, rank))
    g1_b = bf(c_g1)
    lag_b = bf(jnp.concatenate([_pad_rows(c_w2, rank), _pad_rows(c_a2, rank), c_g2], axis=-1))
    bias_b = jnp.concatenate([c_w0, c_a0, jnp.zeros_like(c_w0)], axis=-1).reshape(c_w0.shape[0], 1, 3 * d)
    v1_b = bf(_pad_cols(c_v1, LORA_PAD))
    v2_b = bf(_pad_rows(c_v2, LORA_PAD))

    vec3 = lambda w: w.reshape(w.shape[0], 1, -1)
    a_vnorm3, b_onorm3 = vec3(a_vnorm), vec3(b_onorm)
    a_bs4 = a_bs.reshape(a_bs.shape + (1,))
    v0_3 = vec3(c_v0)
    kk3, ka3, rk3, gg3, gb3 = vec3(c_kk), vec3(c_ka), vec3(c_rk), vec3(c_gn_g), vec3(c_gn_b)
    fg = final_norm.reshape(1, d)

    h = x.reshape(t, d)
    p2 = p.reshape(depth, t, p.shape[-1])
    rkv_first = None
    for i in range(depth):
        j = i // 2
        h = _ffn(h, norms[i, 0].reshape(1, d), wg_b, wu_b, wd_b, i, 0)
        g1n = norms[i, 1].reshape(1, d)
        if i % 2 == 0:
            proj = _nmm(h, g1n, e_in_b, j)
            a_out = _gmlp(proj, a_vnorm3, a_ws, a_bs4, j, a_width)
            b_out = _hgrn2(proj, b_lb_logits, b_onorm3, i, j, bsz, seq, a_width, b_width)
            h = _mm2_res(h, a_out, b_out, e_out_b, j)
        else:
            vres = None if j == 0 else (v1_b, v2_b, v0_3, rkv_first, j - 1)
            rkv, lag = _rwkv_proj(h, g1n, c_mix, rkv_b, w1_b, a1_b, g1_b, lag_b, bias_b, j, seq, vres=vres)
            if j == 0:
                rkv_first = rkv
            y = _rwkv_scan(rkv, lag, kk3, ka3, rk3, gg3, gb3, j, bsz, seq)
            h = _mm_res(h, y, wo_b, j)
        h = _ffn(h, norms[i, 2].reshape(1, d), wg_b, wu_b, wd_b, i, 1)
        h = _ple(h, norms[i, 3].reshape(1, d), p2, ple_wg_b, ple_wp_b, fg, i, final=(i == depth - 1))
    return h.reshape(bsz, seq, d)
```

```python
import functools

import jax
import jax.numpy as jnp
from jax import lax
from jax.experimental import pallas as pl
from jax.experimental.pallas import tpu as pltpu

F32 = jnp.float32
BF16 = jnp.bfloat16

LANES = 128
SUBLANES = 8
RMS_EPS = 1e-6
A_CHUNK = 128
B_HEAD = 128
B_MIN_F = 1e-30
C_HEAD = 64
C_GN_EPS = 64e-5
LORA_PAD = 128

VMEM_LIMIT = 56 * 1024 * 1024

_NN = ((1,), (0,))
_NT = ((1,), (1,))
_TN = ((0,), (0,))


def _cparams(sem):
    return pltpu.CompilerParams(dimension_semantics=sem, vmem_limit_bytes=VMEM_LIMIT)


def _rms(x, g, eps=RMS_EPS):
    return x * lax.rsqrt(jnp.mean(x * x, axis=-1, keepdims=True) + eps) * g


def _sigmoid(x):
    return 1.0 / (1.0 + jnp.exp(-x))


def _silu(x):
    return x * _sigmoid(x)


def _gelu_tanh(x):
    return 0.5 * x * (1.0 + jnp.tanh(0.7978845608028654 * (x + 0.044715 * (x * x * x))))


def _dot(a, b, dims=_NN):
    return lax.dot_general(a, b, (dims, ((), ())), preferred_element_type=F32)


def _cumsum_rows(tri_bf16, x):
    hi = x.astype(BF16)
    r1 = x - hi.astype(F32)
    mid = r1.astype(BF16)
    lo = (r1 - mid.astype(F32)).astype(BF16)
    return _dot(tri_bf16, hi) + (_dot(tri_bf16, mid) + _dot(tri_bf16, lo))


def _tri_masks(n):
    row = lax.broadcasted_iota(jnp.int32, (n, n), 0)
    col = lax.broadcasted_iota(jnp.int32, (n, n), 1)
    return col <= row, col < row


FFN_DOWN_COLS = 512


def _ffn_body(x_ref, g_ref, wg_ref, wu_ref, wd_ref, o_ref, xn_ref):
    j = pl.program_id(1)

    @pl.when(j == 0)
    def _():
        x = x_ref[...]
        xn_ref[...] = _rms(x, g_ref[...]).astype(BF16)
        o_ref[...] = x

    xn = xn_ref[...]
    gate = _dot(xn, wg_ref[...].astype(BF16))
    up = _dot(xn, wu_ref[...].astype(BF16))
    hid = (0.5 * _silu(gate) * up).astype(BF16)
    for c in range(0, o_ref.shape[1], FFN_DOWN_COLS):
        cols = slice(c, c + FFN_DOWN_COLS)
        o_ref[:, cols] += _dot(hid, wd_ref[:, cols].astype(BF16))


def _ffn(h, g, wg, wu, wd, layer, half, tm=1024, tf=256):
    t, d = h.shape
    f = wg.shape[-1]
    tm = min(tm, t)
    return pl.pallas_call(
        _ffn_body,
        grid=(t // tm, f // tf),
        in_specs=[
            pl.BlockSpec((tm, d), lambda i, j: (i, 0)),
            pl.BlockSpec((1, d), lambda i, j: (0, 0)),
            pl.BlockSpec((None, None, d, tf), lambda i, j: (layer, half, 0, j)),
            pl.BlockSpec((None, None, d, tf), lambda i, j: (layer, half, 0, j)),
            pl.BlockSpec((None, None, tf, d), lambda i, j: (layer, half, j, 0)),
        ],
        out_specs=pl.BlockSpec((tm, d), lambda i, j: (i, 0)),
        out_shape=jax.ShapeDtypeStruct((t, d), F32),
        scratch_shapes=[pltpu.VMEM((tm, d), BF16)],
        compiler_params=_cparams(("parallel", "arbitrary")),
        name="ffn",
    )(h, g, wg, wu, wd)


def _mm_res_body(h_ref, x_ref, w_ref, o_ref):
    o_ref[...] = h_ref[...] + _dot(x_ref[...], w_ref[...])


def _mm_res(h, x, w, layer, tm=512, tn=2048):
    t, k = x.shape
    n = w.shape[-1]
    tm = min(tm, t)
    return pl.pallas_call(
        _mm_res_body,
        grid=(t // tm, n // tn),
        in_specs=[
            pl.BlockSpec((tm, tn), lambda i, j: (i, j)),
            pl.BlockSpec((tm, k), lambda i, j: (i, 0)),
            pl.BlockSpec((None, k, tn), lambda i, j: (layer, 0, j)),
        ],
        out_specs=pl.BlockSpec((tm, tn), lambda i, j: (i, j)),
        out_shape=jax.ShapeDtypeStruct((t, n), F32),
        compiler_params=_cparams(("parallel", "parallel")),
        name="mm_res",
    )(h, x, w)


def _mm2_res_body(h_ref, xa_ref, xb_ref, wa_ref, wb_ref, o_ref):
    o_ref[...] = h_ref[...] + (_dot(xa_ref[...], wa_ref[...]) + _dot(xb_ref[...], wb_ref[...]))


def _mm2_res(h, xa, xb, w, layer, tm=512, tn=2048):
    t, ka = xa.shape
    kb = xb.shape[1]
    n = w.shape[-1]
    assert ka == kb
    tm = min(tm, t)
    return pl.pallas_call(
        _mm2_res_body,
        grid=(t // tm, n // tn),
        in_specs=[
            pl.BlockSpec((tm, tn), lambda i, j: (i, j)),
            pl.BlockSpec((tm, ka), lambda i, j: (i, 0)),
            pl.BlockSpec((tm, kb), lambda i, j: (i, 0)),
            pl.BlockSpec((None, ka, tn), lambda i, j: (layer, 0, j)),
            pl.BlockSpec((None, kb, tn), lambda i, j: (layer, 1, j)),
        ],
        out_specs=pl.BlockSpec((tm, tn), lambda i, j: (i, j)),
        out_shape=jax.ShapeDtypeStruct((t, n), F32),
        compiler_params=_cparams(("parallel", "parallel")),
        name="mm2_res",
    )(h, xa, xb, w, w)


def _nmm_body(x_ref, g_ref, w_ref, o_ref, xn_ref):
    @pl.when(pl.program_id(1) == 0)
    def _():
        xn_ref[...] = _rms(x_ref[...], g_ref[...]).astype(BF16)

    o_ref[...] = _dot(xn_ref[...], w_ref[...].astype(BF16))


def _nmm(h, g, w, layer, tm=1024, tn=1024):
    t, d = h.shape
    n = w.shape[-1]
    tm = min(tm, t)
    return pl.pallas_call(
        _nmm_body,
        grid=(t // tm, n // tn),
        in_specs=[
            pl.BlockSpec((tm, d), lambda i, j: (i, 0)),
            pl.BlockSpec((1, d), lambda i, j: (0, 0)),
            pl.BlockSpec((None, d, tn), lambda i, j: (layer, 0, j)),
        ],
        out_specs=pl.BlockSpec((tm, tn), lambda i, j: (i, j)),
        out_shape=jax.ShapeDtypeStruct((t, n), F32),
        scratch_shapes=[pltpu.VMEM((tm, d), BF16)],
        compiler_params=_cparams(("parallel", "arbitrary")),
        name="norm_mm",
    )(h, g, w)


def _ple_body(h_ref, g_ref, p_ref, wg_ref, wp_ref, fg_ref, o_ref, *, final):
    h = h_ref[...]
    xn = _rms(h, g_ref[...]).astype(BF16)
    gate = _sigmoid(_dot(xn, wg_ref[...].astype(BF16)))
    pe = _dot(p_ref[...].astype(BF16), wp_ref[...].astype(BF16))
    out = h + gate * pe
    if final:
        out = _rms(out, fg_ref[...])
    o_ref[...] = out


def _ple(h, g, p, wg, wp, fg, layer, final, tm=512):
    t, d = h.shape
    pd = p.shape[-1]
    return pl.pallas_call(
        functools.partial(_ple_body, final=final),
        grid=(t // tm,),
        in_specs=[
            pl.BlockSpec((tm, d), lambda i: (i, 0)),
            pl.BlockSpec((1, d), lambda i: (0, 0)),
            pl.BlockSpec((None, tm, pd), lambda i: (layer, i, 0)),
            pl.BlockSpec((None, d, d), lambda i: (layer, 0, 0)),
            pl.BlockSpec((None, pd, d), lambda i: (layer, 0, 0)),
            pl.BlockSpec((1, d), lambda i: (0, 0)),
        ],
        out_specs=pl.BlockSpec((tm, d), lambda i: (i, 0)),
        out_shape=jax.ShapeDtypeStruct((t, d), F32),
        compiler_params=_cparams(("parallel",)),
        name="ple",
    )(h, g, p, wg, wp, fg)


GMLP_CHUNKS = 4


def _gmlp_body(u_ref, v_ref, gain_ref, ws_ref, bs_ref, o_ref):
    groups = ws_ref.shape[0]
    tril, _ = _tri_masks(A_CHUNK)
    for g in range(groups):
        w = jnp.where(tril, ws_ref[g], 0.0).astype(BF16)
        bias = bs_ref[g]
        cols = slice(g * A_CHUNK, (g + 1) * A_CHUNK)
        for c in range(GMLP_CHUNKS):
            rows = slice(c * A_CHUNK, (c + 1) * A_CHUNK)
            u = _gelu_tanh(u_ref[rows, cols])
            v = _gelu_tanh(v_ref[rows, cols])
            vg = _rms(v, gain_ref[:, cols])
            s = _dot(w, vg.astype(BF16)) + bias
            o_ref[rows, cols] = (u * s).astype(o_ref.dtype)


def _gmlp(proj, gain, ws, bs, layer, a_width):
    t = proj.shape[0]
    groups = a_width // A_CHUNK
    tm = GMLP_CHUNKS * A_CHUNK
    return pl.pallas_call(
        _gmlp_body,
        grid=(t // tm,),
        in_specs=[
            pl.BlockSpec((tm, a_width), lambda c: (c, 0)),
            pl.BlockSpec((tm, a_width), lambda c: (c, 1)),
            pl.BlockSpec((None, 1, a_width), lambda c: (layer, 0, 0)),
            pl.BlockSpec((None, groups, A_CHUNK, A_CHUNK), lambda c: (layer, 0, 0, 0)),
            pl.BlockSpec((None, groups, A_CHUNK, 1), lambda c: (layer, 0, 0, 0)),
        ],
        out_specs=pl.BlockSpec((tm, a_width), lambda c: (c, 0)),
        out_shape=jax.ShapeDtypeStruct((t, a_width), BF16),
        compiler_params=_cparams(("parallel",)),
        name="gmlp",
    )(proj, proj, gain, ws, bs)


HG_CHUNK = 64
HG_SUB = 16


HG_HEADS = 8


def _hgrn2_body(q_ref, f_ref, i_ref, g_ref, lbl_ref, on_ref, o_ref, st_ref, *, layer):
    c = pl.program_id(2)

    @pl.when(c == 0)
    def _():
        st_ref[...] = jnp.zeros_like(st_ref)

    logits = lbl_ref[...]
    e = jnp.exp(logits - jnp.max(logits, axis=0, keepdims=True))
    probs = e / jnp.sum(e, axis=0, keepdims=True)
    lb = jnp.zeros((1, logits.shape[1]), F32)
    for r in range(1, layer + 1):
        lb = lb + probs[r:r + 1, :]

    n = HG_CHUNK
    hs = range(HG_HEADS)
    cols = [slice(hd * B_HEAD, (hd + 1) * B_HEAD) for hd in hs]
    tril, _ = _tri_masks(n)
    tri = jnp.where(tril, 1.0, 0.0).astype(BF16)

    st = [st_ref[hd] for hd in hs]
    v = [i_ref[:, c] for c in cols]
    f = [lb[:, c] + (1.0 - lb[:, c]) * _sigmoid(f_ref[:, c]) for c in cols]
    kf = [1.0 - x for x in f]
    qf = [_silu(q_ref[:, c]) for c in cols]
    cum = [_cumsum_rows(tri, jnp.log(jnp.maximum(x, B_MIN_F))) for x in f]
    last = [x[n - 1:n, :] for x in cum]

    v_b = [x.astype(BF16) for x in v]
    o = [_dot((qf[h] * jnp.exp(cum[h])).astype(BF16), st[h].astype(BF16), _NT) for h in hs]
    for h in hs:
        kend = (kf[h] * jnp.exp(last[h] - cum[h])).astype(BF16)
        st_ref[h] = st[h] * jnp.exp(last[h]) + _dot(v_b[h], kend, _TN)

    nsub = n // HG_SUB
    trow = lax.broadcasted_iota(jnp.int32, (HG_SUB, 1), 0)
    rows = [[] for _ in hs]
    for bi in range(nsub):
        lo, hi = bi * HG_SUB, (bi + 1) * HG_SUB
        acc = [jnp.zeros((HG_SUB, B_HEAD), F32) for _ in hs]
        if bi > 0:
            ref = [cum[h][lo - 1:lo, :] for h in hs]
            qh = [(qf[h][lo:hi] * jnp.exp(cum[h][lo:hi] - ref[h])).astype(BF16) for h in hs]
            kh = [(kf[h][:lo] * jnp.exp(ref[h] - cum[h][:lo])).astype(BF16) for h in hs]
            att = [_dot(qh[h], kh[h], _NT).astype(BF16) for h in hs]
            acc = [_dot(att[h], v_b[h][:lo]) for h in hs]
        for s in range(HG_SUB):
            for h in hs:
                c_b = cum[h][lo:hi]
                dec = jnp.exp(jnp.minimum(c_b - c_b[s:s + 1, :], 0.0))
                col = jnp.sum(qf[h][lo:hi] * dec * kf[h][lo + s:lo + s + 1, :], axis=-1, keepdims=True)
                col = jnp.where(trow >= s, col, 0.0)
                acc[h] = acc[h] + col * v[h][lo + s:lo + s + 1, :]
        for h in hs:
            rows[h].append(acc[h])

    for h in hs:
        out = o[h] + jnp.concatenate(rows[h], axis=0)
        o_ref[:, cols[h]] = (_rms(out, on_ref[:, cols[h]]) * _silu(g_ref[:, cols[h]])).astype(o_ref.dtype)


def _hgrn2(proj, lb_logits, onorm, layer, elayer, bsz, seq, a_width, b_width):
    t = proj.shape[0]
    heads = b_width // B_HEAD
    hgroups = heads // HG_HEADS
    nchunk = seq // HG_CHUNK
    width = HG_HEADS * B_HEAD
    off = 2 * a_width // width

    def col(which):
        return lambda b, h, c: (b * nchunk + c, off + which * hgroups + h)

    blk = (HG_CHUNK, width)
    depth = lb_logits.shape[0]
    return pl.pallas_call(
        functools.partial(_hgrn2_body, layer=layer),
        grid=(bsz, hgroups, nchunk),
        in_specs=[
            pl.BlockSpec(blk, col(0)),
            pl.BlockSpec(blk, col(1)),
            pl.BlockSpec(blk, col(2)),
            pl.BlockSpec(blk, col(3)),
            pl.BlockSpec((depth, width), lambda b, h, c: (0, h)),
            pl.BlockSpec((None, 1, width), lambda b, h, c: (elayer, 0, h)),
        ],
        out_specs=pl.BlockSpec(blk, lambda b, h, c: (b * nchunk + c, h)),
        out_shape=jax.ShapeDtypeStruct((t, b_width), BF16),
        scratch_shapes=[pltpu.VMEM((HG_HEADS, B_HEAD, B_HEAD), F32)],
        compiler_params=_cparams(("parallel", "parallel", "arbitrary")),
        name="hgrn2",
    )(proj, proj, proj, proj, lb_logits, onorm)


MIX_R, MIX_W, MIX_K, MIX_V, MIX_A, MIX_G = range(6)
RW_DECAY_SCALE = 0.6065306597126334
PROJ_SUB = 128


def _rwkv_proj_body(*refs, tm, seq, ntile, vres):
    if vres:
        (x_ref, xp_ref, g_ref, mix_ref, w_ref, w1_ref, a1_ref, g1_ref, l_ref, b_ref,
         v1_ref, v2_ref, v0_ref, vf_ref, o_ref, lag_ref, xs_ref, hs_ref, hv_ref) = refs
    else:
        (x_ref, xp_ref, g_ref, mix_ref, w_ref, w1_ref, a1_ref, g1_ref, l_ref, b_ref,
         o_ref, lag_ref, xs_ref, hs_ref) = refs
    i = pl.program_id(0)
    j = pl.program_id(1)

    @pl.when(j == 0)
    def _():
        g = g_ref[...]
        for r0 in range(0, tm, PROJ_SUB):
            rows = slice(r0, r0 + PROJ_SUB)
            hn = _rms(x_ref[rows, :], g)
            if r0 == 0:
                hp = _rms(xp_ref[...], g)[SUBLANES - 1:SUBLANES, :]
                hp = jnp.where((i * tm) % seq == 0, 0.0, hp)
            else:
                hp = _rms(x_ref[r0 - SUBLANES:r0, :], g)[SUBLANES - 1:SUBLANES, :]
            row = lax.broadcasted_iota(jnp.int32, hn.shape, 0)
            xx = jnp.where(row == 0, hp, pltpu.roll(hn, 1, 0)) - hn

            def mixed(m):
                return (hn + xx * mix_ref[m:m + 1, :]).astype(BF16)

            xs_ref[0, rows, :] = mixed(MIX_R)
            xs_ref[1, rows, :] = mixed(MIX_K)
            xv = mixed(MIX_V)
            xs_ref[2, rows, :] = xv
            hs_ref[0, rows, :] = jnp.tanh(_dot(mixed(MIX_W), w1_ref[...])).astype(BF16)
            hs_ref[1, rows, :] = _dot(mixed(MIX_A), a1_ref[...]).astype(BF16)
            hs_ref[2, rows, :] = _sigmoid(_dot(mixed(MIX_G), g1_ref[...])).astype(BF16)
            if vres:
                hv_ref[rows, :] = _dot(xv, v1_ref[...]).astype(BF16)

    third = j // ntile
    y = _dot(xs_ref[third], w_ref[...])
    z = b_ref[...] + _dot(hs_ref[third], l_ref[...])

    s = _sigmoid(z)
    lag_ref[...] = jnp.where(third == 2, z, s * jnp.where(third == 0, -RW_DECAY_SCALE, 1.0))
    if vres:
        @pl.when(third == 2)
        def _():
            mv = _sigmoid(v0_ref[...] + _dot(hv_ref[...], v2_ref[...]))
            o_ref[...] = y + (vf_ref[...] - y) * mv

        @pl.when(third != 2)
        def _():
            o_ref[...] = y
    else:
        o_ref[...] = y


def _rwkv_proj(h, g, mix, wcat, w1, a1, g1, lcat, bcat, layer, seq, vres=None, tm=1024, tn=512):
    t, d = h.shape
    tm = min(tm, t)
    ntile = d // tn
    rank = w1.shape[-1]

    def lora(lyr):
        return pl.BlockSpec((None, d, rank), lambda i, j: (lyr, 0, 0))

    def vtile(i, j):
        return jnp.maximum(j - 2 * ntile, 0)

    args = [h, h, g, mix, wcat, w1, a1, g1, lcat, bcat]
    in_specs = [
        pl.BlockSpec((tm, d), lambda i, j: (i, 0)),
        pl.BlockSpec((SUBLANES, d), lambda i, j: (jnp.maximum(i * (tm // SUBLANES) - 1, 0), 0)),
        pl.BlockSpec((1, d), lambda i, j: (0, 0)),
        pl.BlockSpec((None, mix.shape[1], d), lambda i, j: (layer, 0, 0)),
        pl.BlockSpec((None, d, tn), lambda i, j: (layer, 0, j)),
        lora(layer), lora(layer), lora(layer),
        pl.BlockSpec((None, rank, tn), lambda i, j: (layer, 0, j)),
        pl.BlockSpec((None, 1, tn), lambda i, j: (layer, 0, j)),
    ]
    scratch = [pltpu.VMEM((3, tm, d), BF16), pltpu.VMEM((3, tm, rank), BF16)]
    if vres is not None:
        v1, v2, v0, rkv_first, vl = vres
        vrank = v1.shape[-1]
        args += [v1, v2, v0, rkv_first]
        in_specs += [
            pl.BlockSpec((None, d, vrank), lambda i, j: (vl, 0, 0)),
            pl.BlockSpec((None, vrank, tn), lambda i, j: (vl, 0, vtile(i, j))),
            pl.BlockSpec((None, 1, tn), lambda i, j: (vl, 0, vtile(i, j))),
            pl.BlockSpec((tm, tn), lambda i, j: (i, 2 * ntile + vtile(i, j))),
        ]
        scratch.append(pltpu.VMEM((tm, vrank), BF16))
    tile = pl.BlockSpec((tm, tn), lambda i, j: (i, j))
    return pl.pallas_call(
        functools.partial(_rwkv_proj_body, tm=tm, seq=seq, ntile=ntile, vres=vres is not None),
        grid=(t // tm, 3 * ntile),
        in_specs=in_specs,
        out_specs=[tile, tile],
        out_shape=[jax.ShapeDtypeStruct((t, 3 * d), F32), jax.ShapeDtypeStruct((t, 3 * d), F32)],
        scratch_shapes=scratch,
        compiler_params=_cparams(("parallel", "arbitrary")),
        name="rwkv_proj",
    )(*args)


RW_CHUNK = 64


RW_PAIRS = 16


def _head_sum(x, m0):
    s0 = jnp.sum(jnp.where(m0, x, 0.0), axis=-1, keepdims=True)
    s1 = jnp.sum(jnp.where(m0, 0.0, x), axis=-1, keepdims=True)
    return jnp.where(m0, s0, s1)


def _rwkv_scan_body(r_ref, k_ref, v_ref, lw_ref, a_ref, g_ref, kk_ref, ka_ref, rk_ref, gg_ref, gb_ref,
                    o_ref, st_ref):
    c = pl.program_id(2)

    @pl.when(c == 0)
    def _():
        st_ref[...] = jnp.zeros_like(st_ref)

    n = RW_CHUNK
    pairs = range(RW_PAIRS)
    cols = [slice(p * LANES, (p + 1) * LANES) for p in pairs]
    lane = lax.broadcasted_iota(jnp.int32, (1, LANES), 1)
    m0 = lane < C_HEAD
    m1 = jnp.logical_not(m0)
    tril, _ = _tri_masks(n)
    tri = jnp.where(tril, 1.0, 0.0).astype(BF16)
    row = lax.broadcasted_iota(jnp.int32, (LANES, LANES), 0)
    colm = lax.broadcasted_iota(jnp.int32, (LANES, LANES), 1)
    bdiag = (row < C_HEAD) == (colm < C_HEAD)

    def hsel(x, hd):
        return jnp.where(m0 if hd == 0 else m1, x, jnp.zeros_like(x))

    st = [st_ref[p] for p in pairs]
    r = [r_ref[:, c] for c in cols]
    k = [k_ref[:, c] for c in cols]
    v = [v_ref[:, c] for c in cols]
    asig = [a_ref[:, c] for c in cols]

    cum = [_cumsum_rows(tri, lw_ref[:, c]) for c in cols]
    last = [x[n - 1:n, :] for x in cum]
    kkr = [k[p] * kk_ref[:, cols[p]] for p in pairs]
    kk = [x / jnp.maximum(jnp.sqrt(_head_sum(x * x, m0)), 1e-12) for x in kkr]
    kmod = [k[p] * (1.0 + (asig[p] - 1.0) * ka_ref[:, cols[p]]) for p in pairs]
    b = [kk[p] * asig[p] for p in pairs]
    bonus = [_head_sum(r[p] * kmod[p] * rk_ref[:, cols[p]], m0) * v[p] for p in pairs]
    rt = [(r[p] * jnp.exp(cum[p])).astype(BF16) for p in pairs]
    at = [(-kk[p] * jnp.exp(cum[p] - lw_ref[:, cols[p]])).astype(BF16) for p in pairs]
    einv = [jnp.exp(-x) for x in cum]
    bt = [(b[p] * einv[p]).astype(BF16) for p in pairs]
    kt = [(kmod[p] * einv[p]).astype(BF16) for p in pairs]
    eend = [jnp.exp(last[p] - cum[p]) for p in pairs]
    v_b = [x.astype(BF16) for x in v]
    st_b = [x.astype(BF16) for x in st]

    heads = [(p, hd) for p in pairs for hd in range(2)]
    _, stril = _tri_masks(n)
    eye = jnp.where(tril & jnp.logical_not(stril), 1.0, 0.0)
    at_h = [hsel(at[p], hd) for p, hd in heads]
    rt_h = [hsel(rt[p], hd) for p, hd in heads]
    pw = [jnp.where(stril, _dot(at_h[i], bt[p], _NT), 0.0) for i, (p, hd) in enumerate(heads)]
    t_inv = [eye + x for x in pw]
    fill = {}
    fillers = [
        lambda: fill.update(a_ak=[jnp.where(stril, _dot(at_h[i], kt[p], _NT), 0.0).astype(BF16)
                                  for i, (p, hd) in enumerate(heads)]),
        lambda: fill.update(a_rb=[jnp.where(tril, _dot(rt_h[i], bt[p], _NT), 0.0).astype(BF16)
                                  for i, (p, hd) in enumerate(heads)]),
        lambda: fill.update(a_rk=[jnp.where(tril, _dot(rt_h[i], kt[p], _NT), 0.0).astype(BF16)
                                  for i, (p, hd) in enumerate(heads)]),
        lambda: fill.update(av=[_dot(fill["a_ak"][i], v_b[p]) for i, (p, hd) in enumerate(heads)]),
        lambda: fill.update(y_v=[_dot(fill["a_rk"][i], v_b[p]) for i, (p, hd) in enumerate(heads)]),
    ]
    step = 2
    while step < n:
        pw_b = [x.astype(BF16) for x in pw]
        pw = [_dot(x, x) for x in pw_b]
        if fillers:
            fillers.pop(0)()
        t_inv = [t + _dot(t.astype(BF16), x.astype(BF16)) for t, x in zip(t_inv, pw)]
        step *= 2
    for filler in fillers:
        filler()
    t_inv = [x.astype(BF16) for x in t_inv]
    a_rb, av, y_v = fill["a_rb"], fill["av"], fill["y_v"]

    x = [(_dot(at[p], st_b[p], _NT) + jnp.where(m0, av[2 * p], av[2 * p + 1])).astype(BF16) for p in pairs]
    u = [jnp.where(m0, _dot(t_inv[2 * p], x[p]), _dot(t_inv[2 * p + 1], x[p])) for p in pairs]
    u_b = [t.astype(BF16) for t in u]
    y = [_dot(rt[p], st_b[p], _NT)
         + jnp.where(m0, _dot(a_rb[2 * p], u_b[p]) + y_v[2 * p], _dot(a_rb[2 * p + 1], u_b[p]) + y_v[2 * p + 1])
         for p in pairs]

    for p in pairs:
        uv = jnp.concatenate([u_b[p], v_b[p]], axis=0)
        bk = jnp.concatenate([(b[p] * eend[p]).astype(BF16), (kmod[p] * eend[p]).astype(BF16)], axis=0)
        st_ref[p] = st[p] * jnp.exp(last[p]) + jnp.where(bdiag, _dot(uv, bk, _TN), 0.0)

    inv_n = 1.0 / C_HEAD
    dy = [y[p] - _head_sum(y[p], m0) * inv_n for p in pairs]
    var = [_head_sum(t * t, m0) * inv_n for t in dy]
    for p in pairs:
        yn = dy[p] * lax.rsqrt(var[p] + C_GN_EPS) * gg_ref[:, cols[p]] + gb_ref[:, cols[p]]
        o_ref[:, cols[p]] = ((yn + bonus[p]) * g_ref[:, cols[p]]).astype(o_ref.dtype)


def _rwkv_scan(rkv, lag, kk, ka, rk, gg, gb, layer, bsz, seq):
    t = rkv.shape[0]
    d = rkv.shape[1] // 3
    nchunk = seq // RW_CHUNK
    width = RW_PAIRS * LANES
    nblk = d // width

    def third(which):
        return pl.BlockSpec((RW_CHUNK, width), lambda b, p, c: (b * nchunk + c, which * nblk + p))

    blk = third(0)
    vec = pl.BlockSpec((None, 1, width), lambda b, p, c: (layer, 0, p))
    r, k, v, lw, a, g = rkv, rkv, rkv, lag, lag, lag
    return pl.pallas_call(
        _rwkv_scan_body,
        grid=(bsz, nblk, nchunk),
        in_specs=[third(0), third(1), third(2)] * 2 + [vec] * 5,
        out_specs=blk,
        out_shape=jax.ShapeDtypeStruct((t, d), BF16),
        scratch_shapes=[pltpu.VMEM((RW_PAIRS, LANES, LANES), F32)],
        compiler_params=_cparams(("parallel", "parallel", "arbitrary")),
        name="rwkv_scan",
    )(r, k, v, lw, a, g, kk, ka, rk, gg, gb)


def _pad_cols(w, n):
    return jnp.pad(w, ((0, 0), (0, 0), (0, n - w.shape[-1])))


def _pad_rows(w, n):
    return jnp.pad(w, ((0, 0), (0, n - w.shape[1]), (0, 0)))


def kernel(x, p, norms, final_norm, ffn_wg, ffn_wu, ffn_wd, ple_wp, ple_wg, e_w_in, e_w_out, a_vnorm, a_ws, a_bs, b_onorm, b_lb_logits, c_mix, c_wr, c_wk, c_wv, c_wo, c_w0, c_w1, c_w2, c_a0, c_a1, c_a2, c_g1, c_g2, c_kk, c_ka, c_rk, c_gn_g, c_gn_b, c_v0, c_v1, c_v2):
    bsz, seq, d = x.shape
    depth = p.shape[0]
    t = bsz * seq
    a_width = a_vnorm.shape[-1]
    b_width = b_onorm.shape[-1]

    bf = lambda w: w.astype(BF16)
    wg_b, wu_b, wd_b = ffn_wg, ffn_wu, ffn_wd
    ple_wp_b, ple_wg_b = ple_wp, ple_wg
    e_in_b = e_w_in
    e_out_b = bf(e_w_out)
    wo_b = bf(c_wo)
    rkv_b = bf(jnp.concatenate([c_wr, c_wk, c_wv], axis=-1))
    rank = c_g1.shape[-1]
    w1_b = bf(_pad_cols(c_w1, rank))
    a1_b = bf(_pad_cols(c_a1, rank))
    g1_b = bf(c_g1)
    lag_b = bf(jnp.concatenate([_pad_rows(c_w2, rank), _pad_rows(c_a2, rank), c_g2], axis=-1))
    bias_b = jnp.concatenate([c_w0, c_a0, jnp.zeros_like(c_w0)], axis=-1).reshape(c_w0.shape[0], 1, 3 * d)
    v1_b = bf(_pad_cols(c_v1, LORA_PAD))
    v2_b = bf(_pad_rows(c_v2, LORA_PAD))

    vec3 = lambda w: w.reshape(w.shape[0], 1, -1)
    a_vnorm3, b_onorm3 = vec3(a_vnorm), vec3(b_onorm)
    a_bs4 = a_bs.reshape(a_bs.shape + (1,))
    v0_3 = vec3(c_v0)
    kk3, ka3, rk3, gg3, gb3 = vec3(c_kk), vec3(c_ka), vec3(c_rk), vec3(c_gn_g), vec3(c_gn_b)
    fg = final_norm.reshape(1, d)

    h = x.reshape(t, d)
    p2 = p.reshape(depth, t, p.shape[-1])
    rkv_first = None
    for i in range(depth):
        j = i // 2
        h = _ffn(h, norms[i, 0].reshape(1, d), wg_b, wu_b, wd_b, i, 0)
        g1n = norms[i, 1].reshape(1, d)
        if i % 2 == 0:
            proj = _nmm(h, g1n, e_in_b, j)
            a_out = _gmlp(proj, a_vnorm3, a_ws, a_bs4, j, a_width)
            b_out = _hgrn2(proj, b_lb_logits, b_onorm3, i, j, bsz, seq, a_width, b_width)
            h = _mm2_res(h, a_out, b_out, e_out_b, j)
        else:
            vres = None if j == 0 else (v1_b, v2_b, v0_3, rkv_first, j - 1)
            rkv, lag = _rwkv_proj(h, g1n, c_mix, rkv_b, w1_b, a1_b, g1_b, lag_b, bias_b, j, seq, vres=vres)
            if j == 0:
                rkv_first = rkv
            y = _rwkv_scan(rkv, lag, kk3, ka3, rk3, gg3, gb3, j, bsz, seq)
            h = _mm_res(h, y, wo_b, j)
        h = _ffn(h, norms[i, 2].reshape(1, d), wg_b, wu_b, wd_b, i, 1)
        h = _ple(h, norms[i, 3].reshape(1, d), p2, ple_wg_b, ple_wp_b, fg, i, final=(i == depth - 1))
    return h.reshape(bsz, seq, d)
```

```python
import functools

import jax
import jax.numpy as jnp
from jax import lax
from jax.experimental import pallas as pl
from jax.experimental.pallas import tpu as pltpu

F32 = jnp.float32
BF16 = jnp.bfloat16

LANES = 128
SUBLANES = 8
RMS_EPS = 1e-6
A_CHUNK = 128
B_HEAD = 128
B_MIN_F = 1e-30
C_HEAD = 64
C_GN_EPS = 64e-5
LORA_PAD = 128

VMEM_LIMIT = 56 * 1024 * 1024

_NN = ((1,), (0,))
_NT = ((1,), (1,))
_TN = ((0,), (0,))


def _cparams(sem):
    return pltpu.CompilerParams(dimension_semantics=sem, vmem_limit_bytes=VMEM_LIMIT)


def _rms(x, g, eps=RMS_EPS):
    return x * lax.rsqrt(jnp.mean(x * x, axis=-1, keepdims=True) + eps) * g


def _sigmoid(x):
    return 1.0 / (1.0 + jnp.exp(-x))


def _silu(x):
    return x * _sigmoid(x)


def _gelu_tanh(x):
    return 0.5 * x * (1.0 + jnp.tanh(0.7978845608028654 * (x + 0.044715 * (x * x * x))))


def _dot(a, b, dims=_NN):
    return lax.dot_general(a, b, (dims, ((), ())), preferred_element_type=F32)


def _cumsum_rows(tri_bf16, x):
    hi = x.astype(BF16)
    r1 = x - hi.astype(F32)
    mid = r1.astype(BF16)
    lo = (r1 - mid.astype(F32)).astype(BF16)
    return _dot(tri_bf16, hi) + (_dot(tri_bf16, mid) + _dot(tri_bf16, lo))


def _tri_masks(n):
    row = lax.broadcasted_iota(jnp.int32, (n, n), 0)
    col = lax.broadcasted_iota(jnp.int32, (n, n), 1)
    return col <= row, col < row


FFN_DOWN_COLS = 512


def _ffn_body(x_ref, g_ref, wg_ref, wu_ref, wd_ref, o_ref, xn_ref):
    j = pl.program_id(1)

    @pl.when(j == 0)
    def _():
        x = x_ref[...]
        xn_ref[...] = _rms(x, g_ref[...]).astype(BF16)
        o_ref[...] = x

    xn = xn_ref[...]
    gate = _dot(xn, wg_ref[...].astype(BF16))
    up = _dot(xn, wu_ref[...].astype(BF16))
    hid = (0.5 * _silu(gate) * up).astype(BF16)
    for c in range(0, o_ref.shape[1], FFN_DOWN_COLS):
        cols = slice(c, c + FFN_DOWN_COLS)
        o_ref[:, cols] += _dot(hid, wd_ref[:, cols].astype(BF16))


def _ffn(h, g, wg, wu, wd, layer, half, tm=1024, tf=256):
    t, d = h.shape
    f = wg.shape[-1]
    tm = min(tm, t)
    return pl.pallas_call(
        _ffn_body,
        grid=(t // tm, f // tf),
        in_specs=[
            pl.BlockSpec((tm, d), lambda i, j: (i, 0)),
            pl.BlockSpec((1, d), lambda i, j: (0, 0)),
            pl.BlockSpec((None, None, d, tf), lambda i, j: (layer, half, 0, j)),
            pl.BlockSpec((None, None, d, tf), lambda i, j: (layer, half, 0, j)),
            pl.BlockSpec((None, None, tf, d), lambda i, j: (layer, half, j, 0)),
        ],
        out_specs=pl.BlockSpec((tm, d), lambda i, j: (i, 0)),
        out_shape=jax.ShapeDtypeStruct((t, d), F32),
        scratch_shapes=[pltpu.VMEM((tm, d), BF16)],
        compiler_params=_cparams(("parallel", "arbitrary")),
        name="ffn",
    )(h, g, wg, wu, wd)


def _mm_res_body(h_ref, x_ref, w_ref, o_ref):
    o_ref[...] = h_ref[...] + _dot(x_ref[...], w_ref[...])


def _mm_res(h, x, w, layer, tm=512, tn=2048):
    t, k = x.shape
    n = w.shape[-1]
    tm = min(tm, t)
    return pl.pallas_call(
        _mm_res_body,
        grid=(t // tm, n // tn),
        in_specs=[
            pl.BlockSpec((tm, tn), lambda i, j: (i, j)),
            pl.BlockSpec((tm, k), lambda i, j: (i, 0)),
            pl.BlockSpec((None, k, tn), lambda i, j: (layer, 0, j)),
        ],
        out_specs=pl.BlockSpec((tm, tn), lambda i, j: (i, j)),
        out_shape=jax.ShapeDtypeStruct((t, n), F32),
        compiler_params=_cparams(("parallel", "parallel")),
        name="mm_res",
    )(h, x, w)


def _mm2_res_body(h_ref, xa_ref, xb_ref, wa_ref, wb_ref, o_ref):
    o_ref[...] = h_ref[...] + (_dot(xa_ref[...], wa_ref[...]) + _dot(xb_ref[...], wb_ref[...]))


def _mm2_res(h, xa, xb, w, layer, tm=512, tn=2048):
    t, ka = xa.shape
    kb = xb.shape[1]
    n = w.shape[-1]
    assert ka == kb
    tm = min(tm, t)
    return pl.pallas_call(
        _mm2_res_body,
        grid=(t // tm, n // tn),
        in_specs=[
            pl.BlockSpec((tm, tn), lambda i, j: (i, j)),
            pl.BlockSpec((tm, ka), lambda i, j: (i, 0)),
            pl.BlockSpec((tm, kb), lambda i, j: (i, 0)),
            pl.BlockSpec((None, ka, tn), lambda i, j: (layer, 0, j)),
            pl.BlockSpec((None, kb, tn), lambda i, j: (layer, 1, j)),
        ],
        out_specs=pl.BlockSpec((tm, tn), lambda i, j: (i, j)),
        out_shape=jax.ShapeDtypeStruct((t, n), F32),
        compiler_params=_cparams(("parallel", "parallel")),
        name="mm2_res",
    )(h, xa, xb, w, w)


def _nmm_body(x_ref, g_ref, w_ref, o_ref, xn_ref):
    @pl.when(pl.program_id(1) == 0)
    def _():
        xn_ref[...] = _rms(x_ref[...], g_ref[...]).astype(BF16)

    o_ref[...] = _dot(xn_ref[...], w_ref[...].astype(BF16))


def _nmm(h, g, w, layer, tm=1024, tn=1024):
    t, d = h.shape
    n = w.shape[-1]
    tm = min(tm, t)
    return pl.pallas_call(
        _nmm_body,
        grid=(t // tm, n // tn),
        in_specs=[
            pl.BlockSpec((tm, d), lambda i, j: (i, 0)),
            pl.BlockSpec((1, d), lambda i, j: (0, 0)),
            pl.BlockSpec((None, d, tn), lambda i, j: (layer, 0, j)),
        ],
        out_specs=pl.BlockSpec((tm, tn), lambda i, j: (i, j)),
        out_shape=jax.ShapeDtypeStruct((t, n), F32),
        scratch_shapes=[pltpu.VMEM((tm, d), BF16)],
        compiler_params=_cparams(("parallel", "arbitrary")),
        name="norm_mm",
    )(h, g, w)


def _ple_body(h_ref, g_ref, p_ref, wg_ref, wp_ref, fg_ref, o_ref, *, final):
    h = h_ref[...]
    xn = _rms(h, g_ref[...]).astype(BF16)
    gate = _sigmoid(_dot(xn, wg_ref[...].astype(BF16)))
    pe = _dot(p_ref[...].astype(BF16), wp_ref[...].astype(BF16))
    out = h + gate * pe
    if final:
        out = _rms(out, fg_ref[...])
    o_ref[...] = out


def _ple(h, g, p, wg, wp, fg, layer, final, tm=512):
    t, d = h.shape
    pd = p.shape[-1]
    return pl.pallas_call(
        functools.partial(_ple_body, final=final),
        grid=(t // tm,),
        in_specs=[
            pl.BlockSpec((tm, d), lambda i: (i, 0)),
            pl.BlockSpec((1, d), lambda i: (0, 0)),
            pl.BlockSpec((None, tm, pd), lambda i: (layer, i, 0)),
            pl.BlockSpec((None, d, d), lambda i: (layer, 0, 0)),
            pl.BlockSpec((None, pd, d), lambda i: (layer, 0, 0)),
            pl.BlockSpec((1, d), lambda i: (0, 0)),
        ],
        out_specs=pl.BlockSpec((tm, d), lambda i: (i, 0)),
        out_shape=jax.ShapeDtypeStruct((t, d), F32),
        compiler_params=_cparams(("parallel",)),
        name="ple",
    )(h, g, p, wg, wp, fg)


GMLP_CHUNKS = 4


def _gmlp_body(u_ref, v_ref, gain_ref, ws_ref, bs_ref, o_ref):
    groups = ws_ref.shape[0]
    tril, _ = _tri_masks(A_CHUNK)
    for g in range(groups):
        w = jnp.where(tril, ws_ref[g], 0.0).astype(BF16)
        bias = bs_ref[g]
        cols = slice(g * A_CHUNK, (g + 1) * A_CHUNK)
        for c in range(GMLP_CHUNKS):
            rows = slice(c * A_CHUNK, (c + 1) * A_CHUNK)
            u = _gelu_tanh(u_ref[rows, cols])
            v = _gelu_tanh(v_ref[rows, cols])
            vg = _rms(v, gain_ref[:, cols])
            s = _dot(w, vg.astype(BF16)) + bias
            o_ref[rows, cols] = (u * s).astype(o_ref.dtype)


def _gmlp(proj, gain, ws, bs, layer, a_width):
    t = proj.shape[0]
    groups = a_width // A_CHUNK
    tm = GMLP_CHUNKS * A_CHUNK
    return pl.pallas_call(
        _gmlp_body,
        grid=(t // tm,),
        in_specs=[
            pl.BlockSpec((tm, a_width), lambda c: (c, 0)),
            pl.BlockSpec((tm, a_width), lambda c: (c, 1)),
            pl.BlockSpec((None, 1, a_width), lambda c: (layer, 0, 0)),
            pl.BlockSpec((None, groups, A_CHUNK, A_CHUNK), lambda c: (layer, 0, 0, 0)),
            pl.BlockSpec((None, groups, A_CHUNK, 1), lambda c: (layer, 0, 0, 0)),
        ],
        out_specs=pl.BlockSpec((tm, a_width), lambda c: (c, 0)),
        out_shape=jax.ShapeDtypeStruct((t, a_width), BF16),
        compiler_params=_cparams(("parallel",)),
        name="gmlp",
    )(proj, proj, gain, ws, bs)


HG_CHUNK = 64
HG_SUB = 16


HG_HEADS = 8


def _hgrn2_body(q_ref, f_ref, i_ref, g_ref, lbl_ref, on_ref, o_ref, st_ref, *, layer):
    c = pl.program_id(2)

    @pl.when(c == 0)
    def _():
        st_ref[...] = jnp.zeros_like(st_ref)

    logits = lbl_ref[...]
    e = jnp.exp(logits - jnp.max(logits, axis=0, keepdims=True))
    probs = e / jnp.sum(e, axis=0, keepdims=True)
    lb = jnp.zeros((1, logits.shape[1]), F32)
    for r in range(1, layer + 1):
        lb = lb + probs[r:r + 1, :]

    n = HG_CHUNK
    hs = range(HG_HEADS)
    cols = [slice(hd * B_HEAD, (hd + 1) * B_HEAD) for hd in hs]
    tril, _ = _tri_masks(n)
    tri = jnp.where(tril, 1.0, 0.0).astype(BF16)

    st = [st_ref[hd] for hd in hs]
    v = [i_ref[:, c] for c in cols]
    f = [lb[:, c] + (1.0 - lb[:, c]) * _sigmoid(f_ref[:, c]) for c in cols]
    kf = [1.0 - x for x in f]
    qf = [_silu(q_ref[:, c]) for c in cols]
    cum = [_cumsum_rows(tri, jnp.log(jnp.maximum(x, B_MIN_F))) for x in f]
    last = [x[n - 1:n, :] for x in cum]

    v_b = [x.astype(BF16) for x in v]
    o = [_dot((qf[h] * jnp.exp(cum[h])).astype(BF16), st[h].astype(BF16), _NT) for h in hs]
    for h in hs:
        kend = (kf[h] * jnp.exp(last[h] - cum[h])).astype(BF16)
        st_ref[h] = st[h] * jnp.exp(last[h]) + _dot(v_b[h], kend, _TN)

    nsub = n // HG_SUB
    trow = lax.broadcasted_iota(jnp.int32, (HG_SUB, 1), 0)
    rows = [[] for _ in hs]
    for bi in range(nsub):
        lo, hi = bi * HG_SUB, (bi + 1) * HG_SUB
        acc = [jnp.zeros((HG_SUB, B_HEAD), F32) for _ in hs]
        if bi > 0:
            ref = [cum[h][lo - 1:lo, :] for h in hs]
            qh = [(qf[h][lo:hi] * jnp.exp(cum[h][lo:hi] - ref[h])).astype(BF16) for h in hs]
            kh = [(kf[h][:lo] * jnp.exp(ref[h] - cum[h][:lo])).astype(BF16) for h in hs]
            att = [_dot(qh[h], kh[h], _NT).astype(BF16) for h in hs]
            acc = [_dot(att[h], v_b[h][:lo]) for h in hs]
        for s in range(HG_SUB):
            for h in hs:
                c_b = cum[h][lo:hi]
                dec = jnp.exp(jnp.minimum(c_b - c_b[s:s + 1, :], 0.0))
                col = jnp.sum(qf[h][lo:hi] * dec * kf[h][lo + s:lo + s + 1, :], axis=-1, keepdims=True)
                col = jnp.where(trow >= s, col, 0.0)
                acc[h] = acc[h] + col * v[h][lo + s:lo + s + 1, :]
        for h in hs:
            rows[h].append(acc[h])

    for h in hs:
        out = o[h] + jnp.concatenate(rows[h], axis=0)
        o_ref[:, cols[h]] = (_rms(out, on_ref[:, cols[h]]) * _silu(g_ref[:, cols[h]])).astype(o_ref.dtype)


def _hgrn2(proj, lb_logits, onorm, layer, elayer, bsz, seq, a_width, b_width):
    t = proj.shape[0]
    heads = b_width // B_HEAD
    hgroups = heads // HG_HEADS
    nchunk = seq // HG_CHUNK
    width = HG_HEADS * B_HEAD
    off = 2 * a_width // width

    def col(which):
        return lambda b, h, c: (b * nchunk + c, off + which * hgroups + h)

    blk = (HG_CHUNK, width)
    depth = lb_logits.shape[0]
    return pl.pallas_call(
        functools.partial(_hgrn2_body, layer=layer),
        grid=(bsz, hgroups, nchunk),
        in_specs=[
            pl.BlockSpec(blk, col(0)),
            pl.BlockSpec(blk, col(1)),
            pl.BlockSpec(blk, col(2)),
            pl.BlockSpec(blk, col(3)),
            pl.BlockSpec((depth, width), lambda b, h, c: (0, h)),
            pl.BlockSpec((None, 1, width), lambda b, h, c: (elayer, 0, h)),
        ],
        out_specs=pl.BlockSpec(blk, lambda b, h, c: (b * nchunk + c, h)),
        out_shape=jax.ShapeDtypeStruct((t, b_width), BF16),
        scratch_shapes=[pltpu.VMEM((HG_HEADS, B_HEAD, B_HEAD), F32)],
        compiler_params=_cparams(("parallel", "parallel", "arbitrary")),
        name="hgrn2",
    )(proj, proj, proj, proj, lb_logits, onorm)


MIX_R, MIX_W, MIX_K, MIX_V, MIX_A, MIX_G = range(6)
RW_DECAY_SCALE = 0.6065306597126334
PROJ_SUB = 128


def _rwkv_proj_body(*refs, tm, seq, ntile, vres):
    if vres:
        (x_ref, xp_ref, g_ref, mix_ref, w_ref, w1_ref, a1_ref, g1_ref, l_ref, b_ref,
         v1_ref, v2_ref, v0_ref, vf_ref, o_ref, lag_ref, xs_ref, hs_ref, hv_ref) = refs
    else:
        (x_ref, xp_ref, g_ref, mix_ref, w_ref, w1_ref, a1_ref, g1_ref, l_ref, b_ref,
         o_ref, lag_ref, xs_ref, hs_ref) = refs
    i = pl.program_id(0)
    j = pl.program_id(1)

    @pl.when(j == 0)
    def _():
        g = g_ref[...]
        for r0 in range(0, tm, PROJ_SUB):
            rows = slice(r0, r0 + PROJ_SUB)
            hn = _rms(x_ref[rows, :], g)
            if r0 == 0:
                hp = _rms(xp_ref[...], g)[SUBLANES - 1:SUBLANES, :]
                hp = jnp.where((i * tm) % seq == 0, 0.0, hp)
            else:
                hp = _rms(x_ref[r0 - SUBLANES:r0, :], g)[SUBLANES - 1:SUBLANES, :]
            row = lax.broadcasted_iota(jnp.int32, hn.shape, 0)
            xx = jnp.where(row == 0, hp, pltpu.roll(hn, 1, 0)) - hn

            def mixed(m):
                return (hn + xx * mix_ref[m:m + 1, :]).astype(BF16)

            xs_ref[0, rows, :] = mixed(MIX_R)
            xs_ref[1, rows, :] = mixed(MIX_K)
            xv = mixed(MIX_V)
            xs_ref[2, rows, :] = xv
            hs_ref[0, rows, :] = jnp.tanh(_dot(mixed(MIX_W), w1_ref[...])).astype(BF16)
            hs_ref[1, rows, :] = _dot(mixed(MIX_A), a1_ref[...]).astype(BF16)
            hs_ref[2, rows, :] = _sigmoid(_dot(mixed(MIX_G), g1_ref[...])).astype(BF16)
            if vres:
                hv_ref[rows, :] = _dot(xv, v1_ref[...]).astype(BF16)

    third = j // ntile
    y = _dot(xs_ref[third], w_ref[...])
    z = b_ref[...] + _dot(hs_ref[third], l_ref[...])

    s = _sigmoid(z)
    lag_ref[...] = jnp.where(third == 2, z, s * jnp.where(third == 0, -RW_DECAY_SCALE, 1.0))
    if vres:
        @pl.when(third == 2)
        def _():
            mv = _sigmoid(v0_ref[...] + _dot(hv_ref[...], v2_ref[...]))
            o_ref[...] = y + (vf_ref[...] - y) * mv

        @pl.when(third != 2)
        def _():
            o_ref[...] = y
    else:
        o_ref[...] = y


def _rwkv_proj(h, g, mix, wcat, w1, a1, g1, lcat, bcat, layer, seq, vres=None, tm=1024, tn=512):
    t, d = h.shape
    tm = min(tm, t)
    ntile = d // tn
    rank = w1.shape[-1]

    def lora(lyr):
        return pl.BlockSpec((None, d, rank), lambda i, j: (lyr, 0, 0))

    def vtile(i, j):
        return jnp.maximum(j - 2 * ntile, 0)

    args = [h, h, g, mix, wcat, w1, a1, g1, lcat, bcat]
    in_specs = [
        pl.BlockSpec((tm, d), lambda i, j: (i, 0)),
        pl.BlockSpec((SUBLANES, d), lambda i, j: (jnp.maximum(i * (tm // SUBLANES) - 1, 0), 0)),
        pl.BlockSpec((1, d), lambda i, j: (0, 0)),
        pl.BlockSpec((None, mix.shape[1], d), lambda i, j: (layer, 0, 0)),
        pl.BlockSpec((None, d, tn), lambda i, j: (layer, 0, j)),
        lora(layer), lora(layer), lora(layer),
        pl.BlockSpec((None, rank, tn), lambda i, j: (layer, 0, j)),
        pl.BlockSpec((None, 1, tn), lambda i, j: (layer, 0, j)),
    ]
    scratch = [pltpu.VMEM((3, tm, d), BF16), pltpu.VMEM((3, tm, rank), BF16)]
    if vres is not None:
        v1, v2, v0, rkv_first, vl = vres
        vrank = v1.shape[-1]
        args += [v1, v2, v0, rkv_first]
        in_specs += [
            pl.BlockSpec((None, d, vrank), lambda i, j: (vl, 0, 0)),
            pl.BlockSpec((None, vrank, tn), lambda i, j: (vl, 0, vtile(i, j))),
            pl.BlockSpec((None, 1, tn), lambda i, j: (vl, 0, vtile(i, j))),
            pl.BlockSpec((tm, tn), lambda i, j: (i, 2 * ntile + vtile(i, j))),
        ]
        scratch.append(pltpu.VMEM((tm, vrank), BF16))
    tile = pl.BlockSpec((tm, tn), lambda i, j: (i, j))
    return pl.pallas_call(
        functools.partial(_rwkv_proj_body, tm=tm, seq=seq, ntile=ntile, vres=vres is not None),
        grid=(t // tm, 3 * ntile),
        in_specs=in_specs,
        out_specs=[tile, tile],
        out_shape=[jax.ShapeDtypeStruct((t, 3 * d), F32), jax.ShapeDtypeStruct((t, 3 * d), F32)],
        scratch_shapes=scratch,
        compiler_params=_cparams(("parallel", "arbitrary")),
        name="rwkv_proj",
    )(*args)


RW_CHUNK = 64


RW_PAIRS = 16


def _head_sum(x, m0):
    s0 = jnp.sum(jnp.where(m0, x, 0.0), axis=-1, keepdims=True)
    s1 = jnp.sum(jnp.where(m0, 0.0, x), axis=-1, keepdims=True)
    return jnp.where(m0, s0, s1)


def _rwkv_scan_body(r_ref, k_ref, v_ref, lw_ref, a_ref, g_ref, kk_ref, ka_ref, rk_ref, gg_ref, gb_ref,
                    o_ref, st_ref):
    c = pl.program_id(2)

    @pl.when(c == 0)
    def _():
        st_ref[...] = jnp.zeros_like(st_ref)

    n = RW_CHUNK
    pairs = range(RW_PAIRS)
    cols = [slice(p * LANES, (p + 1) * LANES) for p in pairs]
    lane = lax.broadcasted_iota(jnp.int32, (1, LANES), 1)
    m0 = lane < C_HEAD
    m1 = jnp.logical_not(m0)
    tril, _ = _tri_masks(n)
    tri = jnp.where(tril, 1.0, 0.0).astype(BF16)
    row = lax.broadcasted_iota(jnp.int32, (LANES, LANES), 0)
    colm = lax.broadcasted_iota(jnp.int32, (LANES, LANES), 1)
    bdiag = (row < C_HEAD) == (colm < C_HEAD)

    def hsel(x, hd):
        return jnp.where(m0 if hd == 0 else m1, x, jnp.zeros_like(x))

    st = [st_ref[p] for p in pairs]
    r = [r_ref[:, c] for c in cols]
    k = [k_ref[:, c] for c in cols]
    v = [v_ref[:, c] for c in cols]
    asig = [a_ref[:, c] for c in cols]

    cum = [_cumsum_rows(tri, lw_ref[:, c]) for c in cols]
    last = [x[n - 1:n, :] for x in cum]
    kkr = [k[p] * kk_ref[:, cols[p]] for p in pairs]
    kk = [x / jnp.maximum(jnp.sqrt(_head_sum(x * x, m0)), 1e-12) for x in kkr]
    kmod = [k[p] * (1.0 + (asig[p] - 1.0) * ka_ref[:, cols[p]]) for p in pairs]
    b = [kk[p] * asig[p] for p in pairs]
    bonus = [_head_sum(r[p] * kmod[p] * rk_ref[:, cols[p]], m0) * v[p] for p in pairs]
    rt = [(r[p] * jnp.exp(cum[p])).astype(BF16) for p in pairs]
    at = [(-kk[p] * jnp.exp(cum[p] - lw_ref[:, cols[p]])).astype(BF16) for p in pairs]
    einv = [jnp.exp(-x) for x in cum]
    bt = [(b[p] * einv[p]).astype(BF16) for p in pairs]
    kt = [(kmod[p] * einv[p]).astype(BF16) for p in pairs]
    eend = [jnp.exp(last[p] - cum[p]) for p in pairs]
    v_b = [x.astype(BF16) for x in v]
    st_b = [x.astype(BF16) for x in st]

    heads = [(p, hd) for p in pairs for hd in range(2)]
    _, stril = _tri_masks(n)
    eye = jnp.where(tril & jnp.logical_not(stril), 1.0, 0.0)
    at_h = [hsel(at[p], hd) for p, hd in heads]
    rt_h = [hsel(rt[p], hd) for p, hd in heads]
    pw = [jnp.where(stril, _dot(at_h[i], bt[p], _NT), 0.0) for i, (p, hd) in enumerate(heads)]
    t_inv = [eye + x for x in pw]
    fill = {}
    fillers = [
        lambda: fill.update(a_ak=[jnp.where(stril, _dot(at_h[i], kt[p], _NT), 0.0).astype(BF16)
                                  for i, (p, hd) in enumerate(heads)]),
        lambda: fill.update(a_rb=[jnp.where(tril, _dot(rt_h[i], bt[p], _NT), 0.0).astype(BF16)
                                  for i, (p, hd) in enumerate(heads)]),
        lambda: fill.update(a_rk=[jnp.where(tril, _dot(rt_h[i], kt[p], _NT), 0.0).astype(BF16)
                                  for i, (p, hd) in enumerate(heads)]),
        lambda: fill.update(av=[_dot(fill["a_ak"][i], v_b[p]) for i, (p, hd) in enumerate(heads)]),
        lambda: fill.update(y_v=[_dot(fill["a_rk"][i], v_b[p]) for i, (p, hd) in enumerate(heads)]),
    ]
    step = 2
    while step < n:
        pw_b = [x.astype(BF16) for x in pw]
        pw = [_dot(x, x) for x in pw_b]
        if fillers:
            fillers.pop(0)()
        t_inv = [t + _dot(t.astype(BF16), x.astype(BF16)) for t, x in zip(t_inv, pw)]
        step *= 2
    for filler in fillers:
        filler()
    t_inv = [x.astype(BF16) for x in t_inv]
    a_rb, av, y_v = fill["a_rb"], fill["av"], fill["y_v"]

    x = [(_dot(at[p], st_b[p], _NT) + jnp.where(m0, av[2 * p], av[2 * p + 1])).astype(BF16) for p in pairs]
    u = [jnp.where(m0, _dot(t_inv[2 * p], x[p]), _dot(t_inv[2 * p + 1], x[p])) for p in pairs]
    u_b = [t.astype(BF16) for t in u]
    y = [_dot(rt[p], st_b[p], _NT)
         + jnp.where(m0, _dot(a_rb[2 * p], u_b[p]) + y_v[2 * p], _dot(a_rb[2 * p + 1], u_b[p]) + y_v[2 * p + 1])
         for p in pairs]

    for p in pairs:
        uv = jnp.concatenate([u_b[p], v_b[p]], axis=0)
        bk = jnp.concatenate([(b[p] * eend[p]).astype(BF16), (kmod[p] * eend[p]).astype(BF16)], axis=0)
        st_ref[p] = st[p] * jnp.exp(last[p]) + jnp.where(bdiag, _dot(uv, bk, _TN), 0.0)

    inv_n = 1.0 / C_HEAD
    mu = [_head_sum(y[p], m0) * inv_n for p in pairs]
    ex2 = [_head_sum(y[p] * y[p], m0) * inv_n for p in pairs]
    dy = [y[p] - mu[p] for p in pairs]
    var = [ex2[p] - mu[p] * mu[p] for p in pairs]
    for p in pairs:
        yn = dy[p] * lax.rsqrt(var[p] + C_GN_EPS) * gg_ref[:, cols[p]] + gb_ref[:, cols[p]]
        o_ref[:, cols[p]] = ((yn + bonus[p]) * g_ref[:, cols[p]]).astype(o_ref.dtype)


def _rwkv_scan(rkv, lag, kk, ka, rk, gg, gb, layer, bsz, seq):
    t = rkv.shape[0]
    d = rkv.shape[1] // 3
    nchunk = seq // RW_CHUNK
    width = RW_PAIRS * LANES
    nblk = d // width

    def third(which):
        return pl.BlockSpec((RW_CHUNK, width), lambda b, p, c: (b * nchunk + c, which * nblk + p))

    blk = third(0)
    vec = pl.BlockSpec((None, 1, width), lambda b, p, c: (layer, 0, p))
    r, k, v, lw, a, g = rkv, rkv, rkv, lag, lag, lag
    return pl.pallas_call(
        _rwkv_scan_body,
        grid=(bsz, nblk, nchunk),
        in_specs=[third(0), third(1), third(2)] * 2 + [vec] * 5,
        out_specs=blk,
        out_shape=jax.ShapeDtypeStruct((t, d), BF16),
        scratch_shapes=[pltpu.VMEM((RW_PAIRS, LANES, LANES), F32)],
        compiler_params=_cparams(("parallel", "parallel", "arbitrary")),
        name="rwkv_scan",
    )(r, k, v, lw, a, g, kk, ka, rk, gg, gb)


def _pad_cols(w, n):
    return jnp.pad(w, ((0, 0), (0, 0), (0, n - w.shape[-1])))


def _pad_rows(w, n):
    return jnp.pad(w, ((0, 0), (0, n - w.shape[1]), (0, 0)))


def kernel(x, p, norms, final_norm, ffn_wg, ffn_wu, ffn_wd, ple_wp, ple_wg, e_w_in, e_w_out, a_vnorm, a_ws, a_bs, b_onorm, b_lb_logits, c_mix, c_wr, c_wk, c_wv, c_wo, c_w0, c_w1, c_w2, c_a0, c_a1, c_a2, c_g1, c_g2, c_kk, c_ka, c_rk, c_gn_g, c_gn_b, c_v0, c_v1, c_v2):
    bsz, seq, d = x.shape
    depth = p.shape[0]
    t = bsz * seq
    a_width = a_vnorm.shape[-1]
    b_width = b_onorm.shape[-1]

    bf = lambda w: w.astype(BF16)
    wg_b, wu_b, wd_b = ffn_wg, ffn_wu, ffn_wd
    ple_wp_b, ple_wg_b = ple_wp, ple_wg
    e_in_b = e_w_in
    e_out_b = bf(e_w_out)
    wo_b = bf(c_wo)
    rkv_b = bf(jnp.concatenate([c_wr, c_wk, c_wv], axis=-1))
    rank = c_g1.shape[-1]
    w1_b = bf(_pad_cols(c_w1, rank))
    a1_b = bf(_pad_cols(c_a1, rank))
    g1_b = bf(c_g1)
    lag_b = bf(jnp.concatenate([_pad_rows(c_w2, rank), _pad_rows(c_a2, rank), c_g2], axis=-1))
    bias_b = jnp.concatenate([c_w0, c_a0, jnp.zeros_like(c_w0)], axis=-1).reshape(c_w0.shape[0], 1, 3 * d)
    v1_b = bf(_pad_cols(c_v1, LORA_PAD))
    v2_b = bf(_pad_rows(c_v2, LORA_PAD))

    vec3 = lambda w: w.reshape(w.shape[0], 1, -1)
    a_vnorm3, b_onorm3 = vec3(a_vnorm), vec3(b_onorm)
    a_bs4 = a_bs.reshape(a_bs.shape + (1,))
    v0_3 = vec3(c_v0)
    kk3, ka3, rk3, gg3, gb3 = vec3(c_kk), vec3(c_ka), vec3(c_rk), vec3(c_gn_g), vec3(c_gn_b)
    fg = final_norm.reshape(1, d)

    h = x.reshape(t, d)
    p2 = p.reshape(depth, t, p.shape[-1])
    rkv_first = None
    for i in range(depth):
        j = i // 2
        h = _ffn(h, norms[i, 0].reshape(1, d), wg_b, wu_b, wd_b, i, 0)
        g1n = norms[i, 1].reshape(1, d)
        if i % 2 == 0:
            proj = _nmm(h, g1n, e_in_b, j)
            a_out = _gmlp(proj, a_vnorm3, a_ws, a_bs4, j, a_width)
            b_out = _hgrn2(proj, b_lb_logits, b_onorm3, i, j, bsz, seq, a_width, b_width)
            h = _mm2_res(h, a_out, b_out, e_out_b, j)
        else:
            vres = None if j == 0 else (v1_b, v2_b, v0_3, rkv_first, j - 1)
            rkv, lag = _rwkv_proj(h, g1n, c_mix, rkv_b, w1_b, a1_b, g1_b, lag_b, bias_b, j, seq, vres=vres)
            if j == 0:
                rkv_first = rkv
            y = _rwkv_scan(rkv, lag, kk3, ka3, rk3, gg3, gb3, j, bsz, seq)
            h = _mm_res(h, y, wo_b, j)
        h = _ffn(h, norms[i, 2].reshape(1, d), wg_b, wu_b, wd_b, i, 1)
        h = _ple(h, norms[i, 3].reshape(1, d), p2, ple_wg_b, ple_wp_b, fg, i, final=(i == depth - 1))
    return h.reshape(bsz, seq, d)
```
